```python
import math
import jax
import jax.numpy as jnp
from jax import lax
import numpy as np


D_MODEL = 1024
BATCH = 8
SEQ = 2048
DEPTH = 4

GRID_W = 64
CTX_LEN = 256
Q_BLOCK = 128
ROPE_BASE = 10000.0
EPS = 1e-6

A_HEADS = 4
A_HD = 64
A_VD = 2 * A_HD
B_HEADS = 8
B_NOPE = 64
B_ROPE = 32
B_QK = B_NOPE + B_ROPE
B_VD = 64
B_QLORA = 384
B_KVLORA = 256
POOL_WINDOWS = (2, 4, 8, 16)
POOL_GROUP = 128
C_WIDTH = len(POOL_WINDOWS) * POOL_GROUP
D_WIDTH = 512
N_BRANCH = 4
BRANCH_W = 512
D_FF = 4 * D_MODEL

IN_SIZES = (2 * A_HEADS * A_HD, 2 * A_HEADS * A_HD, A_HEADS * A_VD,
            B_QLORA, B_KVLORA, B_ROPE,
            C_WIDTH,
            D_WIDTH, D_WIDTH, D_WIDTH,
            N_BRANCH * D_MODEL)
D_IN = sum(IN_SIZES)

kernel_name = 'hybrid_parallel_gated_diffusion_block'


def rmsnorm(x, g):
    xf = x.astype(jnp.float32)
    y = xf * lax.rsqrt(jnp.mean(xf * xf, axis=-1, keepdims=True) + EPS)
    return (y * g.astype(jnp.float32)).astype(x.dtype)


def modulate(h, shift, scale):
    return h * (1.0 + scale) + shift


def rope_tables(n_tok, rot_dim):
    rows = n_tok // GRID_W
    row = jnp.repeat(jnp.arange(rows, dtype=jnp.float32), GRID_W)
    col = jnp.tile(jnp.arange(GRID_W, dtype=jnp.float32), rows)
    n_freq = rot_dim // 4
    inv = ROPE_BASE ** (-jnp.arange(n_freq, dtype=jnp.float32) / n_freq)
    ang = jnp.concatenate([row[:, None] * inv, col[:, None] * inv], axis=-1)
    return jnp.cos(ang), jnp.sin(ang)


def apply_rope(t, cos, sin):
    half = t.shape[-1] // 2
    tf = t.astype(jnp.float32)
    t1, t2 = tf[..., :half], tf[..., half:]
    return jnp.concatenate([t1 * cos - t2 * sin, t1 * sin + t2 * cos], axis=-1).astype(t.dtype)


def split_in(p):
    outs = []
    off = 0
    for n in IN_SIZES:
        outs.append(p[..., off:off + n])
        off += n
    return outs


def to_heads(t, n_heads):
    b, s, _ = t.shape
    return t.reshape(b, s, n_heads, -1).transpose(0, 2, 1, 3)


def from_heads(t):
    b, h, s, d = t.shape
    return t.transpose(0, 2, 1, 3).reshape(b, s, h * d)


def attend(qs, ks, v, coef):
    m, b, h, s, dk = qs.shape
    nb = s // Q_BLOCK
    qb = jnp.moveaxis(qs.reshape(m, b, h, nb, Q_BLOCK, dk), 3, 0)
    scale = dk ** -0.5

    def block(qi):
        sc = jnp.einsum('mbhqd,mbhkd->mbhqk', qi, ks, preferred_element_type=jnp.float32) * scale
        p = jnp.einsum('m,mbhqk->bhqk', coef, jax.nn.softmax(sc, axis=-1))
        return jnp.einsum('bhqk,bhkd->bhqd', p.astype(v.dtype), v)

    o = lax.map(block, qb)
    return jnp.moveaxis(o, 0, 2).reshape(b, h, s, v.shape[-1])


def diff_qkv(pq, pk, pv, gq, gk, rope):
    b, s, _ = pq.shape
    q = rmsnorm(pq.reshape(b, s, A_HEADS, 2, A_HD), gq).transpose(3, 0, 2, 1, 4)
    k = rmsnorm(pk.reshape(b, s, A_HEADS, 2, A_HD), gk).transpose(3, 0, 2, 1, 4)
    v = to_heads(pv, A_HEADS)
    if rope is not None:
        q = apply_rope(q, *rope)
        k = apply_rope(k, *rope)
    return q, k, v


def diff_post(o, g_sub, lam_init):
    return from_heads(rmsnorm(o, g_sub) * (1.0 - lam_init))


def mla_qkv(pcq, pckv, pkr, g_cq, w_uq, g_ckv, w_ukv, gq, gk, rope):
    b, s, _ = pcq.shape
    q = (rmsnorm(pcq, g_cq) @ w_uq).reshape(b, s, B_HEADS, B_QK)
    kv = (rmsnorm(pckv, g_ckv) @ w_ukv).reshape(b, s, B_HEADS, B_NOPE + B_VD)
    q = jnp.concatenate([rmsnorm(q[..., :B_NOPE], gq[:B_NOPE]),
                         rmsnorm(q[..., B_NOPE:], gq[B_NOPE:])], axis=-1).transpose(0, 2, 1, 3)
    k_nope = rmsnorm(kv[..., :B_NOPE], gk[:B_NOPE]).transpose(0, 2, 1, 3)
    k_rope = rmsnorm(pkr, gk[B_NOPE:])[:, None, :, :]
    v = kv[..., B_NOPE:].transpose(0, 2, 1, 3)
    if rope is not None:
        q = jnp.concatenate([q[..., :B_NOPE], apply_rope(q[..., B_NOPE:], *rope)], axis=-1)
        k_rope = apply_rope(k_rope, *rope)
    k = jnp.concatenate([k_nope, jnp.broadcast_to(k_rope, k_nope.shape[:3] + (B_ROPE,))], axis=-1)
    return q, k, v


def pool_mixer(u, w_pool, s_pool):
    b, s, _ = u.shape
    uf = u.astype(jnp.float32)
    cs = jnp.pad(jnp.cumsum(uf, axis=1), ((0, 0), (1, 0), (0, 0)))
    t = jnp.arange(s)
    outs = []
    for gi, w in enumerate(POOL_WINDOWS):
        lo = jnp.clip(t - w // 2, 0, s)
        hi = jnp.clip(t + w // 2, 0, s)
        sl = slice(gi * POOL_GROUP, (gi + 1) * POOL_GROUP)
        csg = cs[..., sl]
        mean = (csg[:, hi] - csg[:, lo]) / (hi - lo).astype(jnp.float32)[:, None]
        outs.append(mean - uf[..., sl])
    d = jnp.stack(outs, axis=2).astype(u.dtype)
    y = jnp.einsum('bsgc,gcd->bsgd', d, w_pool).reshape(b, s, C_WIDTH)
    return y * s_pool


def conv_mixer(pb, pc, px, w_conv):
    u = pc * px
    up = jnp.pad(u, ((0, 0), (1, 1), (0, 0)))
    y = up[:, :-2] * w_conv[0] + up[:, 1:-1] * w_conv[1] + up[:, 2:] * w_conv[2]
    return pb * y


def merge_branches(ys, gate_logits, w_branch, w_o):
    b, s, _ = gate_logits.shape
    y = jnp.stack(ys, axis=2)
    proj = jnp.einsum('bsnc,ncd->bsnd', y, w_branch)
    g = jax.nn.sigmoid(gate_logits.reshape(b, s, N_BRANCH, D_MODEL))
    return jnp.einsum('bsnd,bsnd->bsd', g, proj) @ w_o


def squared_relu_mlp(h, w1, w2):
    a = jax.nn.relu(h @ w1)
    return (a * a) @ w2


def setup_inputs(seed: int = 0) -> dict:
    key = jax.random.key(seed)
    ks = jax.random.split(key, 26)
    f32 = jnp.float32

    def nrm(k, shape, scale):
        return jax.random.normal(k, shape, f32) * scale

    def gain(k, shape, noise=0.05):
        return 1.0 + nrm(k, shape, noise)

    return {
        'x': nrm(ks[0], (BATCH, SEQ, D_MODEL), 1.0),
        'c': nrm(ks[1], (BATCH, D_MODEL), 1.0),
        'ctx': nrm(ks[2], (BATCH, CTX_LEN, D_MODEL), 1.0),
        'c_ctx': nrm(ks[3], (D_MODEL,), 1.0),
        'w_mod': nrm(ks[4], (DEPTH, D_MODEL, 6 * D_MODEL), D_MODEL ** -0.5),
        'b_mod': nrm(ks[5], (DEPTH, 6 * D_MODEL), 0.02),
        'g_norm1': gain(ks[6], (DEPTH, D_MODEL)),
        'g_norm2': gain(ks[7], (DEPTH, D_MODEL)),
        'w_in': nrm(ks[8], (DEPTH, D_MODEL, D_IN), D_MODEL ** -0.5),
        'gq_a': gain(ks[9], (DEPTH, A_HD)),
        'gk_a': gain(ks[10], (DEPTH, A_HD)),
        'lam_a': nrm(ks[11], (DEPTH, 4, A_HD), 0.1),
        'g_sub_a': gain(ks[12], (DEPTH, A_VD)),
        'g_cq': gain(ks[13], (DEPTH, B_QLORA)),
        'w_uq': nrm(ks[14], (DEPTH, B_QLORA, B_HEADS * B_QK), B_QLORA ** -0.5),
        'g_ckv': gain(ks[15], (DEPTH, B_KVLORA)),
        'w_ukv': nrm(ks[16], (DEPTH, B_KVLORA, B_HEADS * (B_NOPE + B_VD)), B_KVLORA ** -0.5),
        'gq_b': gain(ks[17], (DEPTH, B_QK)),
        'gk_b': gain(ks[18], (DEPTH, B_QK)),
        'w_pool': nrm(ks[19], (DEPTH, len(POOL_WINDOWS), POOL_GROUP, POOL_GROUP), POOL_GROUP ** -0.5),
        's_pool': gain(ks[20], (DEPTH, C_WIDTH), 0.1),
        'w_conv': nrm(ks[21], (DEPTH, 3, D_WIDTH), 3 ** -0.5),
        'w_branch': nrm(ks[22], (DEPTH, N_BRANCH, BRANCH_W, D_MODEL), BRANCH_W ** -0.5),
        'w_o': nrm(ks[23], (DEPTH, D_MODEL, D_MODEL), D_MODEL ** -0.5),
        'w_ff1': nrm(ks[24], (DEPTH, D_MODEL, D_FF), D_MODEL ** -0.5),
        'w_ff2': nrm(ks[25], (DEPTH, D_FF, D_MODEL), D_FF ** -0.5),
    }


def reference(x, c, ctx, c_ctx, w_mod, b_mod, g_norm1, g_norm2, w_in, gq_a, gk_a, lam_a, g_sub_a,
              g_cq, w_uq, g_ckv, w_ukv, gq_b, gk_b, w_pool, s_pool, w_conv, w_branch, w_o,
              w_ff1, w_ff2):
    n_lat = x.shape[1]
    rope_a = rope_tables(n_lat, A_HD)
    rope_b = rope_tables(n_lat, B_ROPE)
    coef_b = jnp.ones((1,), jnp.float32)
    xc = ctx
    for l in range(DEPTH):
        last = l == DEPTH - 1
        lam_init = 0.8 - 0.6 * math.exp(-0.3 * l)
        mod_x = jnp.split((jax.nn.silu(c) @ w_mod[l] + b_mod[l])[:, None, :], 6, axis=-1)
        mod_c = jnp.split(jax.nn.silu(c_ctx) @ w_mod[l] + b_mod[l], 6, axis=-1)

        px = split_in(modulate(rmsnorm(x, g_norm1[l]), mod_x[0], mod_x[1]) @ w_in[l])
        pc = split_in(modulate(rmsnorm(xc, g_norm1[l]), mod_c[0], mod_c[1]) @ w_in[l])

        la = lam_a[l].astype(jnp.float32)
        lam = jnp.exp(jnp.sum(la[0] * la[1])) - jnp.exp(jnp.sum(la[2] * la[3])) + lam_init
        coef_a = jnp.stack([jnp.ones((), jnp.float32), -lam])
        qa_x, ka_x, va_x = diff_qkv(px[0], px[1], px[2], gq_a[l], gk_a[l], rope_a)
        qa_c, ka_c, va_c = diff_qkv(pc[0], pc[1], pc[2], gq_a[l], gk_a[l], None)
        ka_all = jnp.concatenate([ka_c, ka_x], axis=3)
        va_all = jnp.concatenate([va_c, va_x], axis=2)
        ya_x = diff_post(attend(qa_x, ka_all, va_all, coef_a), g_sub_a[l], lam_init)

        qb_x, kb_x, vb_x = mla_qkv(px[3], px[4], px[5], g_cq[l], w_uq[l], g_ckv[l], w_ukv[l],
                                   gq_b[l], gk_b[l], rope_b)
        qb_c, kb_c, vb_c = mla_qkv(pc[3], pc[4], pc[5], g_cq[l], w_uq[l], g_ckv[l], w_ukv[l],
                                   gq_b[l], gk_b[l], None)
        kb_all = jnp.concatenate([kb_c, kb_x], axis=2)
        vb_all = jnp.concatenate([vb_c, vb_x], axis=2)
        yb_x = from_heads(attend(qb_x[None], kb_all[None], vb_all, coef_b))

        mix_x = merge_branches([ya_x, yb_x,
                                pool_mixer(px[6], w_pool[l], s_pool[l]),
                                conv_mixer(px[7], px[8], px[9], w_conv[l])],
                               px[10], w_branch[l], w_o[l])
        x = x + mod_x[2] * mix_x
        x = x + mod_x[5] * squared_relu_mlp(modulate(rmsnorm(x, g_norm2[l]), mod_x[3], mod_x[4]),
                                            w_ff1[l], w_ff2[l])

        if not last:
            ya_c = diff_post(attend(qa_c, ka_c, va_c, coef_a), g_sub_a[l], lam_init)
            yb_c = from_heads(attend(qb_c[None], kb_c[None], vb_c, coef_b))
            mix_c = merge_branches([ya_c, yb_c,
                                    pool_mixer(pc[6], w_pool[l], s_pool[l]),
                                    conv_mixer(pc[7], pc[8], pc[9], w_conv[l])],
                                   pc[10], w_branch[l], w_o[l])
            xc = xc + mod_c[2] * mix_c
            xc = xc + mod_c[5] * squared_relu_mlp(modulate(rmsnorm(xc, g_norm2[l]), mod_c[3], mod_c[4]),
                                                  w_ff1[l], w_ff2[l])
    return x
```

```python
import functools
import math

import numpy as np
import jax
import jax.numpy as jnp
from jax import lax
from jax.experimental import pallas as pl
from jax.experimental.pallas import tpu as pltpu

F32 = jnp.float32
BF16 = jnp.bfloat16

D_MODEL = 1024
GRID_W = 64
ROPE_BASE = 10000.0
EPS = 1e-6

A_HEADS = 4
A_HD = 64
A_VD = 128
B_HEADS = 8
B_NOPE = 64
B_ROPE = 32
B_QK = B_NOPE + B_ROPE
B_VD = 64
B_QLORA = 384
B_KVLORA = 256
POOL_WINDOWS = (2, 4, 8, 16)
POOL_GROUP = 128
POOL_HALO = 8
BRANCH_W = 512
N_BRANCH = 4
D_FF = 4 * D_MODEL
N_MOD = 6

LANES = 128

GATE_OFF = 0
QA_OFF = 4096
KA_OFF = 4608
VA_OFF = 5120
U_OFF = 5632
PB_OFF = 6144
PC_OFF = 6656
PXX_OFF = 7168
CKV_OFF = 7680
KR_OFF = 7936
CQ_OFF = 8064
D_IN_P = CQ_OFF + B_QLORA
IN_CHUNK = 768

PX_DTYPE = jnp.float32
VMEM_LIMIT = 56 * 1024 * 1024


def _dot(a, b):
    return jnp.dot(a, b, preferred_element_type=F32)


def _dot_nt(a, b):
    return lax.dot_general(a, b, (((1,), (1,)), ((), ())), preferred_element_type=F32)


def _lane_iota(n=LANES):
    return lax.broadcasted_iota(jnp.int32, (1, n), 1)


def _rope(t, cos, s_left, s_right, half):
    n = t.shape[-1]
    return (t * cos + pltpu.roll(t, n - half, 1) * s_left
            + pltpu.roll(t, half, 1) * s_right)


def _const_spec(shape):
    nd = len(shape)
    return pl.BlockSpec(shape, lambda *_: (0,) * nd, pipeline_mode=pl.Buffered(1))


def _mod_kernel(c_ref, w_ref, b_ref, o_ref):
    c = c_ref[...]
    h = (c * jax.nn.sigmoid(c)).astype(BF16)
    o_ref[...] = _dot(h, w_ref[...].astype(BF16)) + b_ref[...]


def _modulation(cc, w_mod, b_mod):
    depth, d, n = w_mod.shape
    rows = cc.shape[0]
    tn = 1536
    return pl.pallas_call(
        _mod_kernel,
        grid=(depth, n // tn),
        in_specs=[
            pl.BlockSpec((rows, d), lambda l, j: (0, 0)),
            pl.BlockSpec((None, d, tn), lambda l, j: (l, 0, j)),
            pl.BlockSpec((None, 1, tn), lambda l, j: (l, 0, j)),
        ],
        out_specs=pl.BlockSpec((None, rows, tn), lambda l, j: (l, 0, j)),
        out_shape=jax.ShapeDtypeStruct((depth, rows, n), F32),
        compiler_params=pltpu.CompilerParams(
            dimension_semantics=("parallel", "parallel"), vmem_limit_bytes=VMEM_LIMIT),
        name="modulation",
    )(cc, w_mod, b_mod.reshape(depth, 1, n))


def _inproj_kernel(x_ref, mod_ref, g_ref, w_ref, o_ref):
    x = x_ref[0]
    y = x * lax.rsqrt(jnp.mean(x * x, axis=-1, keepdims=True) + EPS) * g_ref[...]
    shift = mod_ref[:, 0:D_MODEL]
    scale = mod_ref[:, D_MODEL:2 * D_MODEL]
    h = (y * (1.0 + scale) + shift).astype(BF16)
    for c0 in range(0, D_IN_P, IN_CHUNK):
        o_ref[0, :, c0:c0 + IN_CHUNK] = _dot(h, w_ref[:, c0:c0 + IN_CHUNK]).astype(o_ref.dtype)


def _inproj(x_all, mods_l, g1, w_in_p, tq):
    b, t, d = x_all.shape
    nt = t // tq
    return pl.pallas_call(
        _inproj_kernel,
        grid=(b, nt),
        in_specs=[
            pl.BlockSpec((1, tq, d), lambda i, j: (i, j, 0)),
            pl.BlockSpec((None, 1, N_MOD * d), lambda i, j: (jnp.where(j == 0, b, i), 0, 0)),
            _const_spec((1, d)),
            _const_spec((d, D_IN_P)),
        ],
        out_specs=pl.BlockSpec((1, tq, D_IN_P), lambda i, j: (i, j, 0)),
        out_shape=jax.ShapeDtypeStruct((b, t, D_IN_P), PX_DTYPE),
        compiler_params=pltpu.CompilerParams(
            dimension_semantics=("parallel", "parallel"), vmem_limit_bytes=VMEM_LIMIT),
        name="inproj",
    )(x_all, mods_l, g1, w_in_p)


def _norm_halves(x, lo):
    x2 = x * x
    s_lo = jnp.sum(jnp.where(lo, x2, 0.0), axis=-1, keepdims=True)
    s_hi = jnp.sum(jnp.where(lo, 0.0, x2), axis=-1, keepdims=True)
    inv = jnp.where(lo, lax.rsqrt(s_lo * (1.0 / A_HD) + EPS), lax.rsqrt(s_hi * (1.0 / A_HD) + EPS))
    return x * inv


def _softmax_pv(s, vx):
    p = jnp.exp(s - jnp.max(s, axis=-1, keepdims=True))
    return _dot(p.astype(BF16), vx)


def _attn_a_kernel(q_ref, k_ref, v_ref, cq_ref, slq_ref, srq_ref, ck_ref, slk_ref, srk_ref,
                   gq_ref, gk_ref, lam_ref, gsub_ref, o_ref, k_s, vx_s,
                   *, ctx, q_off, lam_init):
    qi = pl.program_id(2) + q_off
    lo = _lane_iota() < A_HD
    t_all = k_s.shape[0]

    @pl.when(pl.program_id(2) == 0)
    def _prep():
        kn = _norm_halves(k_ref[0].astype(F32), lo) * gk_ref[...]
        kn = _rope(kn, ck_ref[...], slk_ref[...], srk_ref[...], A_HD // 2)
        k_s[...] = kn.astype(BF16)
        vx_s[:, 0:A_VD] = v_ref[0].astype(BF16)
        vx_s[:, A_VD:2 * A_VD] = jnp.ones((t_all, A_VD), BF16)

    qn = _norm_halves(q_ref[0].astype(F32), lo) * (gq_ref[...] * (A_HD ** -0.5))
    qn = _rope(qn, cq_ref[...], slq_ref[...], srq_ref[...], A_HD // 2)
    q1 = jnp.where(lo, qn, 0.0).astype(BF16)
    q2 = jnp.where(lo, 0.0, qn).astype(BF16)

    la = lam_ref[...].astype(F32)
    lam = (jnp.exp(jnp.sum(la[0:1] * la[1:2], axis=-1, keepdims=True))
           - jnp.exp(jnp.sum(la[2:3] * la[3:4], axis=-1, keepdims=True)) + lam_init)

    def attend(klen):
        k = k_s[0:klen, :]
        vx = vx_s[0:klen, :]
        r1 = _softmax_pv(_dot_nt(q1, k), vx)
        r2 = _softmax_pv(_dot_nt(q2, k), vx)
        o = (r1[:, 0:A_VD] / r1[:, A_VD:A_VD + 1]
             - lam * (r2[:, 0:A_VD] / r2[:, A_VD:A_VD + 1]))
        y = o * lax.rsqrt(jnp.mean(o * o, axis=-1, keepdims=True) + EPS)
        o_ref[0] = (y * gsub_ref[...] * (1.0 - lam_init)).astype(o_ref.dtype)

    @pl.when(qi == 0)
    def _ctx():
        attend(ctx)

    @pl.when(qi != 0)
    def _lat():
        attend(t_all)


def _attn_a(px, rope_a, gq2, gk2, lam_l, gsub, ctx, q_off, lam_init):
    b, t, _ = px.shape
    tq = ctx
    nq = t // tq - q_off
    cos, sl, sr = rope_a
    qmap = lambda i, h, j: (j + q_off, 0)
    kmap = lambda i, h, j: (0, 0)
    return pl.pallas_call(
        functools.partial(_attn_a_kernel, ctx=ctx, q_off=q_off, lam_init=lam_init),
        grid=(b, A_HEADS, nq),
        in_specs=[
            pl.BlockSpec((1, tq, LANES), lambda i, h, j: (i, j + q_off, QA_OFF // LANES + h)),
            pl.BlockSpec((1, t, LANES), lambda i, h, j: (i, 0, KA_OFF // LANES + h)),
            pl.BlockSpec((1, t, LANES), lambda i, h, j: (i, 0, VA_OFF // LANES + h)),
            pl.BlockSpec((tq, LANES), qmap), pl.BlockSpec((tq, LANES), qmap),
            pl.BlockSpec((tq, LANES), qmap),
            pl.BlockSpec((t, LANES), kmap), pl.BlockSpec((t, LANES), kmap),
            pl.BlockSpec((t, LANES), kmap),
            pl.BlockSpec((1, LANES), kmap), pl.BlockSpec((1, LANES), kmap),
            pl.BlockSpec((4, A_HD), kmap), pl.BlockSpec((1, LANES), kmap),
        ],
        out_specs=pl.BlockSpec((1, tq, LANES), lambda i, h, j: (i, j + q_off, h)),
        out_shape=jax.ShapeDtypeStruct((b, t, BRANCH_W), BF16),
        scratch_shapes=[pltpu.VMEM((t, LANES), BF16), pltpu.VMEM((t, 2 * A_VD), BF16)],
        compiler_params=pltpu.CompilerParams(
            dimension_semantics=("parallel", "parallel", "arbitrary"),
            vmem_limit_bytes=VMEM_LIMIT),
        name="attn_a",
    )(px, px, px, cos, sl, sr, cos, sl, sr, gq2, gk2, lam_l, gsub)


def _attn_b_kernel(cq_ref, ckv_ref, kr_ref, gcq_ref, gckv_ref, wuq_ref, wukv_ref,
                   cosq_ref, slq_ref, srq_ref, cosk_ref, slk_ref, srk_ref,
                   gq_ref, gk_ref, o_ref, k_s, vx_s, *, ctx, q_off):
    qi = pl.program_id(2) + q_off
    lane = _lane_iota()
    nope = lane < B_NOPE
    t_all = k_s.shape[1]

    @pl.when(pl.program_id(2) == 0)
    def _prep():
        ckv = ckv_ref[0].astype(F32)
        ckvn = (ckv * lax.rsqrt(jnp.mean(ckv * ckv, axis=-1, keepdims=True) + EPS)
                * gckv_ref[...]).astype(BF16)
        kr = kr_ref[0].astype(F32)
        krn = kr * lax.rsqrt(jnp.sum(kr * kr, axis=-1, keepdims=True) * (1.0 / B_ROPE) + EPS)
        gk_rope = jnp.where(nope, 0.0, gk_ref[...])
        krn = _rope(krn * gk_rope, cosk_ref[...], slk_ref[...], srk_ref[...], B_ROPE // 2)
        gk_nope = jnp.where(nope, gk_ref[...], 0.0)
        for hh in range(2):
            kv = _dot(ckvn, wukv_ref[:, hh * LANES:(hh + 1) * LANES])
            ss = jnp.sum(jnp.where(nope, kv * kv, 0.0), axis=-1, keepdims=True)
            kn = kv * lax.rsqrt(ss * (1.0 / B_NOPE) + EPS) * gk_nope
            k_s[hh] = (kn + krn).astype(BF16)
            vx_s[hh] = jnp.where(nope, 1.0, kv).astype(BF16)

    cq = cq_ref[0].astype(F32)
    cqn = (cq * lax.rsqrt(jnp.mean(cq * cq, axis=-1, keepdims=True) + EPS)
           * gcq_ref[...]).astype(BF16)
    qs = []
    for hh in range(2):
        q = _dot(cqn, wuq_ref[:, hh * LANES:(hh + 1) * LANES])
        q2 = q * q
        s_n = jnp.sum(jnp.where(nope, q2, 0.0), axis=-1, keepdims=True)
        s_r = jnp.sum(jnp.where(nope, 0.0, q2), axis=-1, keepdims=True)
        inv = jnp.where(nope, lax.rsqrt(s_n * (1.0 / B_NOPE) + EPS),
                        lax.rsqrt(s_r * (1.0 / B_ROPE) + EPS))
        qn = q * inv * (gq_ref[...] * (B_QK ** -0.5))
        qn = _rope(qn, cosq_ref[...], slq_ref[...], srq_ref[...], B_ROPE // 2)
        qs.append(qn.astype(BF16))

    def attend(klen):
        outs = []
        for hh in range(2):
            r = _softmax_pv(_dot_nt(qs[hh], k_s[hh, 0:klen, :]), vx_s[hh, 0:klen, :])
            outs.append(r / r[:, 0:1])
        o = jnp.where(nope, pltpu.roll(outs[0], B_VD, 1), outs[1])
        o_ref[0] = o.astype(o_ref.dtype)

    @pl.when(qi == 0)
    def _ctx():
        attend(ctx)

    @pl.when(qi != 0)
    def _lat():
        attend(t_all)


def _attn_b(px, rope_b, gcq, gckv, wuq_p, wukv, gq_v, gk_v, ctx, q_off):
    b, t, _ = px.shape
    tq = ctx
    nq = t // tq - q_off
    cos, sl, sr = rope_b
    qmap = lambda i, h, j: (j + q_off, 0)
    kmap = lambda i, h, j: (0, 0)
    return pl.pallas_call(
        functools.partial(_attn_b_kernel, ctx=ctx, q_off=q_off),
        grid=(b, B_HEADS // 2, nq),
        in_specs=[
            pl.BlockSpec((1, tq, B_QLORA), lambda i, h, j: (i, j + q_off, CQ_OFF // B_QLORA)),
            pl.BlockSpec((1, t, B_KVLORA), lambda i, h, j: (i, 0, CKV_OFF // B_KVLORA)),
            pl.BlockSpec((1, t, LANES), lambda i, h, j: (i, 0, KR_OFF // LANES)),
            pl.BlockSpec((1, B_QLORA), kmap), pl.BlockSpec((1, B_KVLORA), kmap),
            pl.BlockSpec((B_QLORA, 2 * LANES), lambda i, h, j: (0, h)),
            pl.BlockSpec((B_KVLORA, 2 * LANES), lambda i, h, j: (0, h)),
            pl.BlockSpec((tq, LANES), qmap), pl.BlockSpec((tq, LANES), qmap),
            pl.BlockSpec((tq, LANES), qmap),
            pl.BlockSpec((t, LANES), kmap), pl.BlockSpec((t, LANES), kmap),
            pl.BlockSpec((t, LANES), kmap),
            pl.BlockSpec((1, LANES), kmap), pl.BlockSpec((1, LANES), kmap),
        ],
        out_specs=pl.BlockSpec((1, tq, LANES), lambda i, h, j: (i, j + q_off, h)),
        out_shape=jax.ShapeDtypeStruct((b, t, BRANCH_W), BF16),
        scratch_shapes=[pltpu.VMEM((2, t, LANES), BF16), pltpu.VMEM((2, t, LANES), BF16)],
        compiler_params=pltpu.CompilerParams(
            dimension_semantics=("parallel", "parallel", "arbitrary"),
            vmem_limit_bytes=VMEM_LIMIT),
        name="attn_b",
    )(px, px, px, gcq, gckv, wuq_p, wukv, cos, sl, sr, cos, sl, sr, gq_v, gk_v)


def _mix_cd_kernel(u_ref, pb_ref, pc_ref, pxx_ref, wp_ref, sp_ref, wc_ref,
                   yc_ref, yd_ref, pad_s, *, segments):
    g = pl.program_id(1)
    zeros_halo = jnp.zeros((POOL_HALO, LANES), F32)
    wc = wc_ref[...]

    for start, length in segments:
        rows = lax.broadcasted_iota(jnp.int32, (length, 1), 0)
        u = u_ref[0, start:start + length, :].astype(F32)
        pad_s[0:POOL_HALO, :] = zeros_halo
        pad_s[POOL_HALO:POOL_HALO + length, :] = u
        pad_s[POOL_HALO + length:2 * POOL_HALO + length, :] = zeros_halo

        for gi, w in enumerate(POOL_WINDOWS):
            @pl.when(g == gi)
            def _pool(w=w):
                acc = pad_s[POOL_HALO - w // 2:POOL_HALO - w // 2 + length, :]
                for j in range(1 - w // 2, w // 2):
                    acc = acc + pad_s[POOL_HALO + j:POOL_HALO + j + length, :]
                cnt = (jnp.minimum(rows + w // 2, length) - jnp.maximum(rows - w // 2, 0)).astype(F32)
                d = acc / cnt - u
                y = _dot(d.astype(BF16), wp_ref[...]) * sp_ref[...]
                yc_ref[0, start:start + length, :] = y.astype(yc_ref.dtype)

        uu = pc_ref[0, start:start + length, :].astype(F32) * pxx_ref[0, start:start + length, :].astype(F32)
        pad_s[POOL_HALO:POOL_HALO + length, :] = uu
        y = (pad_s[POOL_HALO - 1:POOL_HALO - 1 + length, :] * wc[0:1]
             + uu * wc[1:2]
             + pad_s[POOL_HALO + 1:POOL_HALO + 1 + length, :] * wc[2:3])
        yd_ref[0, start:start + length, :] = (
            pb_ref[0, start:start + length, :].astype(F32) * y).astype(yd_ref.dtype)


def _mix_cd(px, w_pool_l, s_pool_l, w_conv_l, ctx):
    b, t, _ = px.shape
    segments = ((0, ctx), (ctx, t - ctx))
    n_g = len(POOL_WINDOWS)

    def col(off):
        return pl.BlockSpec((1, t, LANES), lambda i, g: (i, 0, off // LANES + g))

    out_spec = pl.BlockSpec((1, t, LANES), lambda i, g: (i, 0, g))
    return pl.pallas_call(
        functools.partial(_mix_cd_kernel, segments=segments),
        grid=(b, n_g),
        in_specs=[
            col(U_OFF), col(PB_OFF), col(PC_OFF), col(PXX_OFF),
            pl.BlockSpec((None, POOL_GROUP, POOL_GROUP), lambda i, g: (g, 0, 0)),
            pl.BlockSpec((1, LANES), lambda i, g: (0, g)),
            pl.BlockSpec((3, LANES), lambda i, g: (0, g)),
        ],
        out_specs=[out_spec, out_spec],
        out_shape=[jax.ShapeDtypeStruct((b, t, BRANCH_W), BF16)] * 2,
        scratch_shapes=[pltpu.VMEM((t - ctx + 2 * POOL_HALO, LANES), F32)],
        compiler_params=pltpu.CompilerParams(
            dimension_semantics=("parallel", "parallel"), vmem_limit_bytes=VMEM_LIMIT),
        name="mix_cd",
    )(px, px, px, px, w_pool_l, s_pool_l, w_conv_l)


def _merge_ffn_kernel(ya_ref, yb_ref, yc_ref, yd_ref, gate_ref, x_ref, mod_ref,
                      wb_ref, wo_ref, g2_ref, w1_ref, w2_ref, o_ref):
    d = D_MODEL
    merged = None
    for n, y_ref in enumerate((ya_ref, yb_ref, yc_ref, yd_ref)):
        proj = _dot(y_ref[0], wb_ref[n])
        gate = jax.nn.sigmoid(gate_ref[0, :, n * d:(n + 1) * d].astype(F32))
        merged = gate * proj if merged is None else merged + gate * proj
    mix = _dot(merged.astype(BF16), wo_ref[...])
    x1 = x_ref[0] + mod_ref[:, 2 * d:3 * d] * mix

    y = x1 * lax.rsqrt(jnp.mean(x1 * x1, axis=-1, keepdims=True) + EPS) * g2_ref[...]
    h = (y * (1.0 + mod_ref[:, 4 * d:5 * d]) + mod_ref[:, 3 * d:4 * d]).astype(BF16)
    f = None
    for c0 in range(0, D_FF, d):
        a = jnp.maximum(_dot(h, w1_ref[:, c0:c0 + d]), 0.0)
        part = _dot((a * a).astype(BF16), w2_ref[c0:c0 + d, :])
        f = part if f is None else f + part
    o_ref[0] = x1 + mod_ref[:, 5 * d:6 * d] * f


def _merge_ffn(ys, px, x_all, mods_l, wb, wo, g2, w1, w2, tq, q_off):
    b, t, d = x_all.shape
    nq = t // tq - q_off
    row = lambda i, j: (i, j + q_off, 0)
    y_spec = pl.BlockSpec((1, tq, BRANCH_W), row)
    if q_off == 0:
        out_shape = jax.ShapeDtypeStruct((b, t, d), F32)
        out_spec = pl.BlockSpec((1, tq, d), row)
        aliases = {5: 0}
    else:
        out_shape = jax.ShapeDtypeStruct((b, t - q_off * tq, d), F32)
        out_spec = pl.BlockSpec((1, tq, d), lambda i, j: (i, j, 0))
        aliases = {}
    return pl.pallas_call(
        _merge_ffn_kernel,
        grid=(b, nq),
        in_specs=[
            y_spec, y_spec, y_spec, y_spec,
            pl.BlockSpec((1, tq, N_BRANCH * d), lambda i, j: (i, j + q_off, GATE_OFF // (N_BRANCH * d))),
            pl.BlockSpec((1, tq, d), row),
            pl.BlockSpec((None, 1, N_MOD * d), lambda i, j: (jnp.where(j + q_off == 0, b, i), 0, 0)),
            _const_spec((N_BRANCH, BRANCH_W, d)),
            _const_spec((d, d)),
            _const_spec((1, d)),
            _const_spec((d, D_FF)),
            _const_spec((D_FF, d)),
        ],
        out_specs=out_spec,
        out_shape=out_shape,
        input_output_aliases=aliases,
        compiler_params=pltpu.CompilerParams(
            dimension_semantics=("parallel", "parallel"), vmem_limit_bytes=VMEM_LIMIT),
        name="merge_ffn",
    )(*ys, px, x_all, mods_l, wb, wo, g2, w1, w2)


def _rope_tables(ctx, n_lat, rot_dim, lane_lo, period, total=LANES):
    rows = n_lat // GRID_W
    row = np.repeat(np.arange(rows, dtype=np.float64), GRID_W)
    col = np.tile(np.arange(GRID_W, dtype=np.float64), rows)
    n_freq = rot_dim // 4
    inv = ROPE_BASE ** (-np.arange(n_freq, dtype=np.float64) / n_freq)
    inv = inv.astype(np.float32).astype(np.float64)
    ang = np.concatenate([row[:, None] * inv, col[:, None] * inv], axis=-1)
    ang = ang.astype(np.float32).astype(np.float64)
    half = rot_dim // 2
    t = ctx + n_lat
    cos = np.ones((t, total), np.float32)
    s_left = np.zeros((t, total), np.float32)
    s_right = np.zeros((t, total), np.float32)
    starts = [lane_lo] if period == 0 else list(range(lane_lo, total, period))
    for s0 in starts:
        cos[ctx:, s0:s0 + half] = np.cos(ang)
        cos[ctx:, s0 + half:s0 + rot_dim] = np.cos(ang)
        s_left[ctx:, s0:s0 + half] = -np.sin(ang)
        s_right[ctx:, s0 + half:s0 + rot_dim] = np.sin(ang)
    return jnp.asarray(cos), jnp.asarray(s_left), jnp.asarray(s_right)


def _relayout_w_in(w_in):
    depth, d, _ = w_in.shape
    sizes = (512, 512, 512, B_QLORA, B_KVLORA, B_ROPE, 512, 512, 512, 512, N_BRANCH * D_MODEL)
    offs = np.concatenate([[0], np.cumsum(sizes)])
    qa, ka, va, cq, ckv, kr, u, pb, pc, pxx, gates = [
        w_in[:, :, offs[i]:offs[i + 1]] for i in range(len(sizes))]
    kr_slot = jnp.pad(kr, ((0, 0), (0, 0), (B_NOPE, LANES - B_NOPE - B_ROPE)))
    return jnp.concatenate([gates, qa, ka, va, u, pb, pc, pxx, ckv, kr_slot, cq], axis=-1).astype(BF16)


def _pad_head_slots(v, width):
    lead = v.shape[:-1]
    v = v.reshape(lead + (-1, width))
    v = jnp.pad(v, [(0, 0)] * len(lead) + [(0, 0), (0, LANES - width)])
    return v.reshape(lead + (-1,))


def kernel(x, c, ctx, c_ctx, w_mod, b_mod, g_norm1, g_norm2, w_in, gq_a, gk_a, lam_a, g_sub_a,
           g_cq, w_uq, g_ckv, w_ukv, gq_b, gk_b, w_pool, s_pool, w_conv, w_branch, w_o,
           w_ff1, w_ff2):
    b, n_lat, d = x.shape
    n_ctx = ctx.shape[1]
    depth = w_mod.shape[0]
    tq = n_ctx
    assert d == D_MODEL and n_lat % tq == 0 and tq % LANES == 0 and n_lat % GRID_W == 0

    rope_a = _rope_tables(n_ctx, n_lat, A_HD, 0, A_HD)
    rope_b = _rope_tables(n_ctx, n_lat, B_ROPE, B_NOPE, 0)

    mod_rows = -(-(b + 1) // 8) * 8
    cc = jnp.concatenate([c, c_ctx[None, :], jnp.zeros((mod_rows - b - 1, d), F32)], axis=0)
    mods = _modulation(cc, w_mod, b_mod).reshape(depth, mod_rows, 1, N_MOD * d)

    w_in_p = _relayout_w_in(w_in)
    w_uq_p = _pad_head_slots(w_uq, B_QK).astype(BF16)
    w_ukv_b = w_ukv.astype(BF16)
    w_pool_b = w_pool.astype(BF16)
    w_branch_b = w_branch.astype(BF16)
    w_o_b = w_o.astype(BF16)
    w_ff1_b = w_ff1.astype(BF16)
    w_ff2_b = w_ff2.astype(BF16)
    gq_a2 = jnp.tile(gq_a, (1, 2))[:, None, :]
    gk_a2 = jnp.tile(gk_a, (1, 2))[:, None, :]
    gq_bv = jnp.pad(gq_b, ((0, 0), (0, LANES - B_QK)))[:, None, :]
    gk_bv = jnp.pad(gk_b, ((0, 0), (0, LANES - B_QK)))[:, None, :]

    x_all = jnp.concatenate([ctx, x], axis=1)
    for l in range(depth):
        last = l == depth - 1
        q_off = 1 if last else 0
        lam_init = 0.8 - 0.6 * math.exp(-0.3 * l)
        px = _inproj(x_all, mods[l], g_norm1[l][None, :], w_in_p[l], tq)
        ya = _attn_a(px, rope_a, gq_a2[l], gk_a2[l], lam_a[l], g_sub_a[l][None, :],
                     n_ctx, q_off, lam_init)
        yb = _attn_b(px, rope_b, g_cq[l][None, :], g_ckv[l][None, :], w_uq_p[l], w_ukv_b[l],
                     gq_bv[l], gk_bv[l], n_ctx, q_off)
        yc, yd = _mix_cd(px, w_pool_b[l], s_pool[l][None, :], w_conv[l], n_ctx)
        x_all = _merge_ffn((ya, yb, yc, yd), px, x_all, mods[l], w_branch_b[l], w_o_b[l],
                           g_norm2[l][None, :], w_ff1_b[l], w_ff2_b[l], tq, q_off)
    return x_all
```

```python
import functools
import math

import numpy as np
import jax
import jax.numpy as jnp
from jax import lax
from jax.experimental import pallas as pl
from jax.experimental.pallas import tpu as pltpu

F32 = jnp.float32
BF16 = jnp.bfloat16

D_MODEL = 1024
GRID_W = 64
ROPE_BASE = 10000.0
EPS = 1e-6
LOG2E = math.log2(math.e)

A_HEADS = 4
A_HD = 64
A_VD = 128
B_HEADS = 8
B_NOPE = 64
B_ROPE = 32
B_QK = B_NOPE + B_ROPE
B_VD = 64
B_QLORA = 384
B_KVLORA = 256
POOL_WINDOWS = (2, 4, 8, 16)
POOL_GROUP = 128
POOL_HALO = 8
BRANCH_W = 512
N_BRANCH = 4
D_FF = 4 * D_MODEL
N_MOD = 6

LANES = 128

CQ_OFF = 0
KR_OFF = CQ_OFF + B_QLORA
CKV_OFF = KR_OFF + LANES
QA_OFF = CKV_OFF + B_KVLORA
KA_OFF = QA_OFF + 512
VA_OFF = KA_OFF + 512
CD_OFF = VA_OFF + 512
CD_W = 4 * 512
GATE_OFF = CD_OFF + CD_W
D_IN_P = GATE_OFF + N_BRANCH * D_MODEL
IN_CHUNK = 512

ATT_SUB = 128
ATT_GROUP = 1024

VMEM_LIMIT = 56 * 1024 * 1024


def _dot(a, b):
    return jnp.dot(a, b, preferred_element_type=F32)


def _dot_nt(a, b):
    return lax.dot_general(a, b, (((1,), (1,)), ((), ())), preferred_element_type=F32)


def _lane_iota(n=LANES):
    return lax.broadcasted_iota(jnp.int32, (1, n), 1)


def _rms(x):
    return x * lax.rsqrt(jnp.mean(x * x, axis=-1, keepdims=True) + EPS)


def _rope(t, tabs, half):
    cos, s_left, s_right = tabs
    n = t.shape[-1]
    return (t * cos + pltpu.roll(t, n - half, 1) * s_left
            + pltpu.roll(t, half, 1) * s_right)


def _const_spec(shape):
    nd = len(shape)
    return pl.BlockSpec(shape, lambda *_: (0,) * nd, pipeline_mode=pl.Buffered(1))


def _mod_kernel(c_ref, w_ref, b_ref, o_ref):
    c = c_ref[...]
    h = (c * jax.nn.sigmoid(c)).astype(BF16)
    o_ref[...] = _dot(h, w_ref[...].astype(BF16)) + b_ref[...]


def _modulation(cc, w_mod, b_mod):
    depth, d, n = w_mod.shape
    rows = cc.shape[0]
    tn = 1536
    return pl.pallas_call(
        _mod_kernel,
        grid=(depth, n // tn),
        in_specs=[
            pl.BlockSpec((rows, d), lambda l, j: (0, 0)),
            pl.BlockSpec((None, d, tn), lambda l, j: (l, 0, j)),
            pl.BlockSpec((None, 1, tn), lambda l, j: (l, 0, j)),
        ],
        out_specs=pl.BlockSpec((None, rows, tn), lambda l, j: (l, 0, j)),
        out_shape=jax.ShapeDtypeStruct((depth, rows, n), F32),
        compiler_params=pltpu.CompilerParams(
            dimension_semantics=("parallel", "parallel"), vmem_limit_bytes=VMEM_LIMIT),
        name="modulation",
    )(cc, w_mod, b_mod.reshape(depth, 1, n))


def _inproj_kernel(x_ref, mod_ref, g_ref, w_ref, wuq_ref, wukv_ref, gcq_ref, gckv_ref,
                   gqa_ref, gka_ref, gqb_ref, gkb_ref,
                   ca_ref, la_ref, ra_ref, cb_ref, lb_ref, rb_ref,
                   gate_ref, cd_ref, qa_ref, ka_ref, vxa_ref, qb_ref, kb_ref, vxb_ref):
    d = D_MODEL
    y = _rms(x_ref[0]) * g_ref[...]
    h = (y * (1.0 + mod_ref[:, d:2 * d]) + mod_ref[:, 0:d]).astype(BF16)
    tq = h.shape[0]

    def proj(off, width):
        return _dot(h, w_ref[:, off:off + width])

    lane = _lane_iota()
    lo = lane < A_HD
    nope = lane < B_NOPE
    rope_a = (ca_ref[...], la_ref[...], ra_ref[...])
    rope_b = (cb_ref[...], lb_ref[...], rb_ref[...])

    cq = proj(CQ_OFF, B_QLORA)
    q_all = _dot((_rms(cq) * gcq_ref[...]).astype(BF16), wuq_ref[...])
    gq_b = gqb_ref[...] * (B_QK ** -0.5 * LOG2E)
    for hd in range(B_HEADS):
        q = q_all[:, hd * LANES:(hd + 1) * LANES]
        q2 = q * q
        s_n = jnp.sum(jnp.where(nope, q2, 0.0), axis=-1, keepdims=True)
        s_r = jnp.sum(jnp.where(nope, 0.0, q2), axis=-1, keepdims=True)
        inv = jnp.where(nope, lax.rsqrt(s_n * (1.0 / B_NOPE) + EPS),
                        lax.rsqrt(s_r * (1.0 / B_ROPE) + EPS))
        qb_ref[0, :, hd * LANES:(hd + 1) * LANES] = _rope(
            q * inv * gq_b, rope_b, B_ROPE // 2).astype(BF16)

    kr = proj(KR_OFF, LANES)
    krn = kr * lax.rsqrt(jnp.sum(kr * kr, axis=-1, keepdims=True) * (1.0 / B_ROPE) + EPS)
    krn = _rope(krn * jnp.where(nope, 0.0, gkb_ref[...]), rope_b, B_ROPE // 2)
    gk_nope = jnp.where(nope, gkb_ref[...], 0.0)
    ckv = proj(CKV_OFF, B_KVLORA)
    kv_all = _dot((_rms(ckv) * gckv_ref[...]).astype(BF16), wukv_ref[...])
    for hd in range(B_HEADS):
        kv = kv_all[:, hd * LANES:(hd + 1) * LANES]
        ss = jnp.sum(jnp.where(nope, kv * kv, 0.0), axis=-1, keepdims=True)
        kn = kv * lax.rsqrt(ss * (1.0 / B_NOPE) + EPS) * gk_nope
        kb_ref[0, :, hd * LANES:(hd + 1) * LANES] = (kn + krn).astype(BF16)
        vxb_ref[0, :, hd * LANES:(hd + 1) * LANES] = jnp.where(nope, 1.0, kv).astype(BF16)

    def norm_halves(x):
        x2 = x * x
        s_lo = jnp.sum(jnp.where(lo, x2, 0.0), axis=-1, keepdims=True)
        s_hi = jnp.sum(jnp.where(lo, 0.0, x2), axis=-1, keepdims=True)
        return x * jnp.where(lo, lax.rsqrt(s_lo * (1.0 / A_HD) + EPS),
                             lax.rsqrt(s_hi * (1.0 / A_HD) + EPS))

    gq_a = gqa_ref[...] * (A_HD ** -0.5 * LOG2E)
    for off, gain, o_ref in ((QA_OFF, gq_a, qa_ref), (KA_OFF, gka_ref[...], ka_ref)):
        t = proj(off, A_HEADS * LANES)
        for hd in range(A_HEADS):
            sl = slice(hd * LANES, (hd + 1) * LANES)
            o_ref[0, :, sl] = _rope(norm_halves(t[:, sl]) * gain, rope_a, A_HD // 2).astype(BF16)
    va = proj(VA_OFF, A_HEADS * A_VD).astype(BF16)
    ones = jnp.ones((tq, A_VD), BF16)
    for hd in range(A_HEADS):
        vxa_ref[0, :, 2 * hd * A_VD:(2 * hd + 1) * A_VD] = va[:, hd * A_VD:(hd + 1) * A_VD]
        vxa_ref[0, :, (2 * hd + 1) * A_VD:(2 * hd + 2) * A_VD] = ones

    for c0 in range(0, CD_W, IN_CHUNK):
        cd_ref[0, :, c0:c0 + IN_CHUNK] = proj(CD_OFF + c0, IN_CHUNK)
    for c0 in range(0, N_BRANCH * d, IN_CHUNK):
        gate_ref[0, :, c0:c0 + IN_CHUNK] = proj(GATE_OFF + c0, IN_CHUNK)


def _inproj(x_all, mods_l, g1, w_in_p, wuq_p, wukv, gcq, gckv, gqa, gka, gqb, gkb,
            rope_a, rope_b, tq, n_lat):
    b, t, d = x_all.shape
    nt = t // tq
    ctx_tile = n_lat // tq
    row = lambda i, j: (i, j, 0)
    tab = pl.BlockSpec((tq, LANES), lambda i, j: (j, 0))
    vec = lambda n: _const_spec((1, n))

    def out(width, dtype):
        return (pl.BlockSpec((1, tq, width), row), jax.ShapeDtypeStruct((b, t, width), dtype))

    outs = [out(N_BRANCH * d, F32), out(CD_W, F32), out(A_HEADS * LANES, BF16),
            out(A_HEADS * LANES, BF16), out(2 * A_HEADS * A_VD, BF16),
            out(B_HEADS * LANES, BF16), out(B_HEADS * LANES, BF16), out(B_HEADS * LANES, BF16)]
    return pl.pallas_call(
        _inproj_kernel,
        grid=(b, nt),
        in_specs=[
            pl.BlockSpec((1, tq, d), row),
            pl.BlockSpec((None, 1, N_MOD * d), lambda i, j: (jnp.where(j == ctx_tile, b, i), 0, 0)),
            vec(d),
            _const_spec((d, D_IN_P)),
            _const_spec((B_QLORA, B_HEADS * LANES)),
            _const_spec((B_KVLORA, B_HEADS * LANES)),
            vec(B_QLORA), vec(B_KVLORA), vec(LANES), vec(LANES), vec(LANES), vec(LANES),
            tab, tab, tab, tab, tab, tab,
        ],
        out_specs=[o[0] for o in outs],
        out_shape=[o[1] for o in outs],
        compiler_params=pltpu.CompilerParams(
            dimension_semantics=("parallel", "parallel"), vmem_limit_bytes=VMEM_LIMIT),
        name="inproj",
    )(x_all, mods_l, g1, w_in_p, wuq_p, wukv, gcq, gckv, gqa, gka, gqb, gkb, *rope_a, *rope_b)


def _softmax_pv(q, k, vx):
    s = _dot_nt(q, k)
    p = jnp.exp2(s - jnp.max(s, axis=-1, keepdims=True)).astype(BF16)
    return _dot(p, vx)


def _sweep(tile, n_lat, t_all):
    group = math.gcd(n_lat, ATT_GROUP)

    def body(g, carry):
        r0 = pl.multiple_of(g * group, group)
        for s0 in range(0, group, ATT_SUB):
            tile(r0 + s0, 0)
        return carry

    lax.fori_loop(0, n_lat // group, body, 0)
    for s0 in range(n_lat, t_all, ATT_SUB):
        tile(s0, n_lat)


def _attn_a_kernel(q_ref, k_ref, vx_ref, lam_ref, gsub_ref, o_ref, *, n_lat, lam_init):
    t_all = k_ref.shape[1]
    lo = _lane_iota() < A_HD
    la = lam_ref[...].astype(F32)
    lam = (jnp.exp(jnp.sum(la[0:1] * la[1:2], axis=-1, keepdims=True))
           - jnp.exp(jnp.sum(la[2:3] * la[3:4], axis=-1, keepdims=True)) + lam_init)
    post = gsub_ref[...] * (1.0 - lam_init)

    def tile(r0, key_lo):
        q = q_ref[0, pl.ds(r0, ATT_SUB), :]
        k = k_ref[0, key_lo:t_all, :]
        vx = vx_ref[0, key_lo:t_all, :]
        zero = jnp.zeros_like(q)
        r1 = _softmax_pv(jnp.where(lo, q, zero), k, vx)
        r2 = _softmax_pv(jnp.where(lo, zero, q), k, vx)
        o = (r1[:, 0:A_VD] / r1[:, A_VD:A_VD + 1]
             - lam * (r2[:, 0:A_VD] / r2[:, A_VD:A_VD + 1]))
        o_ref[0, pl.ds(r0, ATT_SUB), :] = (_rms(o) * post).astype(o_ref.dtype)

    _sweep(tile, n_lat, t_all)


def _attn_a(qa, ka, vxa, lam_l, gsub, n_lat, lam_init):
    b, t, _ = qa.shape
    head = lambda i, h: (i, 0, h)
    return pl.pallas_call(
        functools.partial(_attn_a_kernel, n_lat=n_lat, lam_init=lam_init),
        grid=(b, A_HEADS),
        in_specs=[
            pl.BlockSpec((1, t, LANES), head),
            pl.BlockSpec((1, t, LANES), head),
            pl.BlockSpec((1, t, 2 * A_VD), head),
            pl.BlockSpec((4, A_HD), lambda i, h: (0, 0)),
            pl.BlockSpec((1, LANES), lambda i, h: (0, 0)),
        ],
        out_specs=pl.BlockSpec((1, t, LANES), head),
        out_shape=jax.ShapeDtypeStruct((b, t, BRANCH_W), BF16),
        compiler_params=pltpu.CompilerParams(
            dimension_semantics=("parallel", "parallel"), vmem_limit_bytes=VMEM_LIMIT),
        name="attn_a",
    )(qa, ka, vxa, lam_l, gsub)


def _attn_b_kernel(q_ref, k_ref, vx_ref, o_ref, *, n_lat):
    t_all = k_ref.shape[1]
    nope = _lane_iota() < B_NOPE

    def tile(r0, key_lo):
        outs = []
        for hh in range(2):
            sl = slice(hh * LANES, (hh + 1) * LANES)
            r = _softmax_pv(q_ref[0, pl.ds(r0, ATT_SUB), sl], k_ref[0, key_lo:t_all, sl],
                            vx_ref[0, key_lo:t_all, sl])
            outs.append(r / r[:, 0:1])
        o = jnp.where(nope, pltpu.roll(outs[0], B_VD, 1), outs[1])
        o_ref[0, pl.ds(r0, ATT_SUB), :] = o.astype(o_ref.dtype)

    _sweep(tile, n_lat, t_all)


def _attn_b(qb, kb, vxb, n_lat):
    b, t, _ = qb.shape
    pair = lambda i, h: (i, 0, h)
    spec = pl.BlockSpec((1, t, 2 * LANES), pair)
    return pl.pallas_call(
        functools.partial(_attn_b_kernel, n_lat=n_lat),
        grid=(b, B_HEADS // 2),
        in_specs=[spec, spec, spec],
        out_specs=pl.BlockSpec((1, t, LANES), pair),
        out_shape=jax.ShapeDtypeStruct((b, t, BRANCH_W), BF16),
        compiler_params=pltpu.CompilerParams(
            dimension_semantics=("parallel", "parallel"), vmem_limit_bytes=VMEM_LIMIT),
        name="attn_b",
    )(qb, kb, vxb)


def _mix_cd_kernel(u_ref, pb_ref, pc_ref, pxx_ref, wp_ref, sp_ref, wc_ref,
                   yc_ref, yd_ref, pad_s, *, segments):
    g = pl.program_id(1)
    zeros_halo = jnp.zeros((POOL_HALO, LANES), F32)
    wc = wc_ref[...]

    for start, length in segments:
        rows = lax.broadcasted_iota(jnp.int32, (length, 1), 0)
        u = u_ref[0, start:start + length, :]
        pad_s[0:POOL_HALO, :] = zeros_halo
        pad_s[POOL_HALO:POOL_HALO + length, :] = u
        pad_s[POOL_HALO + length:2 * POOL_HALO + length, :] = zeros_halo

        for gi, w in enumerate(POOL_WINDOWS):
            @pl.when(g == gi)
            def _pool(w=w):
                acc = pad_s[POOL_HALO - w // 2:POOL_HALO - w // 2 + length, :]
                for j in range(1 - w // 2, w // 2):
                    acc = acc + pad_s[POOL_HALO + j:POOL_HALO + j + length, :]
                cnt = (jnp.minimum(rows + w // 2, length) - jnp.maximum(rows - w // 2, 0)).astype(F32)
                dd = acc / cnt - u
                y = _dot(dd.astype(BF16), wp_ref[...]) * sp_ref[...]
                yc_ref[0, start:start + length, :] = y.astype(yc_ref.dtype)

        uu = pc_ref[0, start:start + length, :] * pxx_ref[0, start:start + length, :]
        pad_s[POOL_HALO:POOL_HALO + length, :] = uu
        y = (pad_s[POOL_HALO - 1:POOL_HALO - 1 + length, :] * wc[0:1]
             + uu * wc[1:2]
             + pad_s[POOL_HALO + 1:POOL_HALO + 1 + length, :] * wc[2:3])
        yd_ref[0, start:start + length, :] = (
            pb_ref[0, start:start + length, :] * y).astype(yd_ref.dtype)


def _mix_cd(cd, w_pool_l, s_pool_l, w_conv_l, n_lat):
    b, t, _ = cd.shape
    segments = ((0, n_lat), (n_lat, t - n_lat))
    n_g = len(POOL_WINDOWS)

    def col(k):
        return pl.BlockSpec((1, t, LANES), lambda i, g: (i, 0, k * n_g + g))

    out_spec = pl.BlockSpec((1, t, LANES), lambda i, g: (i, 0, g))
    return pl.pallas_call(
        functools.partial(_mix_cd_kernel, segments=segments),
        grid=(b, n_g),
        in_specs=[
            col(0), col(1), col(2), col(3),
            pl.BlockSpec((None, POOL_GROUP, POOL_GROUP), lambda i, g: (g, 0, 0)),
            pl.BlockSpec((1, LANES), lambda i, g: (0, g)),
            pl.BlockSpec((3, LANES), lambda i, g: (0, g)),
        ],
        out_specs=[out_spec, out_spec],
        out_shape=[jax.ShapeDtypeStruct((b, t, BRANCH_W), BF16)] * 2,
        scratch_shapes=[pltpu.VMEM((max(n_lat, t - n_lat) + 2 * POOL_HALO, LANES), F32)],
        compiler_params=pltpu.CompilerParams(
            dimension_semantics=("parallel", "parallel"), vmem_limit_bytes=VMEM_LIMIT),
        name="mix_cd",
    )(cd, cd, cd, cd, w_pool_l, s_pool_l, w_conv_l)


def _merge_ffn_kernel(ya_ref, yb_ref, yc_ref, yd_ref, gate_ref, x_ref, mod_ref,
                      wb_ref, wo_ref, g2_ref, w1_ref, w2_ref, o_ref):
    d = D_MODEL
    merged = None
    for n, y_ref in enumerate((ya_ref, yb_ref, yc_ref, yd_ref)):
        proj = _dot(y_ref[0], wb_ref[n])
        gate = jax.nn.sigmoid(gate_ref[0, :, n * d:(n + 1) * d])
        merged = gate * proj if merged is None else merged + gate * proj
    mix = _dot(merged.astype(BF16), wo_ref[...])
    x1 = x_ref[0] + mod_ref[:, 2 * d:3 * d] * mix

    y = _rms(x1) * g2_ref[...]
    h = (y * (1.0 + mod_ref[:, 4 * d:5 * d]) + mod_ref[:, 3 * d:4 * d]).astype(BF16)
    f = None
    for c0 in range(0, D_FF, d):
        a = jnp.maximum(_dot(h, w1_ref[:, c0:c0 + d]), 0.0)
        part = _dot((a * a).astype(BF16), w2_ref[c0:c0 + d, :])
        f = part if f is None else f + part
    o_ref[0] = x1 + mod_ref[:, 5 * d:6 * d] * f


def _merge_ffn(ys, gates, x_all, mods_l, wb, wo, g2, w1, w2, tq, n_lat, with_ctx):
    b, t, d = x_all.shape
    ctx_tile = n_lat // tq
    row = lambda i, j: (i, j, 0)
    y_spec = pl.BlockSpec((1, tq, BRANCH_W), row)
    if with_ctx:
        nq, out_rows, aliases = t // tq, t, {5: 0}
    else:
        nq, out_rows, aliases = ctx_tile, n_lat, {}
    return pl.pallas_call(
        _merge_ffn_kernel,
        grid=(b, nq),
        in_specs=[
            y_spec, y_spec, y_spec, y_spec,
            pl.BlockSpec((1, tq, N_BRANCH * d), row),
            pl.BlockSpec((1, tq, d), row),
            pl.BlockSpec((None, 1, N_MOD * d), lambda i, j: (jnp.where(j == ctx_tile, b, i), 0, 0)),
            _const_spec((N_BRANCH, BRANCH_W, d)),
            _const_spec((d, d)),
            _const_spec((1, d)),
            _const_spec((d, D_FF)),
            _const_spec((D_FF, d)),
        ],
        out_specs=pl.BlockSpec((1, tq, d), row),
        out_shape=jax.ShapeDtypeStruct((b, out_rows, d), F32),
        input_output_aliases=aliases,
        compiler_params=pltpu.CompilerParams(
            dimension_semantics=("parallel", "parallel"), vmem_limit_bytes=VMEM_LIMIT),
        name="merge_ffn",
    )(*ys, gates, x_all, mods_l, wb, wo, g2, w1, w2)


def _rope_tables(n_lat, n_ctx, rot_dim, lane_lo, period, total=LANES):
    rows = n_lat // GRID_W
    row = np.repeat(np.arange(rows, dtype=np.float64), GRID_W)
    col = np.tile(np.arange(GRID_W, dtype=np.float64), rows)
    n_freq = rot_dim // 4
    inv = ROPE_BASE ** (-np.arange(n_freq, dtype=np.float64) / n_freq)
    inv = inv.astype(np.float32).astype(np.float64)
    ang = np.concatenate([row[:, None] * inv, col[:, None] * inv], axis=-1)
    ang = ang.astype(np.float32).astype(np.float64)
    half = rot_dim // 2
    t = n_lat + n_ctx
    cos = np.ones((t, total), np.float32)
    s_left = np.zeros((t, total), np.float32)
    s_right = np.zeros((t, total), np.float32)
    starts = [lane_lo] if period == 0 else list(range(lane_lo, total, period))
    for s0 in starts:
        cos[:n_lat, s0:s0 + half] = np.cos(ang)
        cos[:n_lat, s0 + half:s0 + rot_dim] = np.cos(ang)
        s_left[:n_lat, s0:s0 + half] = -np.sin(ang)
        s_right[:n_lat, s0 + half:s0 + rot_dim] = np.sin(ang)
    return jnp.asarray(cos), jnp.asarray(s_left), jnp.asarray(s_right)


def _relayout_w_in(w_in):
    sizes = (512, 512, 512, B_QLORA, B_KVLORA, B_ROPE, 512, 512, 512, 512, N_BRANCH * D_MODEL)
    offs = np.concatenate([[0], np.cumsum(sizes)])
    qa, ka, va, cq, ckv, kr, u, pb, pc, pxx, gates = [
        w_in[:, :, offs[i]:offs[i + 1]] for i in range(len(sizes))]
    kr_slot = jnp.pad(kr, ((0, 0), (0, 0), (B_NOPE, LANES - B_NOPE - B_ROPE)))
    return jnp.concatenate([cq, kr_slot, ckv, qa, ka, va, u, pb, pc, pxx, gates], axis=-1).astype(BF16)


def _pad_head_slots(v, width):
    lead = v.shape[:-1]
    v = v.reshape(lead + (-1, width))
    v = jnp.pad(v, [(0, 0)] * len(lead) + [(0, 0), (0, LANES - width)])
    return v.reshape(lead + (-1,))


def kernel(x, c, ctx, c_ctx, w_mod, b_mod, g_norm1, g_norm2, w_in, gq_a, gk_a, lam_a, g_sub_a,
           g_cq, w_uq, g_ckv, w_ukv, gq_b, gk_b, w_pool, s_pool, w_conv, w_branch, w_o,
           w_ff1, w_ff2):
    b, n_lat, d = x.shape
    n_ctx = ctx.shape[1]
    depth = w_mod.shape[0]
    tq = n_ctx
    assert d == D_MODEL and n_lat % ATT_SUB == 0 and n_ctx % ATT_SUB == 0
    assert n_lat % tq == 0 and tq % LANES == 0 and n_lat % GRID_W == 0

    rope_a = _rope_tables(n_lat, n_ctx, A_HD, 0, A_HD)
    rope_b = _rope_tables(n_lat, n_ctx, B_ROPE, B_NOPE, 0)

    mod_rows = -(-(b + 1) // 8) * 8
    cc = jnp.concatenate([c, c_ctx[None, :], jnp.zeros((mod_rows - b - 1, d), F32)], axis=0)
    mods = _modulation(cc, w_mod, b_mod).reshape(depth, mod_rows, 1, N_MOD * d)

    w_in_p = _relayout_w_in(w_in)
    w_uq_p = _pad_head_slots(w_uq, B_QK).astype(BF16)
    w_ukv_b = w_ukv.astype(BF16)
    w_pool_b = w_pool.astype(BF16)
    w_branch_b = w_branch.astype(BF16)
    w_o_b = w_o.astype(BF16)
    w_ff1_b = w_ff1.astype(BF16)
    w_ff2_b = w_ff2.astype(BF16)
    gq_a2 = jnp.tile(gq_a, (1, 2))[:, None, :]
    gk_a2 = jnp.tile(gk_a, (1, 2))[:, None, :]
    gq_bv = jnp.pad(gq_b, ((0, 0), (0, LANES - B_QK)))[:, None, :]
    gk_bv = jnp.pad(gk_b, ((0, 0), (0, LANES - B_QK)))[:, None, :]

    x_all = jnp.concatenate([x, ctx], axis=1)
    for l in range(depth):
        last = l == depth - 1
        lam_init = 0.8 - 0.6 * math.exp(-0.3 * l)
        gates, cd, qa, ka, vxa, qb, kb, vxb = _inproj(
            x_all, mods[l], g_norm1[l][None, :], w_in_p[l], w_uq_p[l], w_ukv_b[l],
            g_cq[l][None, :], g_ckv[l][None, :], gq_a2[l], gk_a2[l], gq_bv[l], gk_bv[l],
            rope_a, rope_b, tq, n_lat)
        ya = _attn_a(qa, ka, vxa, lam_a[l], g_sub_a[l][None, :], n_lat, lam_init)
        yb = _attn_b(qb, kb, vxb, n_lat)
        yc, yd = _mix_cd(cd, w_pool_b[l], s_pool[l][None, :], w_conv[l], n_lat)
        x_all = _merge_ffn((ya, yb, yc, yd), gates, x_all, mods[l], w_branch_b[l], w_o_b[l],
                           g_norm2[l][None, :], w_ff1_b[l], w_ff2_b[l], tq, n_lat, not last)
    return x_all
```

```python
import functools
import math

import numpy as np
import jax
import jax.numpy as jnp
from jax import lax
from jax.experimental import pallas as pl
from jax.experimental.pallas import tpu as pltpu

F32 = jnp.float32
BF16 = jnp.bfloat16

D_MODEL = 1024
GRID_W = 64
ROPE_BASE = 10000.0
EPS = 1e-6
LOG2E = math.log2(math.e)

A_HEADS = 4
A_HD = 64
A_VD = 128
B_HEADS = 8
B_NOPE = 64
B_ROPE = 32
B_QK = B_NOPE + B_ROPE
B_VD = 64
B_QLORA = 384
B_KVLORA = 256
POOL_WINDOWS = (2, 4, 8, 16)
POOL_GROUP = 128
POOL_HALO = 8
BRANCH_W = 512
N_BRANCH = 4
D_FF = 4 * D_MODEL
N_MOD = 6

LANES = 128

CQ_OFF = 0
KR_OFF = CQ_OFF + B_QLORA
CKV_OFF = KR_OFF + LANES
QA_OFF = CKV_OFF + B_KVLORA
KA_OFF = QA_OFF + 512
VA_OFF = KA_OFF + 512
CD_OFF = VA_OFF + 512
CD_W = 4 * 512
GATE_OFF = CD_OFF + CD_W
D_IN_P = GATE_OFF + N_BRANCH * D_MODEL
IN_FILL = 256

ATT_SUB = 128
ATT_GROUP = 2048

VMEM_LIMIT = 56 * 1024 * 1024


def _dot(a, b):
    return jnp.dot(a, b, preferred_element_type=F32)


def _dot_nt(a, b):
    return lax.dot_general(a, b, (((1,), (1,)), ((), ())), preferred_element_type=F32)


def _lane_iota(n=LANES):
    return lax.broadcasted_iota(jnp.int32, (1, n), 1)


def _rms(x):
    return x * lax.rsqrt(jnp.mean(x * x, axis=-1, keepdims=True) + EPS)


def _rope(t, tabs, half):
    cos, s_left, s_right = tabs
    n = t.shape[-1]
    return (t * cos + pltpu.roll(t, n - half, 1) * s_left
            + pltpu.roll(t, half, 1) * s_right)


def _const_spec(shape):
    nd = len(shape)
    return pl.BlockSpec(shape, lambda *_: (0,) * nd, pipeline_mode=pl.Buffered(1))


def _stream_specs(split, tq, d, ctx_tile):
    if split:
        return [pl.BlockSpec((1, tq, d), lambda i, j: (i, jnp.minimum(j, ctx_tile - 1), 0)),
                pl.BlockSpec((1, tq, d), lambda i, j: (i, 0, 0))]
    return [pl.BlockSpec((1, tq, d), lambda i, j: (i, j, 0))]


def _mod_kernel(c_ref, w_ref, b_ref, o_ref):
    c = c_ref[...]
    h = (c * jax.nn.sigmoid(c)).astype(BF16)
    o_ref[...] = _dot(h, w_ref[...].astype(BF16)) + b_ref[...]


def _modulation(cc, w_mod, b_mod):
    depth, d, n = w_mod.shape
    rows = cc.shape[0]
    tn = 1536
    return pl.pallas_call(
        _mod_kernel,
        grid=(depth, n // tn),
        in_specs=[
            pl.BlockSpec((rows, d), lambda l, j: (0, 0)),
            pl.BlockSpec((None, d, tn), lambda l, j: (l, 0, j)),
            pl.BlockSpec((None, 1, tn), lambda l, j: (l, 0, j)),
        ],
        out_specs=pl.BlockSpec((None, rows, tn), lambda l, j: (l, 0, j)),
        out_shape=jax.ShapeDtypeStruct((depth, rows, n), F32),
        compiler_params=pltpu.CompilerParams(
            dimension_semantics=("parallel", "parallel"), vmem_limit_bytes=VMEM_LIMIT),
        name="modulation",
    )(cc, w_mod, b_mod.reshape(depth, 1, n))


def _inproj_kernel(*refs, ctx_tile, split):
    if split:
        x_ref, xc_ref = refs[:2]
        refs = refs[2:]
        x = jnp.where(pl.program_id(1) == ctx_tile, xc_ref[0], x_ref[0])
    else:
        x = refs[0][0]
        refs = refs[1:]
    (mod_ref, g_ref, w_ref, wuq_ref, wukv_ref, gcq_ref, gckv_ref,
     gqa_ref, gka_ref, gqb_ref, gkb_ref, ca_ref, la_ref, ra_ref, cb_ref, lb_ref, rb_ref,
     gate_ref, cd_ref, qa_ref, ka_ref, vxa_ref, qb_ref, kb_ref, vxb_ref) = refs
    d = D_MODEL
    y = _rms(x) * g_ref[...]
    h = (y * (1.0 + mod_ref[:, d:2 * d]) + mod_ref[:, 0:d]).astype(BF16)
    tq = h.shape[0]

    def proj(off, width):
        return _dot(h, w_ref[:, off:off + width])

    fill = [(gate_ref, GATE_OFF, c0) for c0 in range(0, N_BRANCH * d, IN_FILL)]
    fill += [(cd_ref, CD_OFF, c0) for c0 in range(0, CD_W, IN_FILL)]

    def emit_fill(n=1):
        for _ in range(n):
            if fill:
                o_ref, off, c0 = fill.pop(0)
                o_ref[0, :, c0:c0 + IN_FILL] = proj(off + c0, IN_FILL)

    lane = _lane_iota()
    lo = lane < A_HD
    nope = lane < B_NOPE
    rope_a = (ca_ref[...], la_ref[...], ra_ref[...])
    rope_b = (cb_ref[...], lb_ref[...], rb_ref[...])

    cq = proj(CQ_OFF, B_QLORA)
    emit_fill(2)
    q_all = _dot((_rms(cq) * gcq_ref[...]).astype(BF16), wuq_ref[...])
    gq_b = gqb_ref[...] * (B_QK ** -0.5 * LOG2E)
    for hd in range(B_HEADS):
        q = q_all[:, hd * LANES:(hd + 1) * LANES]
        q2 = q * q
        s_n = jnp.sum(jnp.where(nope, q2, 0.0), axis=-1, keepdims=True)
        s_r = jnp.sum(jnp.where(nope, 0.0, q2), axis=-1, keepdims=True)
        inv = jnp.where(nope, lax.rsqrt(s_n * (1.0 / B_NOPE) + EPS),
                        lax.rsqrt(s_r * (1.0 / B_ROPE) + EPS))
        qb_ref[0, :, hd * LANES:(hd + 1) * LANES] = _rope(
            q * inv * gq_b, rope_b, B_ROPE // 2).astype(BF16)
        emit_fill()

    kr = proj(KR_OFF, LANES)
    krn = kr * lax.rsqrt(jnp.sum(kr * kr, axis=-1, keepdims=True) * (1.0 / B_ROPE) + EPS)
    krn = _rope(krn * jnp.where(nope, 0.0, gkb_ref[...]), rope_b, B_ROPE // 2)
    gk_nope = jnp.where(nope, gkb_ref[...], 0.0)
    ckv = proj(CKV_OFF, B_KVLORA)
    emit_fill()
    kv_all = _dot((_rms(ckv) * gckv_ref[...]).astype(BF16), wukv_ref[...])
    for hd in range(B_HEADS):
        kv = kv_all[:, hd * LANES:(hd + 1) * LANES]
        ss = jnp.sum(jnp.where(nope, kv * kv, 0.0), axis=-1, keepdims=True)
        kn = kv * lax.rsqrt(ss * (1.0 / B_NOPE) + EPS) * gk_nope
        kb_ref[0, :, hd * LANES:(hd + 1) * LANES] = (kn + krn).astype(BF16)
        vxb_ref[0, :, hd * LANES:(hd + 1) * LANES] = jnp.where(nope, 1.0, kv).astype(BF16)
        if hd % 2:
            emit_fill()

    def norm_halves(x):
        x2 = x * x
        s_lo = jnp.sum(jnp.where(lo, x2, 0.0), axis=-1, keepdims=True)
        s_hi = jnp.sum(jnp.where(lo, 0.0, x2), axis=-1, keepdims=True)
        return x * jnp.where(lo, lax.rsqrt(s_lo * (1.0 / A_HD) + EPS),
                             lax.rsqrt(s_hi * (1.0 / A_HD) + EPS))

    gq_a = gqa_ref[...] * (A_HD ** -0.5 * LOG2E)
    for off, gain, o_ref in ((QA_OFF, gq_a, qa_ref), (KA_OFF, gka_ref[...], ka_ref)):
        t = proj(off, A_HEADS * LANES)
        for hd in range(A_HEADS):
            sl = slice(hd * LANES, (hd + 1) * LANES)
            o_ref[0, :, sl] = _rope(norm_halves(t[:, sl]) * gain, rope_a, A_HD // 2).astype(BF16)
            emit_fill()
    va = proj(VA_OFF, A_HEADS * A_VD).astype(BF16)
    ones = jnp.ones((tq, A_VD), BF16)
    for hd in range(A_HEADS):
        vxa_ref[0, :, 2 * hd * A_VD:(2 * hd + 1) * A_VD] = va[:, hd * A_VD:(hd + 1) * A_VD]
        vxa_ref[0, :, (2 * hd + 1) * A_VD:(2 * hd + 2) * A_VD] = ones
    emit_fill(len(fill))


def _inproj(x_parts, mods_l, g1, w_in_p, wuq_p, wukv, gcq, gckv, gqa, gka, gqb, gkb,
            rope_a, rope_b, tq, n_lat, n_ctx):
    b, _, d = x_parts[0].shape
    t = n_lat + n_ctx
    nt = t // tq
    ctx_tile = n_lat // tq
    split = len(x_parts) == 2
    row = lambda i, j: (i, j, 0)
    tab = pl.BlockSpec((tq, LANES), lambda i, j: (j, 0))
    vec = lambda n: _const_spec((1, n))

    def out(width, dtype):
        return (pl.BlockSpec((1, tq, width), row), jax.ShapeDtypeStruct((b, t, width), dtype))

    outs = [out(N_BRANCH * d, F32), out(CD_W, F32), out(A_HEADS * LANES, BF16),
            out(A_HEADS * LANES, BF16), out(2 * A_HEADS * A_VD, BF16),
            out(B_HEADS * LANES, BF16), out(B_HEADS * LANES, BF16), out(B_HEADS * LANES, BF16)]
    return pl.pallas_call(
        functools.partial(_inproj_kernel, ctx_tile=ctx_tile, split=split),
        grid=(b, nt),
        in_specs=_stream_specs(split, tq, d, ctx_tile) + [
            pl.BlockSpec((None, 1, N_MOD * d), lambda i, j: (jnp.where(j == ctx_tile, b, i), 0, 0)),
            vec(d),
            _const_spec((d, D_IN_P)),
            _const_spec((B_QLORA, B_HEADS * LANES)),
            _const_spec((B_KVLORA, B_HEADS * LANES)),
            vec(B_QLORA), vec(B_KVLORA), vec(LANES), vec(LANES), vec(LANES), vec(LANES),
            tab, tab, tab, tab, tab, tab,
        ],
        out_specs=[o[0] for o in outs],
        out_shape=[o[1] for o in outs],
        compiler_params=pltpu.CompilerParams(
            dimension_semantics=("parallel", "parallel"), vmem_limit_bytes=VMEM_LIMIT),
        name="inproj",
    )(*x_parts, mods_l, g1, w_in_p, wuq_p, wukv, gcq, gckv, gqa, gka, gqb, gkb, *rope_a, *rope_b)


def _softmax_pv(q, k, vx):
    s = _dot_nt(q, k)
    p = jnp.exp2(s - jnp.max(s, axis=-1, keepdims=True)).astype(BF16)
    return _dot(p, vx)


def _sweep(tile, n_lat, t_all):
    group = math.gcd(n_lat, ATT_GROUP)

    def body(g, carry):
        r0 = pl.multiple_of(g * group, group)
        for s0 in range(0, group, ATT_SUB):
            tile(r0 + s0, 0)
        return carry

    lax.fori_loop(0, n_lat // group, body, 0)
    for s0 in range(n_lat, t_all, ATT_SUB):
        tile(s0, n_lat)


def _attn_a_kernel(q_ref, k_ref, vx_ref, lam_ref, gsub_ref, o_ref, *, n_lat, lam_init):
    t_all = k_ref.shape[1]
    lo = _lane_iota() < A_HD
    la = lam_ref[...].astype(F32)
    lam = (jnp.exp(jnp.sum(la[0:1] * la[1:2], axis=-1, keepdims=True))
           - jnp.exp(jnp.sum(la[2:3] * la[3:4], axis=-1, keepdims=True)) + lam_init)
    post = gsub_ref[...] * (1.0 - lam_init)

    def tile(r0, key_lo):
        q = q_ref[0, pl.ds(r0, ATT_SUB), :]
        k = k_ref[0, key_lo:t_all, :]
        vx = vx_ref[0, key_lo:t_all, :]
        zero = jnp.zeros_like(q)
        r1 = _softmax_pv(jnp.where(lo, q, zero), k, vx)
        r2 = _softmax_pv(jnp.where(lo, zero, q), k, vx)
        o = (r1[:, 0:A_VD] / r1[:, A_VD:A_VD + 1]
             - lam * (r2[:, 0:A_VD] / r2[:, A_VD:A_VD + 1]))
        o_ref[0, pl.ds(r0, ATT_SUB), :] = (_rms(o) * post).astype(o_ref.dtype)

    _sweep(tile, n_lat, t_all)


def _attn_a(qa, ka, vxa, lam_l, gsub, n_lat, lam_init):
    b, t, _ = qa.shape
    head = lambda i, h: (i, 0, h)
    return pl.pallas_call(
        functools.partial(_attn_a_kernel, n_lat=n_lat, lam_init=lam_init),
        grid=(b, A_HEADS),
        in_specs=[
            pl.BlockSpec((1, t, LANES), head),
            pl.BlockSpec((1, t, LANES), head),
            pl.BlockSpec((1, t, 2 * A_VD), head),
            pl.BlockSpec((4, A_HD), lambda i, h: (0, 0)),
            pl.BlockSpec((1, LANES), lambda i, h: (0, 0)),
        ],
        out_specs=pl.BlockSpec((1, t, LANES), head),
        out_shape=jax.ShapeDtypeStruct((b, t, BRANCH_W), BF16),
        compiler_params=pltpu.CompilerParams(
            dimension_semantics=("parallel", "parallel"), vmem_limit_bytes=VMEM_LIMIT),
        name="attn_a",
    )(qa, ka, vxa, lam_l, gsub)


def _attn_b_kernel(q_ref, k_ref, vx_ref, o_ref, *, n_lat):
    t_all = k_ref.shape[1]
    nope = _lane_iota() < B_NOPE

    def tile(r0, key_lo):
        outs = []
        for hh in range(2):
            sl = slice(hh * LANES, (hh + 1) * LANES)
            r = _softmax_pv(q_ref[0, pl.ds(r0, ATT_SUB), sl], k_ref[0, key_lo:t_all, sl],
                            vx_ref[0, key_lo:t_all, :])[:, sl]
            outs.append(r / r[:, 0:1])
        o = jnp.where(nope, pltpu.roll(outs[0], B_VD, 1), outs[1])
        o_ref[0, pl.ds(r0, ATT_SUB), :] = o.astype(o_ref.dtype)

    _sweep(tile, n_lat, t_all)


def _attn_b(qb, kb, vxb, n_lat):
    b, t, _ = qb.shape
    pair = lambda i, h: (i, 0, h)
    spec = pl.BlockSpec((1, t, 2 * LANES), pair)
    return pl.pallas_call(
        functools.partial(_attn_b_kernel, n_lat=n_lat),
        grid=(b, B_HEADS // 2),
        in_specs=[spec, spec, spec],
        out_specs=pl.BlockSpec((1, t, LANES), pair),
        out_shape=jax.ShapeDtypeStruct((b, t, BRANCH_W), BF16),
        compiler_params=pltpu.CompilerParams(
            dimension_semantics=("parallel", "parallel"), vmem_limit_bytes=VMEM_LIMIT),
        name="attn_b",
    )(qb, kb, vxb)


def _mix_cd_kernel(u_ref, pb_ref, pc_ref, pxx_ref, wp_ref, sp_ref, wc_ref,
                   yc_ref, yd_ref, pad_s, *, segments):
    g = pl.program_id(1)
    zeros_halo = jnp.zeros((POOL_HALO, LANES), F32)
    wc = wc_ref[...]

    for start, length in segments:
        rows = lax.broadcasted_iota(jnp.int32, (length, 1), 0)
        u = u_ref[0, start:start + length, :]
        pad_s[0:POOL_HALO, :] = zeros_halo
        pad_s[POOL_HALO:POOL_HALO + length, :] = u
        pad_s[POOL_HALO + length:2 * POOL_HALO + length, :] = zeros_halo

        for gi, w in enumerate(POOL_WINDOWS):
            @pl.when(g == gi)
            def _pool(w=w):
                acc = pad_s[POOL_HALO - w // 2:POOL_HALO - w // 2 + length, :]
                for j in range(1 - w // 2, w // 2):
                    acc = acc + pad_s[POOL_HALO + j:POOL_HALO + j + length, :]
                cnt = (jnp.minimum(rows + w // 2, length) - jnp.maximum(rows - w // 2, 0)).astype(F32)
                dd = acc / cnt - u
                y = _dot(dd.astype(BF16), wp_ref[...]) * sp_ref[...]
                yc_ref[0, start:start + length, :] = y.astype(yc_ref.dtype)

        uu = pc_ref[0, start:start + length, :] * pxx_ref[0, start:start + length, :]
        pad_s[POOL_HALO:POOL_HALO + length, :] = uu
        y = (pad_s[POOL_HALO - 1:POOL_HALO - 1 + length, :] * wc[0:1]
             + uu * wc[1:2]
             + pad_s[POOL_HALO + 1:POOL_HALO + 1 + length, :] * wc[2:3])
        yd_ref[0, start:start + length, :] = (
            pb_ref[0, start:start + length, :] * y).astype(yd_ref.dtype)


def _mix_cd(cd, w_pool_l, s_pool_l, w_conv_l, n_lat):
    b, t, _ = cd.shape
    segments = ((0, n_lat), (n_lat, t - n_lat))
    n_g = len(POOL_WINDOWS)

    def col(k):
        return pl.BlockSpec((1, t, LANES), lambda i, g: (i, 0, k * n_g + g))

    out_spec = pl.BlockSpec((1, t, LANES), lambda i, g: (i, 0, g))
    return pl.pallas_call(
        functools.partial(_mix_cd_kernel, segments=segments),
        grid=(b, n_g),
        in_specs=[
            col(0), col(1), col(2), col(3),
            pl.BlockSpec((None, POOL_GROUP, POOL_GROUP), lambda i, g: (g, 0, 0)),
            pl.BlockSpec((1, LANES), lambda i, g: (0, g)),
            pl.BlockSpec((3, LANES), lambda i, g: (0, g)),
        ],
        out_specs=[out_spec, out_spec],
        out_shape=[jax.ShapeDtypeStruct((b, t, BRANCH_W), BF16)] * 2,
        scratch_shapes=[pltpu.VMEM((max(n_lat, t - n_lat) + 2 * POOL_HALO, LANES), F32)],
        compiler_params=pltpu.CompilerParams(
            dimension_semantics=("parallel", "parallel"), vmem_limit_bytes=VMEM_LIMIT),
        name="mix_cd",
    )(cd, cd, cd, cd, w_pool_l, s_pool_l, w_conv_l)


def _merge_ffn_kernel(*refs, ctx_tile, split):
    ya_ref, yb_ref, yc_ref, yd_ref, gate_ref = refs[:5]
    if split:
        x_ref, xc_ref = refs[5:7]
        refs = refs[7:]
        x = jnp.where(pl.program_id(1) == ctx_tile, xc_ref[0], x_ref[0])
    else:
        x = refs[5][0]
        refs = refs[6:]
    mod_ref, wb_ref, wo_ref, g2_ref, w1_ref, w2_ref, o_ref = refs
    d = D_MODEL
    merged = None
    for n, y_ref in enumerate((ya_ref, yb_ref, yc_ref, yd_ref)):
        proj = _dot(y_ref[0], wb_ref[n])
        gate = jax.nn.sigmoid(gate_ref[0, :, n * d:(n + 1) * d])
        merged = gate * proj if merged is None else merged + gate * proj
    mix = _dot(merged.astype(BF16), wo_ref[...])
    x1 = x + mod_ref[:, 2 * d:3 * d] * mix

    y = _rms(x1) * g2_ref[...]
    h = (y * (1.0 + mod_ref[:, 4 * d:5 * d]) + mod_ref[:, 3 * d:4 * d]).astype(BF16)
    f = None
    for c0 in range(0, D_FF, d):
        a = jnp.maximum(_dot(h, w1_ref[:, c0:c0 + d]), 0.0)
        part = _dot((a * a).astype(BF16), w2_ref[c0:c0 + d, :])
        f = part if f is None else f + part
    o_ref[0] = x1 + mod_ref[:, 5 * d:6 * d] * f


def _merge_ffn(ys, gates, x_parts, mods_l, wb, wo, g2, w1, w2, tq, n_lat, n_ctx, with_ctx):
    b, _, d = x_parts[0].shape
    t = n_lat + n_ctx
    ctx_tile = n_lat // tq
    split = len(x_parts) == 2
    row = lambda i, j: (i, j, 0)
    y_spec = pl.BlockSpec((1, tq, BRANCH_W), row)
    nq, out_rows = (t // tq, t) if with_ctx else (ctx_tile, n_lat)
    aliases = {5: 0} if (with_ctx and not split) else {}
    return pl.pallas_call(
        functools.partial(_merge_ffn_kernel, ctx_tile=ctx_tile, split=split),
        grid=(b, nq),
        in_specs=[
            y_spec, y_spec, y_spec, y_spec,
            pl.BlockSpec((1, tq, N_BRANCH * d), row),
        ] + _stream_specs(split, tq, d, ctx_tile) + [
            pl.BlockSpec((None, 1, N_MOD * d), lambda i, j: (jnp.where(j == ctx_tile, b, i), 0, 0)),
            _const_spec((N_BRANCH, BRANCH_W, d)),
            _const_spec((d, d)),
            _const_spec((1, d)),
            _const_spec((d, D_FF)),
            _const_spec((D_FF, d)),
        ],
        out_specs=pl.BlockSpec((1, tq, d), row),
        out_shape=jax.ShapeDtypeStruct((b, out_rows, d), F32),
        input_output_aliases=aliases,
        compiler_params=pltpu.CompilerParams(
            dimension_semantics=("parallel", "parallel"), vmem_limit_bytes=VMEM_LIMIT),
        name="merge_ffn",
    )(*ys, gates, *x_parts, mods_l, wb, wo, g2, w1, w2)


def _rope_tables(n_lat, n_ctx, rot_dim, lane_lo, period, total=LANES):
    rows = n_lat // GRID_W
    row = np.repeat(np.arange(rows, dtype=np.float64), GRID_W)
    col = np.tile(np.arange(GRID_W, dtype=np.float64), rows)
    n_freq = rot_dim // 4
    inv = ROPE_BASE ** (-np.arange(n_freq, dtype=np.float64) / n_freq)
    inv = inv.astype(np.float32).astype(np.float64)
    ang = np.concatenate([row[:, None] * inv, col[:, None] * inv], axis=-1)
    ang = ang.astype(np.float32).astype(np.float64)
    half = rot_dim // 2
    t = n_lat + n_ctx
    cos = np.ones((t, total), np.float32)
    s_left = np.zeros((t, total), np.float32)
    s_right = np.zeros((t, total), np.float32)
    starts = [lane_lo] if period == 0 else list(range(lane_lo, total, period))
    for s0 in starts:
        cos[:n_lat, s0:s0 + half] = np.cos(ang)
        cos[:n_lat, s0 + half:s0 + rot_dim] = np.cos(ang)
        s_left[:n_lat, s0:s0 + half] = -np.sin(ang)
        s_right[:n_lat, s0 + half:s0 + rot_dim] = np.sin(ang)
    return jnp.asarray(cos), jnp.asarray(s_left), jnp.asarray(s_right)


def _relayout_w_in(w_in):
    sizes = (512, 512, 512, B_QLORA, B_KVLORA, B_ROPE, 512, 512, 512, 512, N_BRANCH * D_MODEL)
    offs = np.concatenate([[0], np.cumsum(sizes)])
    qa, ka, va, cq, ckv, kr, u, pb, pc, pxx, gates = [
        w_in[:, :, offs[i]:offs[i + 1]] for i in range(len(sizes))]
    kr_slot = jnp.pad(kr, ((0, 0), (0, 0), (B_NOPE, LANES - B_NOPE - B_ROPE)))
    return jnp.concatenate([cq, kr_slot, ckv, qa, ka, va, u, pb, pc, pxx, gates], axis=-1).astype(BF16)


def _pad_head_slots(v, width):
    lead = v.shape[:-1]
    v = v.reshape(lead + (-1, width))
    v = jnp.pad(v, [(0, 0)] * len(lead) + [(0, 0), (0, LANES - width)])
    return v.reshape(lead + (-1,))


def kernel(x, c, ctx, c_ctx, w_mod, b_mod, g_norm1, g_norm2, w_in, gq_a, gk_a, lam_a, g_sub_a,
           g_cq, w_uq, g_ckv, w_ukv, gq_b, gk_b, w_pool, s_pool, w_conv, w_branch, w_o,
           w_ff1, w_ff2):
    b, n_lat, d = x.shape
    n_ctx = ctx.shape[1]
    depth = w_mod.shape[0]
    tq = n_ctx
    assert d == D_MODEL and n_lat % ATT_SUB == 0 and n_ctx % ATT_SUB == 0
    assert n_lat % tq == 0 and tq % LANES == 0 and n_lat % GRID_W == 0

    rope_a = _rope_tables(n_lat, n_ctx, A_HD, 0, A_HD)
    rope_b = _rope_tables(n_lat, n_ctx, B_ROPE, B_NOPE, 0)

    mod_rows = -(-(b + 1) // 8) * 8
    cc = jnp.concatenate([c, c_ctx[None, :], jnp.zeros((mod_rows - b - 1, d), F32)], axis=0)
    mods = _modulation(cc, w_mod, b_mod).reshape(depth, mod_rows, 1, N_MOD * d)

    w_in_p = _relayout_w_in(w_in)
    w_uq_p = _pad_head_slots(w_uq, B_QK).astype(BF16)
    w_ukv_b = w_ukv.astype(BF16)
    w_pool_b = w_pool.astype(BF16)
    w_branch_b = w_branch.astype(BF16)
    w_o_b = w_o.astype(BF16)
    w_ff1_b = w_ff1.astype(BF16)
    w_ff2_b = w_ff2.astype(BF16)
    gq_a2 = jnp.tile(gq_a, (1, 2))[:, None, :]
    gk_a2 = jnp.tile(gk_a, (1, 2))[:, None, :]
    gq_bv = jnp.pad(gq_b, ((0, 0), (0, LANES - B_QK)))[:, None, :]
    gk_bv = jnp.pad(gk_b, ((0, 0), (0, LANES - B_QK)))[:, None, :]

    x_parts = (x, ctx)
    for l in range(depth):
        last = l == depth - 1
        lam_init = 0.8 - 0.6 * math.exp(-0.3 * l)
        gates, cd, qa, ka, vxa, qb, kb, vxb = _inproj(
            x_parts, mods[l], g_norm1[l][None, :], w_in_p[l], w_uq_p[l], w_ukv_b[l],
            g_cq[l][None, :], g_ckv[l][None, :], gq_a2[l], gk_a2[l], gq_bv[l], gk_bv[l],
            rope_a, rope_b, tq, n_lat, n_ctx)
        ya = _attn_a(qa, ka, vxa, lam_a[l], g_sub_a[l][None, :], n_lat, lam_init)
        yb = _attn_b(qb, kb, vxb, n_lat)
        yc, yd = _mix_cd(cd, w_pool_b[l], s_pool[l][None, :], w_conv[l], n_lat)
        x_parts = (_merge_ffn((ya, yb, yc, yd), gates, x_parts, mods[l], w_branch_b[l], w_o_b[l],
                              g_norm2[l][None, :], w_ff1_b[l], w_ff2_b[l], tq, n_lat, n_ctx,
                              not last),)
    return x_parts[0]
```

```python
import functools
import math

import numpy as np
import jax
import jax.numpy as jnp
from jax import lax
from jax.experimental import pallas as pl
from jax.experimental.pallas import tpu as pltpu

F32 = jnp.float32
BF16 = jnp.bfloat16

D_MODEL = 1024
GRID_W = 64
ROPE_BASE = 10000.0
EPS = 1e-6
LOG2E = math.log2(math.e)

A_HEADS = 4
A_HD = 64
A_VD = 128
B_HEADS = 8
B_NOPE = 64
B_ROPE = 32
B_QK = B_NOPE + B_ROPE
B_VD = 64
B_QLORA = 384
B_KVLORA = 256
POOL_WINDOWS = (2, 4, 8, 16)
POOL_GROUP = 128
POOL_MARGIN = 16
BRANCH_W = 512
N_BRANCH = 4
D_FF = 4 * D_MODEL
N_MOD = 6

LANES = 128

QA_OFF = 0
KA_OFF = QA_OFF + 512
VA_OFF = KA_OFF + 512
CQ_OFF = VA_OFF + 512
CKV_OFF = CQ_OFF + B_QLORA
KR_OFF = CKV_OFF + B_KVLORA
CD_OFF = KR_OFF + LANES
CD_W = 4 * 512
GATE_OFF = CD_OFF + CD_W
D_IN_P = GATE_OFF + N_BRANCH * D_MODEL
IN_FILL = 256

ATT_SUB = 128

VMEM_LIMIT = 56 * 1024 * 1024


def _dot(a, b):
    return jnp.dot(a, b, preferred_element_type=F32)


def _dot_nt(a, b):
    return lax.dot_general(a, b, (((1,), (1,)), ((), ())), preferred_element_type=F32)


def _lane_iota(n=LANES):
    return lax.broadcasted_iota(jnp.int32, (1, n), 1)


def _rms(x):
    return x * lax.rsqrt(jnp.mean(x * x, axis=-1, keepdims=True) + EPS)


def _rope(t, tabs, half):
    cos, s_left, s_right = tabs
    n = t.shape[-1]
    return (t * cos + pltpu.roll(t, n - half, 1) * s_left
            + pltpu.roll(t, half, 1) * s_right)


def _const_spec(shape):
    nd = len(shape)
    return pl.BlockSpec(shape, lambda *_: (0,) * nd, pipeline_mode=pl.Buffered(1))


def _stream_specs(split, tq, d, ctx_tile):
    if split:
        return [pl.BlockSpec((1, tq, d), lambda i, j: (i, jnp.minimum(j, ctx_tile - 1), 0)),
                pl.BlockSpec((1, tq, d), lambda i, j: (i, 0, 0))]
    return [pl.BlockSpec((1, tq, d), lambda i, j: (i, j, 0))]


def _mod_kernel(c_ref, w_ref, b_ref, o_ref):
    c = c_ref[...]
    h = (c * jax.nn.sigmoid(c)).astype(BF16)
    o_ref[...] = _dot(h, w_ref[...].astype(BF16)) + b_ref[...]


def _modulation(cc, w_mod, b_mod):
    depth, d, n = w_mod.shape
    rows = cc.shape[0]
    tn = 1536
    return pl.pallas_call(
        _mod_kernel,
        grid=(depth, n // tn),
        in_specs=[
            pl.BlockSpec((rows, d), lambda l, j: (0, 0)),
            pl.BlockSpec((None, d, tn), lambda l, j: (l, 0, j)),
            pl.BlockSpec((None, 1, tn), lambda l, j: (l, 0, j)),
        ],
        out_specs=pl.BlockSpec((None, rows, tn), lambda l, j: (l, 0, j)),
        out_shape=jax.ShapeDtypeStruct((depth, rows, n), F32),
        compiler_params=pltpu.CompilerParams(
            dimension_semantics=("parallel", "parallel"), vmem_limit_bytes=VMEM_LIMIT),
        name="modulation",
    )(cc, w_mod, b_mod.reshape(depth, 1, n))


def _inproj_kernel(*refs, ctx_tile, split):
    if split:
        x_ref, xc_ref = refs[:2]
        refs = refs[2:]
        x = jnp.where(pl.program_id(1) == ctx_tile, xc_ref[0], x_ref[0])
    else:
        x = refs[0][0]
        refs = refs[1:]
    (mod_ref, g_ref, w_ref, wuq_ref, wukv_ref, gcq_ref, gckv_ref,
     gqa_ref, gka_ref, gqb_ref, gkb_ref, ca_ref, la_ref, ra_ref, cb_ref, lb_ref, rb_ref,
     gate_ref, cd_ref, qa_ref, ka_ref, vxa_ref, qb_ref, kb_ref, vxb_ref) = refs
    d = D_MODEL
    y = _rms(x) * g_ref[...]
    h = (y * (1.0 + mod_ref[:, d:2 * d]) + mod_ref[:, 0:d]).astype(BF16)
    tq = h.shape[0]

    def proj(off, width):
        return _dot(h, w_ref[:, off:off + width])

    fill = [(gate_ref, GATE_OFF, c0) for c0 in range(0, N_BRANCH * d, IN_FILL)]
    fill += [(cd_ref, CD_OFF, c0) for c0 in range(0, CD_W, IN_FILL)]

    def emit_fill(n=1):
        for _ in range(n):
            if fill:
                o_ref, off, c0 = fill.pop(0)
                o_ref[0, :, c0:c0 + IN_FILL] = proj(off + c0, IN_FILL)

    lane = _lane_iota()
    lo = lane < A_HD
    nope = lane < B_NOPE
    rope_a = (ca_ref[...], la_ref[...], ra_ref[...])
    rope_b = (cb_ref[...], lb_ref[...], rb_ref[...])

    cq = proj(CQ_OFF, B_QLORA)
    emit_fill(2)
    q_all = _dot((_rms(cq) * gcq_ref[...]).astype(BF16), wuq_ref[...])
    gq_b = gqb_ref[...] * (B_QK ** -0.5 * LOG2E)
    for hd in range(B_HEADS):
        q = q_all[:, hd * LANES:(hd + 1) * LANES]
        q2 = q * q
        s_n = jnp.sum(jnp.where(nope, q2, 0.0), axis=-1, keepdims=True)
        s_r = jnp.sum(jnp.where(nope, 0.0, q2), axis=-1, keepdims=True)
        inv = jnp.where(nope, lax.rsqrt(s_n * (1.0 / B_NOPE) + EPS),
                        lax.rsqrt(s_r * (1.0 / B_ROPE) + EPS))
        qb_ref[0, :, hd * LANES:(hd + 1) * LANES] = _rope(
            q * inv * gq_b, rope_b, B_ROPE // 2).astype(BF16)
        emit_fill()

    kr = proj(KR_OFF, LANES)
    krn = kr * lax.rsqrt(jnp.sum(kr * kr, axis=-1, keepdims=True) * (1.0 / B_ROPE) + EPS)
    krn = _rope(krn * jnp.where(nope, 0.0, gkb_ref[...]), rope_b, B_ROPE // 2)
    gk_nope = jnp.where(nope, gkb_ref[...], 0.0)
    ckv = proj(CKV_OFF, B_KVLORA)
    emit_fill()
    kv_all = _dot((_rms(ckv) * gckv_ref[...]).astype(BF16), wukv_ref[...])
    for hd in range(B_HEADS):
        kv = kv_all[:, hd * LANES:(hd + 1) * LANES]
        ss = jnp.sum(jnp.where(nope, kv * kv, 0.0), axis=-1, keepdims=True)
        kn = kv * lax.rsqrt(ss * (1.0 / B_NOPE) + EPS) * gk_nope
        kb_ref[0, :, hd * LANES:(hd + 1) * LANES] = (kn + krn).astype(BF16)
        vxb_ref[0, :, hd * LANES:(hd + 1) * LANES] = jnp.where(nope, 1.0, kv).astype(BF16)
        if hd % 2:
            emit_fill()

    def norm_halves(x):
        x2 = x * x
        s_lo = jnp.sum(jnp.where(lo, x2, 0.0), axis=-1, keepdims=True)
        s_hi = jnp.sum(jnp.where(lo, 0.0, x2), axis=-1, keepdims=True)
        return x * jnp.where(lo, lax.rsqrt(s_lo * (1.0 / A_HD) + EPS),
                             lax.rsqrt(s_hi * (1.0 / A_HD) + EPS))

    gq_a = gqa_ref[...] * (A_HD ** -0.5 * LOG2E)
    for off, gain, o_ref in ((QA_OFF, gq_a, qa_ref), (KA_OFF, gka_ref[...], ka_ref)):
        t = proj(off, A_HEADS * LANES)
        for hd in range(A_HEADS):
            sl = slice(hd * LANES, (hd + 1) * LANES)
            o_ref[0, :, sl] = _rope(norm_halves(t[:, sl]) * gain, rope_a, A_HD // 2).astype(BF16)
            emit_fill()
    va = proj(VA_OFF, A_HEADS * A_VD).astype(BF16)
    ones = jnp.ones((tq, A_VD), BF16)
    for hd in range(A_HEADS):
        vxa_ref[0, :, 2 * hd * A_VD:(2 * hd + 1) * A_VD] = va[:, hd * A_VD:(hd + 1) * A_VD]
        vxa_ref[0, :, (2 * hd + 1) * A_VD:(2 * hd + 2) * A_VD] = ones
    emit_fill(len(fill))


def _inproj(x_parts, mods_l, g1, w_in_p, wuq_p, wukv, gcq, gckv, gqa, gka, gqb, gkb,
            rope_a, rope_b, tq, n_lat, n_ctx):
    b, _, d = x_parts[0].shape
    t = n_lat + n_ctx
    nt = t // tq
    ctx_tile = n_lat // tq
    split = len(x_parts) == 2
    row = lambda i, j: (i, j, 0)
    tab = pl.BlockSpec((tq, LANES), lambda i, j: (j, 0))
    vec = lambda n: _const_spec((1, n))

    def out(width, dtype):
        return (pl.BlockSpec((1, tq, width), row), jax.ShapeDtypeStruct((b, t, width), dtype))

    outs = [out(N_BRANCH * d, F32), out(CD_W, F32), out(A_HEADS * LANES, BF16),
            out(A_HEADS * LANES, BF16), out(2 * A_HEADS * A_VD, BF16),
            out(B_HEADS * LANES, BF16), out(B_HEADS * LANES, BF16), out(B_HEADS * LANES, BF16)]
    return pl.pallas_call(
        functools.partial(_inproj_kernel, ctx_tile=ctx_tile, split=split),
        grid=(b, nt),
        in_specs=_stream_specs(split, tq, d, ctx_tile) + [
            pl.BlockSpec((None, 1, N_MOD * d), lambda i, j: (jnp.where(j == ctx_tile, b, i), 0, 0)),
            vec(d),
            _const_spec((d, D_IN_P)),
            _const_spec((B_QLORA, B_HEADS * LANES)),
            _const_spec((B_KVLORA, B_HEADS * LANES)),
            vec(B_QLORA), vec(B_KVLORA), vec(LANES), vec(LANES), vec(LANES), vec(LANES),
            tab, tab, tab, tab, tab, tab,
        ],
        out_specs=[o[0] for o in outs],
        out_shape=[o[1] for o in outs],
        compiler_params=pltpu.CompilerParams(
            dimension_semantics=("parallel", "parallel"), vmem_limit_bytes=VMEM_LIMIT),
        name="inproj",
    )(*x_parts, mods_l, g1, w_in_p, wuq_p, wukv, gcq, gckv, gqa, gka, gqb, gkb, *rope_a, *rope_b)


def _softmax_pv(q, k, vx):
    s = _dot_nt(q, k)
    p = jnp.exp2(s - jnp.max(s, axis=-1, keepdims=True)).astype(BF16)
    return _dot(p, vx)


def _sweep(tile, n_lat, t_all):
    for s0 in range(0, t_all, ATT_SUB):
        tile(s0, 0 if s0 < n_lat else n_lat)


def _fill_padded(buf, src):
    m, n = POOL_MARGIN, src.shape[0]
    zeros = jnp.zeros((m, LANES), F32)
    buf[0:m, :] = zeros
    buf[m:m + n, :] = src
    buf[m + n:2 * m + n, :] = zeros


def _pool_mixer(u_ref, dd_ref, buf_a, buf_b, g, segments):
    m = POOL_MARGIN
    hot = [jnp.where(g == k, 1.0, 0.0).astype(F32) for k in range(len(POOL_WINDOWS))]
    half = lax.shift_left(jnp.int32(1), g)
    for start, length in segments:
        lo_, hi_ = m // 2, length + m + m // 2
        zeros = jnp.zeros((m, LANES), F32)
        u = u_ref[0, start:start + length, :]
        buf_a[0:m, :] = zeros
        buf_a[m:m + length, :] = u
        buf_a[m + length:2 * m + length, :] = zeros
        buf_b[0:m, :] = zeros
        buf_b[m + length:2 * m + length, :] = zeros

        def centre(w):
            return w[m - lo_:m - lo_ + length]

        w = buf_a[lo_ - 1:hi_ - 1, :] + buf_a[lo_:hi_, :]
        sel = hot[0] * centre(w)
        src, dst = buf_b, buf_a
        src[lo_:hi_, :] = w
        for k in range(1, len(POOL_WINDOWS)):
            sh = POOL_WINDOWS[k] // 4
            w = src[lo_ - sh:hi_ - sh, :] + src[lo_ + sh:hi_ + sh, :]
            sel = sel + hot[k] * centre(w)
            if k + 1 < len(POOL_WINDOWS):
                dst[lo_:hi_, :] = w
                src, dst = dst, src
        rows = lax.broadcasted_iota(jnp.int32, (length, 1), 0)
        cnt = (jnp.minimum(rows + half, length) - jnp.maximum(rows - half, 0)).astype(F32)
        dd_ref[0, start:start + length, :] = (sel / cnt - u).astype(dd_ref.dtype)


def _conv_chunk(bufs, pb_ref, wc_ref, yd_ref, s0, n_lat):
    seg0, buf = (0, bufs[0]) if s0 < n_lat else (n_lat, bufs[1])
    base = POOL_MARGIN + s0 - seg0
    wc = wc_ref[...]
    y = (buf[base - 1:base - 1 + ATT_SUB, :] * wc[0:1] + buf[base:base + ATT_SUB, :] * wc[1:2]
         + buf[base + 1:base + 1 + ATT_SUB, :] * wc[2:3])
    yd_ref[0, s0:s0 + ATT_SUB, :] = (pb_ref[0, s0:s0 + ATT_SUB, :] * y).astype(yd_ref.dtype)


def _attn_a_kernel(q_ref, k_ref, vx_ref, lam_ref, gsub_ref, u_ref,
                   o_ref, dd_ref, buf_a, buf_b, *, n_lat, lam_init):
    t_all = k_ref.shape[1]
    _pool_mixer(u_ref, dd_ref, buf_a, buf_b, pl.program_id(1),
                ((0, n_lat), (n_lat, t_all - n_lat)))

    lo = _lane_iota() < A_HD
    la = lam_ref[...].astype(F32)
    lam = (jnp.exp(jnp.sum(la[0:1] * la[1:2], axis=-1, keepdims=True))
           - jnp.exp(jnp.sum(la[2:3] * la[3:4], axis=-1, keepdims=True)) + lam_init)
    post = gsub_ref[...] * (1.0 - lam_init)

    def tile(r0, key_lo):
        q = q_ref[0, r0:r0 + ATT_SUB, :]
        k = k_ref[0, key_lo:t_all, :]
        vx = vx_ref[0, key_lo:t_all, :]
        zero = jnp.zeros_like(q)
        r1 = _softmax_pv(jnp.where(lo, q, zero), k, vx)
        r2 = _softmax_pv(jnp.where(lo, zero, q), k, vx)
        o = (r1[:, 0:A_VD] / r1[:, A_VD:A_VD + 1]
             - lam * (r2[:, 0:A_VD] / r2[:, A_VD:A_VD + 1]))
        o_ref[0, r0:r0 + ATT_SUB, :] = (_rms(o) * post).astype(o_ref.dtype)

    _sweep(tile, n_lat, t_all)


def _attn_a(qa, ka, vxa, lam_l, gsub, cd, n_lat, lam_init):
    b, t, _ = qa.shape
    head = lambda i, h: (i, 0, h)
    out_spec = pl.BlockSpec((1, t, LANES), head)
    return pl.pallas_call(
        functools.partial(_attn_a_kernel, n_lat=n_lat, lam_init=lam_init),
        grid=(b, A_HEADS),
        in_specs=[
            pl.BlockSpec((1, t, LANES), head),
            pl.BlockSpec((1, t, LANES), head),
            pl.BlockSpec((1, t, 2 * A_VD), head),
            pl.BlockSpec((4, A_HD), lambda i, h: (0, 0)),
            pl.BlockSpec((1, LANES), lambda i, h: (0, 0)),
            pl.BlockSpec((1, t, LANES), head),
        ],
        out_specs=[out_spec, out_spec],
        out_shape=[jax.ShapeDtypeStruct((b, t, BRANCH_W), BF16)] * 2,
        scratch_shapes=[pltpu.VMEM((max(n_lat, t - n_lat) + 2 * POOL_MARGIN, LANES), F32)] * 2,
        compiler_params=pltpu.CompilerParams(
            dimension_semantics=("parallel", "parallel"), vmem_limit_bytes=VMEM_LIMIT),
        name="attn_a",
    )(qa, ka, vxa, lam_l, gsub, cd)


def _attn_b_kernel(q_ref, k_ref, vx_ref, pb_ref, pc_ref, pxx_ref, wc_ref,
                   o_ref, yd_ref, buf_lat, buf_ctx, *, n_lat):
    t_all = k_ref.shape[1]
    bufs = (buf_lat, buf_ctx)
    _fill_padded(buf_lat, pc_ref[0, 0:n_lat, :] * pxx_ref[0, 0:n_lat, :])
    _fill_padded(buf_ctx, pc_ref[0, n_lat:t_all, :] * pxx_ref[0, n_lat:t_all, :])
    nope = _lane_iota() < B_NOPE

    def tile(r0, key_lo):
        outs = []
        for hh in range(2):
            sl = slice(hh * LANES, (hh + 1) * LANES)
            r = _softmax_pv(q_ref[0, r0:r0 + ATT_SUB, sl], k_ref[0, key_lo:t_all, sl],
                            vx_ref[0, key_lo:t_all, :])[:, sl]
            outs.append(r / r[:, 0:1])
        o = jnp.where(nope, pltpu.roll(outs[0], B_VD, 1), outs[1])
        o_ref[0, r0:r0 + ATT_SUB, :] = o.astype(o_ref.dtype)
        _conv_chunk(bufs, pb_ref, wc_ref, yd_ref, r0, n_lat)

    _sweep(tile, n_lat, t_all)


def _attn_b(qb, kb, vxb, cd, w_conv_l, n_lat):
    b, t, _ = qb.shape
    pair = lambda i, h: (i, 0, h)
    n_g = BRANCH_W // LANES
    spec = pl.BlockSpec((1, t, 2 * LANES), pair)
    out_spec = pl.BlockSpec((1, t, LANES), pair)

    def col(k):
        return pl.BlockSpec((1, t, LANES), lambda i, h: (i, 0, k * n_g + h))

    return pl.pallas_call(
        functools.partial(_attn_b_kernel, n_lat=n_lat),
        grid=(b, B_HEADS // 2),
        in_specs=[spec, spec, spec, col(1), col(2), col(3),
                  pl.BlockSpec((3, LANES), lambda i, h: (0, h))],
        out_specs=[out_spec, out_spec],
        out_shape=[jax.ShapeDtypeStruct((b, t, BRANCH_W), BF16)] * 2,
        scratch_shapes=[pltpu.VMEM((n + 2 * POOL_MARGIN, LANES), F32) for n in (n_lat, t - n_lat)],
        compiler_params=pltpu.CompilerParams(
            dimension_semantics=("parallel", "parallel"), vmem_limit_bytes=VMEM_LIMIT),
        name="attn_b",
    )(qb, kb, vxb, cd, cd, cd, w_conv_l)


def _merge_ffn_kernel(*refs, ctx_tile, split):
    ya_ref, yb_ref, yc_ref, yd_ref, gate_ref = refs[:5]
    if split:
        x_ref, xc_ref = refs[5:7]
        refs = refs[7:]
        x = jnp.where(pl.program_id(1) == ctx_tile, xc_ref[0], x_ref[0])
    else:
        x = refs[5][0]
        refs = refs[6:]
    mod_ref, wp_ref, sp_ref, wb_ref, wo_ref, g2_ref, w1_ref, w2_ref, o_ref = refs
    d = D_MODEL
    yc = jnp.concatenate(
        [_dot(yc_ref[0, :, gi * POOL_GROUP:(gi + 1) * POOL_GROUP], wp_ref[gi])
         for gi in range(len(POOL_WINDOWS))], axis=-1) * sp_ref[...]
    merged = None
    for n, y in enumerate((ya_ref[0], yb_ref[0], yc.astype(BF16), yd_ref[0])):
        proj = _dot(y, wb_ref[n])
        gate = jax.nn.sigmoid(gate_ref[0, :, n * d:(n + 1) * d])
        merged = gate * proj if merged is None else merged + gate * proj
    mix = _dot(merged.astype(BF16), wo_ref[...])
    x1 = x + mod_ref[:, 2 * d:3 * d] * mix

    y = _rms(x1) * g2_ref[...]
    h = (y * (1.0 + mod_ref[:, 4 * d:5 * d]) + mod_ref[:, 3 * d:4 * d]).astype(BF16)
    f = None
    for c0 in range(0, D_FF, d):
        a = jnp.maximum(_dot(h, w1_ref[:, c0:c0 + d]), 0.0)
        part = _dot((a * a).astype(BF16), w2_ref[c0:c0 + d, :])
        f = part if f is None else f + part
    o_ref[0] = x1 + mod_ref[:, 5 * d:6 * d] * f


def _merge_ffn(ys, gates, x_parts, mods_l, wp, sp, wb, wo, g2, w1, w2, tq, n_lat, n_ctx, with_ctx):
    b, _, d = x_parts[0].shape
    t = n_lat + n_ctx
    ctx_tile = n_lat // tq
    split = len(x_parts) == 2
    row = lambda i, j: (i, j, 0)
    y_spec = pl.BlockSpec((1, tq, BRANCH_W), row)
    nq, out_rows = (t // tq, t) if with_ctx else (ctx_tile, n_lat)
    aliases = {5: 0} if (with_ctx and not split) else {}
    return pl.pallas_call(
        functools.partial(_merge_ffn_kernel, ctx_tile=ctx_tile, split=split),
        grid=(b, nq),
        in_specs=[
            y_spec, y_spec, y_spec, y_spec,
            pl.BlockSpec((1, tq, N_BRANCH * d), row),
        ] + _stream_specs(split, tq, d, ctx_tile) + [
            pl.BlockSpec((None, 1, N_MOD * d), lambda i, j: (jnp.where(j == ctx_tile, b, i), 0, 0)),
            _const_spec((len(POOL_WINDOWS), POOL_GROUP, POOL_GROUP)),
            _const_spec((1, BRANCH_W)),
            _const_spec((N_BRANCH, BRANCH_W, d)),
            _const_spec((d, d)),
            _const_spec((1, d)),
            _const_spec((d, D_FF)),
            _const_spec((D_FF, d)),
        ],
        out_specs=pl.BlockSpec((1, tq, d), row),
        out_shape=jax.ShapeDtypeStruct((b, out_rows, d), F32),
        input_output_aliases=aliases,
        compiler_params=pltpu.CompilerParams(
            dimension_semantics=("parallel", "parallel"), vmem_limit_bytes=VMEM_LIMIT),
        name="merge_ffn",
    )(*ys, gates, *x_parts, mods_l, wp, sp, wb, wo, g2, w1, w2)


def _rope_tables(n_lat, n_ctx, rot_dim, lane_lo, period, total=LANES):
    rows = n_lat // GRID_W
    row = np.repeat(np.arange(rows, dtype=np.float64), GRID_W)
    col = np.tile(np.arange(GRID_W, dtype=np.float64), rows)
    n_freq = rot_dim // 4
    inv = ROPE_BASE ** (-np.arange(n_freq, dtype=np.float64) / n_freq)
    inv = inv.astype(np.float32).astype(np.float64)
    ang = np.concatenate([row[:, None] * inv, col[:, None] * inv], axis=-1)
    ang = ang.astype(np.float32).astype(np.float64)
    half = rot_dim // 2
    t = n_lat + n_ctx
    cos = np.ones((t, total), np.float32)
    s_left = np.zeros((t, total), np.float32)
    s_right = np.zeros((t, total), np.float32)
    starts = [lane_lo] if period == 0 else list(range(lane_lo, total, period))
    for s0 in starts:
        cos[:n_lat, s0:s0 + half] = np.cos(ang)
        cos[:n_lat, s0 + half:s0 + rot_dim] = np.cos(ang)
        s_left[:n_lat, s0:s0 + half] = -np.sin(ang)
        s_right[:n_lat, s0 + half:s0 + rot_dim] = np.sin(ang)
    return jnp.asarray(cos), jnp.asarray(s_left), jnp.asarray(s_right)


def _relayout_w_in(w_in):
    kr_lo = KR_OFF
    w = w_in.astype(BF16)
    z = lambda n: jnp.zeros(w.shape[:2] + (n,), BF16)
    return jnp.concatenate([w[:, :, :kr_lo], z(B_NOPE), w[:, :, kr_lo:kr_lo + B_ROPE],
                            z(LANES - B_NOPE - B_ROPE), w[:, :, kr_lo + B_ROPE:]], axis=-1)


def _pad_head_slots(v, width):
    lead = v.shape[:-1]
    v = v.reshape(lead + (-1, width))
    v = jnp.pad(v, [(0, 0)] * len(lead) + [(0, 0), (0, LANES - width)])
    return v.reshape(lead + (-1,))


def kernel(x, c, ctx, c_ctx, w_mod, b_mod, g_norm1, g_norm2, w_in, gq_a, gk_a, lam_a, g_sub_a,
           g_cq, w_uq, g_ckv, w_ukv, gq_b, gk_b, w_pool, s_pool, w_conv, w_branch, w_o,
           w_ff1, w_ff2):
    b, n_lat, d = x.shape
    n_ctx = ctx.shape[1]
    depth = w_mod.shape[0]
    tq = n_ctx
    assert d == D_MODEL and n_lat % ATT_SUB == 0 and n_ctx % ATT_SUB == 0
    assert n_lat % tq == 0 and tq % LANES == 0 and n_lat % GRID_W == 0

    rope_a = _rope_tables(n_lat, n_ctx, A_HD, 0, A_HD)
    rope_b = _rope_tables(n_lat, n_ctx, B_ROPE, B_NOPE, 0)

    mod_rows = -(-(b + 1) // 8) * 8
    cc = jnp.concatenate([c, c_ctx[None, :], jnp.zeros((mod_rows - b - 1, d), F32)], axis=0)
    mods = _modulation(cc, w_mod, b_mod).reshape(depth, mod_rows, 1, N_MOD * d)

    w_in_p = _relayout_w_in(w_in)
    w_uq_p = _pad_head_slots(w_uq, B_QK).astype(BF16)
    w_ukv_b = w_ukv.astype(BF16)
    w_pool_b = w_pool.astype(BF16)
    w_branch_b = w_branch.astype(BF16)
    w_o_b = w_o.astype(BF16)
    w_ff1_b = w_ff1.astype(BF16)
    w_ff2_b = w_ff2.astype(BF16)
    gq_a2 = jnp.tile(gq_a, (1, 2))[:, None, :]
    gk_a2 = jnp.tile(gk_a, (1, 2))[:, None, :]
    gq_bv = jnp.pad(gq_b, ((0, 0), (0, LANES - B_QK)))[:, None, :]
    gk_bv = jnp.pad(gk_b, ((0, 0), (0, LANES - B_QK)))[:, None, :]

    x_parts = (x, ctx)
    for l in range(depth):
        last = l == depth - 1
        lam_init = 0.8 - 0.6 * math.exp(-0.3 * l)
        gates, cd, qa, ka, vxa, qb, kb, vxb = _inproj(
            x_parts, mods[l], g_norm1[l][None, :], w_in_p[l], w_uq_p[l], w_ukv_b[l],
            g_cq[l][None, :], g_ckv[l][None, :], gq_a2[l], gk_a2[l], gq_bv[l], gk_bv[l],
            rope_a, rope_b, tq, n_lat, n_ctx)
        ya, yc = _attn_a(qa, ka, vxa, lam_a[l], g_sub_a[l][None, :], cd, n_lat, lam_init)
        yb, yd = _attn_b(qb, kb, vxb, cd, w_conv[l], n_lat)
        x_parts = (_merge_ffn((ya, yb, yc, yd), gates, x_parts, mods[l], w_pool_b[l],
                              s_pool[l][None, :], w_branch_b[l], w_o_b[l],
                              g_norm2[l][None, :], w_ff1_b[l], w_ff2_b[l], tq, n_lat, n_ctx,
                              not last),)
    return x_parts[0]
```

```python
import functools
import math

import numpy as np
import jax
import jax.numpy as jnp
from jax import lax
from jax.experimental import pallas as pl
from jax.experimental.pallas import tpu as pltpu

F32 = jnp.float32
BF16 = jnp.bfloat16

D_MODEL = 1024
GRID_W = 64
ROPE_BASE = 10000.0
EPS = 1e-6
LOG2E = math.log2(math.e)

A_HEADS = 4
A_HD = 64
A_VD = 128
B_HEADS = 8
B_NOPE = 64
B_ROPE = 32
B_QK = B_NOPE + B_ROPE
B_VD = 64
B_QLORA = 384
B_KVLORA = 256
POOL_WINDOWS = (2, 4, 8, 16)
POOL_GROUP = 128
POOL_HALO = 8
BRANCH_W = 512
N_BRANCH = 4
D_FF = 4 * D_MODEL
N_MOD = 6

LANES = 128

QA_OFF = 0
KA_OFF = QA_OFF + 512
VA_OFF = KA_OFF + 512
CQ_OFF = VA_OFF + 512
CKV_OFF = CQ_OFF + B_QLORA
HEAD_W = CKV_OFF + B_KVLORA
CD_OFF = 0
CD_W = 4 * 512
GATE_OFF = CD_OFF + CD_W
TAIL_W = GATE_OFF + N_BRANCH * D_MODEL
IN_FILL = 256

ATT_SUB = 128

VMEM_LIMIT = 56 * 1024 * 1024


def _dot(a, b):
    return jnp.dot(a, b, preferred_element_type=F32)


def _dot_nt(a, b):
    return lax.dot_general(a, b, (((1,), (1,)), ((), ())), preferred_element_type=F32)


def _lane_iota(n=LANES):
    return lax.broadcasted_iota(jnp.int32, (1, n), 1)


def _rms(x):
    return x * lax.rsqrt(jnp.mean(x * x, axis=-1, keepdims=True) + EPS)


def _rope(t, tabs, half):
    cos, s_left, s_right = tabs
    n = t.shape[-1]
    return (t * cos + pltpu.roll(t, n - half, 1) * s_left
            + pltpu.roll(t, half, 1) * s_right)


def _layer_spec(l, shape):
    nd = len(shape)
    return pl.BlockSpec((None,) + tuple(shape), lambda *_: (l,) + (0,) * nd,
                        pipeline_mode=pl.Buffered(1))


def _stream_specs(split, tq, d, ctx_tile):
    if split:
        return [pl.BlockSpec((1, tq, d), lambda i, j: (i, jnp.minimum(j, ctx_tile - 1), 0)),
                pl.BlockSpec((1, tq, d), lambda i, j: (i, 0, 0))]
    return [pl.BlockSpec((1, tq, d), lambda i, j: (i, j, 0))]


def _mod_spec(l, b, ctx_tile):
    return pl.BlockSpec((None, None, 1, N_MOD * D_MODEL),
                        lambda i, j: (l, jnp.where(j == ctx_tile, b, i), 0, 0))


def _mod_kernel(c_ref, w_ref, b_ref, o_ref):
    c = c_ref[...]
    h = (c * jax.nn.sigmoid(c)).astype(BF16)
    o_ref[...] = _dot(h, w_ref[...].astype(BF16)) + b_ref[...]


def _modulation(cc, w_mod, b_mod):
    depth, d, n = w_mod.shape
    rows = cc.shape[0]
    tn = 1536
    return pl.pallas_call(
        _mod_kernel,
        grid=(depth, n // tn),
        in_specs=[
            pl.BlockSpec((rows, d), lambda l, j: (0, 0)),
            pl.BlockSpec((None, d, tn), lambda l, j: (l, 0, j)),
            pl.BlockSpec((None, 1, tn), lambda l, j: (l, 0, j)),
        ],
        out_specs=pl.BlockSpec((None, rows, tn), lambda l, j: (l, 0, j)),
        out_shape=jax.ShapeDtypeStruct((depth, rows, n), F32),
        compiler_params=pltpu.CompilerParams(
            dimension_semantics=("parallel", "parallel"), vmem_limit_bytes=VMEM_LIMIT),
        name="modulation",
    )(cc, w_mod, b_mod.reshape(depth, 1, n))


def _inproj_kernel(*refs, ctx_tile, split):
    if split:
        x_ref, xc_ref = refs[:2]
        refs = refs[2:]
        x = jnp.where(pl.program_id(1) == ctx_tile, xc_ref[0], x_ref[0])
    else:
        x = refs[0][0]
        refs = refs[1:]
    (mod_ref, g_ref, wh_ref, wkr_ref, wt_ref, wuq_ref, wukv_ref, gcq_ref, gckv_ref,
     gqa_ref, gka_ref, gqb_ref, gkb_ref, ca_ref, la_ref, ra_ref, cb_ref, lb_ref, rb_ref,
     gate_ref, cd_ref, qa_ref, ka_ref, vxa_ref, qb_ref, kb_ref, vxb_ref) = refs
    d = D_MODEL
    y = _rms(x) * g_ref[...]
    h = (y * (1.0 + mod_ref[:, d:2 * d]) + mod_ref[:, 0:d]).astype(BF16)
    tq = h.shape[0]

    def head(off, width):
        return _dot(h, wh_ref[:, off:off + width])

    fill = [(gate_ref, GATE_OFF, c0) for c0 in range(0, N_BRANCH * d, IN_FILL)]
    fill += [(cd_ref, CD_OFF, c0) for c0 in range(0, CD_W, IN_FILL)]

    def emit_fill(n=1):
        for _ in range(n):
            if fill:
                o_ref, off, c0 = fill.pop(0)
                o_ref[0, :, c0:c0 + IN_FILL] = _dot(h, wt_ref[:, off + c0:off + c0 + IN_FILL])

    lane = _lane_iota()
    lo = lane < A_HD
    nope = lane < B_NOPE
    rope_a = (ca_ref[...], la_ref[...], ra_ref[...])
    rope_b = (cb_ref[...], lb_ref[...], rb_ref[...])

    cq = head(CQ_OFF, B_QLORA)
    emit_fill(2)
    q_all = _dot((_rms(cq) * gcq_ref[...]).astype(BF16), wuq_ref[...])
    gq_b = gqb_ref[...] * (B_QK ** -0.5 * LOG2E)
    for hd in range(B_HEADS):
        q = q_all[:, hd * LANES:(hd + 1) * LANES]
        q2 = q * q
        s_n = jnp.sum(jnp.where(nope, q2, 0.0), axis=-1, keepdims=True)
        s_r = jnp.sum(jnp.where(nope, 0.0, q2), axis=-1, keepdims=True)
        inv = jnp.where(nope, lax.rsqrt(s_n * (1.0 / B_NOPE) + EPS),
                        lax.rsqrt(s_r * (1.0 / B_ROPE) + EPS))
        qb_ref[0, :, hd * LANES:(hd + 1) * LANES] = _rope(
            q * inv * gq_b, rope_b, B_ROPE // 2).astype(BF16)
        emit_fill()

    kr = _dot(h, wkr_ref[...])
    krn = kr * lax.rsqrt(jnp.sum(kr * kr, axis=-1, keepdims=True) * (1.0 / B_ROPE) + EPS)
    krn = _rope(krn * jnp.where(nope, 0.0, gkb_ref[...]), rope_b, B_ROPE // 2)
    gk_nope = jnp.where(nope, gkb_ref[...], 0.0)
    ckv = head(CKV_OFF, B_KVLORA)
    emit_fill()
    kv_all = _dot((_rms(ckv) * gckv_ref[...]).astype(BF16), wukv_ref[...])
    for hd in range(B_HEADS):
        kv = kv_all[:, hd * LANES:(hd + 1) * LANES]
        ss = jnp.sum(jnp.where(nope, kv * kv, 0.0), axis=-1, keepdims=True)
        kn = kv * lax.rsqrt(ss * (1.0 / B_NOPE) + EPS) * gk_nope
        kb_ref[0, :, hd * LANES:(hd + 1) * LANES] = (kn + krn).astype(BF16)
        vxb_ref[0, :, hd * LANES:(hd + 1) * LANES] = jnp.where(nope, 1.0, kv).astype(BF16)
        if hd % 2:
            emit_fill()

    def norm_halves(x):
        x2 = x * x
        s_lo = jnp.sum(jnp.where(lo, x2, 0.0), axis=-1, keepdims=True)
        s_hi = jnp.sum(jnp.where(lo, 0.0, x2), axis=-1, keepdims=True)
        return x * jnp.where(lo, lax.rsqrt(s_lo * (1.0 / A_HD) + EPS),
                             lax.rsqrt(s_hi * (1.0 / A_HD) + EPS))

    gq_a = gqa_ref[...] * (A_HD ** -0.5 * LOG2E)
    for off, gain, o_ref in ((QA_OFF, gq_a, qa_ref), (KA_OFF, gka_ref[...], ka_ref)):
        t = head(off, A_HEADS * LANES)
        for hd in range(A_HEADS):
            sl = slice(hd * LANES, (hd + 1) * LANES)
            o_ref[0, :, sl] = _rope(norm_halves(t[:, sl]) * gain, rope_a, A_HD // 2).astype(BF16)
            emit_fill()
    va = head(VA_OFF, A_HEADS * A_VD).astype(BF16)
    ones = jnp.ones((tq, A_VD), BF16)
    for hd in range(A_HEADS):
        vxa_ref[0, :, 2 * hd * A_VD:(2 * hd + 1) * A_VD] = va[:, hd * A_VD:(hd + 1) * A_VD]
        vxa_ref[0, :, (2 * hd + 1) * A_VD:(2 * hd + 2) * A_VD] = ones
    emit_fill(len(fill))


def _inproj(l, x_parts, mods, p, rope_a, rope_b, tq, n_lat, n_ctx):
    b, _, d = x_parts[0].shape
    t = n_lat + n_ctx
    nt = t // tq
    ctx_tile = n_lat // tq
    split = len(x_parts) == 2
    row = lambda i, j: (i, j, 0)
    tab = pl.BlockSpec((tq, LANES), lambda i, j: (j, 0))
    vec = lambda n: _layer_spec(l, (1, n))

    def out(width, dtype):
        return (pl.BlockSpec((1, tq, width), row), jax.ShapeDtypeStruct((b, t, width), dtype))

    outs = [out(N_BRANCH * d, F32), out(CD_W, F32), out(A_HEADS * LANES, BF16),
            out(A_HEADS * LANES, BF16), out(2 * A_HEADS * A_VD, BF16),
            out(B_HEADS * LANES, BF16), out(B_HEADS * LANES, BF16), out(B_HEADS * LANES, BF16)]
    return pl.pallas_call(
        functools.partial(_inproj_kernel, ctx_tile=ctx_tile, split=split),
        grid=(b, nt),
        in_specs=_stream_specs(split, tq, d, ctx_tile) + [
            _mod_spec(l, b, ctx_tile),
            vec(d),
            _layer_spec(l, (d, HEAD_W)),
            _layer_spec(l, (d, LANES)),
            _layer_spec(l, (d, TAIL_W)),
            _layer_spec(l, (B_QLORA, B_HEADS * LANES)),
            _layer_spec(l, (B_KVLORA, B_HEADS * LANES)),
            vec(B_QLORA), vec(B_KVLORA), vec(LANES), vec(LANES), vec(LANES), vec(LANES),
            tab, tab, tab, tab, tab, tab,
        ],
        out_specs=[o[0] for o in outs],
        out_shape=[o[1] for o in outs],
        compiler_params=pltpu.CompilerParams(
            dimension_semantics=("parallel", "parallel"), vmem_limit_bytes=VMEM_LIMIT),
        name="inproj",
    )(*x_parts, mods, p["g_norm1"], p["w_head"], p["w_kr"], p["w_tail"], p["w_uq"], p["w_ukv"],
      p["g_cq"], p["g_ckv"], p["gq_a"], p["gk_a"], p["gq_b"], p["gk_b"], *rope_a, *rope_b)


def _softmax_pv(q, k, vx):
    s = _dot_nt(q, k)
    p = jnp.exp2(s - jnp.max(s, axis=-1, keepdims=True)).astype(BF16)
    return _dot(p, vx)


def _sweep(tile, n_lat, t_all):
    for s0 in range(0, t_all, ATT_SUB):
        tile(s0, 0 if s0 < n_lat else n_lat)


def _attn_a_kernel(q_ref, k_ref, vx_ref, lam_ref, gsub_ref, o_ref, *, n_lat, lam_init):
    t_all = k_ref.shape[1]
    lo = _lane_iota() < A_HD
    la = lam_ref[...].astype(F32)
    lam = (jnp.exp(jnp.sum(la[0:1] * la[1:2], axis=-1, keepdims=True))
           - jnp.exp(jnp.sum(la[2:3] * la[3:4], axis=-1, keepdims=True)) + lam_init)
    post = gsub_ref[...] * (1.0 - lam_init)

    def tile(r0, key_lo):
        q = q_ref[0, r0:r0 + ATT_SUB, :]
        k = k_ref[0, key_lo:t_all, :]
        vx = vx_ref[0, key_lo:t_all, :]
        zero = jnp.zeros_like(q)
        r1 = _softmax_pv(jnp.where(lo, q, zero), k, vx)
        r2 = _softmax_pv(jnp.where(lo, zero, q), k, vx)
        o = (r1[:, 0:A_VD] / r1[:, A_VD:A_VD + 1]
             - lam * (r2[:, 0:A_VD] / r2[:, A_VD:A_VD + 1]))
        o_ref[0, r0:r0 + ATT_SUB, :] = (_rms(o) * post).astype(o_ref.dtype)

    _sweep(tile, n_lat, t_all)


def _attn_a(l, qa, ka, vxa, p, n_lat, lam_init):
    b, t, _ = qa.shape
    head = lambda i, h: (i, 0, h)
    return pl.pallas_call(
        functools.partial(_attn_a_kernel, n_lat=n_lat, lam_init=lam_init),
        grid=(b, A_HEADS),
        in_specs=[
            pl.BlockSpec((1, t, LANES), head),
            pl.BlockSpec((1, t, LANES), head),
            pl.BlockSpec((1, t, 2 * A_VD), head),
            _layer_spec(l, (4, A_HD)),
            _layer_spec(l, (1, LANES)),
        ],
        out_specs=pl.BlockSpec((1, t, LANES), head),
        out_shape=jax.ShapeDtypeStruct((b, t, BRANCH_W), BF16),
        compiler_params=pltpu.CompilerParams(
            dimension_semantics=("parallel", "parallel"), vmem_limit_bytes=VMEM_LIMIT),
        name="attn_a",
    )(qa, ka, vxa, p["lam_a"], p["g_sub_a"])


def _attn_b_kernel(q_ref, k_ref, vx_ref, o_ref, *, n_lat):
    t_all = k_ref.shape[1]
    nope = _lane_iota() < B_NOPE

    def tile(r0, key_lo):
        outs = []
        for hh in range(2):
            sl = slice(hh * LANES, (hh + 1) * LANES)
            r = _softmax_pv(q_ref[0, r0:r0 + ATT_SUB, sl], k_ref[0, key_lo:t_all, sl],
                            vx_ref[0, key_lo:t_all, :])[:, sl]
            outs.append(r / r[:, 0:1])
        o = jnp.where(nope, pltpu.roll(outs[0], B_VD, 1), outs[1])
        o_ref[0, r0:r0 + ATT_SUB, :] = o.astype(o_ref.dtype)

    _sweep(tile, n_lat, t_all)


def _attn_b(qb, kb, vxb, n_lat):
    b, t, _ = qb.shape
    pair = lambda i, h: (i, 0, h)
    spec = pl.BlockSpec((1, t, 2 * LANES), pair)
    return pl.pallas_call(
        functools.partial(_attn_b_kernel, n_lat=n_lat),
        grid=(b, B_HEADS // 2),
        in_specs=[spec, spec, spec],
        out_specs=pl.BlockSpec((1, t, LANES), pair),
        out_shape=jax.ShapeDtypeStruct((b, t, BRANCH_W), BF16),
        compiler_params=pltpu.CompilerParams(
            dimension_semantics=("parallel", "parallel"), vmem_limit_bytes=VMEM_LIMIT),
        name="attn_b",
    )(qb, kb, vxb)


def _mix_cd_kernel(u_ref, pb_ref, pc_ref, pxx_ref, wp_ref, sp_ref, wc_ref,
                   yc_ref, yd_ref, pad_s, *, segments):
    g = pl.program_id(1)
    zeros_halo = jnp.zeros((POOL_HALO, LANES), F32)
    wc = wc_ref[...]

    for start, length in segments:
        rows = lax.broadcasted_iota(jnp.int32, (length, 1), 0)
        u = u_ref[0, start:start + length, :]
        pad_s[0:POOL_HALO, :] = zeros_halo
        pad_s[POOL_HALO:POOL_HALO + length, :] = u
        pad_s[POOL_HALO + length:2 * POOL_HALO + length, :] = zeros_halo

        for gi, w in enumerate(POOL_WINDOWS):
            @pl.when(g == gi)
            def _pool(w=w):
                acc = pad_s[POOL_HALO - w // 2:POOL_HALO - w // 2 + length, :]
                for j in range(1 - w // 2, w // 2):
                    acc = acc + pad_s[POOL_HALO + j:POOL_HALO + j + length, :]
                cnt = (jnp.minimum(rows + w // 2, length) - jnp.maximum(rows - w // 2, 0)).astype(F32)
                dd = acc / cnt - u
                y = _dot(dd.astype(BF16), wp_ref[...]) * sp_ref[...]
                yc_ref[0, start:start + length, :] = y.astype(yc_ref.dtype)

        uu = pc_ref[0, start:start + length, :] * pxx_ref[0, start:start + length, :]
        pad_s[POOL_HALO:POOL_HALO + length, :] = uu
        y = (pad_s[POOL_HALO - 1:POOL_HALO - 1 + length, :] * wc[0:1]
             + uu * wc[1:2]
             + pad_s[POOL_HALO + 1:POOL_HALO + 1 + length, :] * wc[2:3])
        yd_ref[0, start:start + length, :] = (
            pb_ref[0, start:start + length, :] * y).astype(yd_ref.dtype)


def _mix_cd(l, cd, p, n_lat):
    b, t, _ = cd.shape
    segments = ((0, n_lat), (n_lat, t - n_lat))
    n_g = len(POOL_WINDOWS)

    def col(k):
        return pl.BlockSpec((1, t, LANES), lambda i, g: (i, 0, k * n_g + g))

    out_spec = pl.BlockSpec((1, t, LANES), lambda i, g: (i, 0, g))
    return pl.pallas_call(
        functools.partial(_mix_cd_kernel, segments=segments),
        grid=(b, n_g),
        in_specs=[
            col(0), col(1), col(2), col(3),
            pl.BlockSpec((None, None, POOL_GROUP, POOL_GROUP), lambda i, g: (l, g, 0, 0)),
            pl.BlockSpec((None, 1, LANES), lambda i, g: (l, 0, g)),
            pl.BlockSpec((None, 3, LANES), lambda i, g: (l, 0, g)),
        ],
        out_specs=[out_spec, out_spec],
        out_shape=[jax.ShapeDtypeStruct((b, t, BRANCH_W), BF16)] * 2,
        scratch_shapes=[pltpu.VMEM((max(n_lat, t - n_lat) + 2 * POOL_HALO, LANES), F32)],
        compiler_params=pltpu.CompilerParams(
            dimension_semantics=("parallel", "parallel"), vmem_limit_bytes=VMEM_LIMIT),
        name="mix_cd",
    )(cd, cd, cd, cd, p["w_pool"], p["s_pool"], p["w_conv"])


def _merge_ffn_kernel(*refs, ctx_tile, split):
    ya_ref, yb_ref, yc_ref, yd_ref, gate_ref = refs[:5]
    if split:
        x_ref, xc_ref = refs[5:7]
        refs = refs[7:]
        x = jnp.where(pl.program_id(1) == ctx_tile, xc_ref[0], x_ref[0])
    else:
        x = refs[5][0]
        refs = refs[6:]
    mod_ref, wb_ref, wo_ref, g2_ref, w1_ref, w2_ref, o_ref = refs
    d = D_MODEL
    merged = None
    for n, y_ref in enumerate((ya_ref, yb_ref, yc_ref, yd_ref)):
        proj = _dot(y_ref[0], wb_ref[n])
        gate = jax.nn.sigmoid(gate_ref[0, :, n * d:(n + 1) * d])
        merged = gate * proj if merged is None else merged + gate * proj
    mix = _dot(merged.astype(BF16), wo_ref[...])
    x1 = x + mod_ref[:, 2 * d:3 * d] * mix

    y = _rms(x1) * g2_ref[...]
    h = (y * (1.0 + mod_ref[:, 4 * d:5 * d]) + mod_ref[:, 3 * d:4 * d]).astype(BF16)
    f = None
    for c0 in range(0, D_FF, d):
        a = jnp.maximum(_dot(h, w1_ref[:, c0:c0 + d]), 0.0)
        part = _dot((a * a).astype(BF16), w2_ref[c0:c0 + d, :])
        f = part if f is None else f + part
    o_ref[0] = x1 + mod_ref[:, 5 * d:6 * d] * f


def _merge_ffn(l, ys, gates, x_parts, mods, p, tq, n_lat, n_ctx, with_ctx):
    b, _, d = x_parts[0].shape
    t = n_lat + n_ctx
    ctx_tile = n_lat // tq
    split = len(x_parts) == 2
    row = lambda i, j: (i, j, 0)
    y_spec = pl.BlockSpec((1, tq, BRANCH_W), row)
    nq, out_rows = (t // tq, t) if with_ctx else (ctx_tile, n_lat)
    aliases = {5: 0} if (with_ctx and not split) else {}
    return pl.pallas_call(
        functools.partial(_merge_ffn_kernel, ctx_tile=ctx_tile, split=split),
        grid=(b, nq),
        in_specs=[
            y_spec, y_spec, y_spec, y_spec,
            pl.BlockSpec((1, tq, N_BRANCH * d), row),
        ] + _stream_specs(split, tq, d, ctx_tile) + [
            _mod_spec(l, b, ctx_tile),
            _layer_spec(l, (N_BRANCH, BRANCH_W, d)),
            _layer_spec(l, (d, d)),
            _layer_spec(l, (1, d)),
            _layer_spec(l, (d, D_FF)),
            _layer_spec(l, (D_FF, d)),
        ],
        out_specs=pl.BlockSpec((1, tq, d), row),
        out_shape=jax.ShapeDtypeStruct((b, out_rows, d), F32),
        input_output_aliases=aliases,
        compiler_params=pltpu.CompilerParams(
            dimension_semantics=("parallel", "parallel"), vmem_limit_bytes=VMEM_LIMIT),
        name="merge_ffn",
    )(*ys, gates, *x_parts, mods, p["w_branch"], p["w_o"], p["g_norm2"], p["w_ff1"], p["w_ff2"])


def _rope_tables(n_lat, n_ctx, rot_dim, lane_lo, period, total=LANES):
    rows = n_lat // GRID_W
    row = np.repeat(np.arange(rows, dtype=np.float64), GRID_W)
    col = np.tile(np.arange(GRID_W, dtype=np.float64), rows)
    n_freq = rot_dim // 4
    inv = ROPE_BASE ** (-np.arange(n_freq, dtype=np.float64) / n_freq)
    inv = inv.astype(np.float32).astype(np.float64)
    ang = np.concatenate([row[:, None] * inv, col[:, None] * inv], axis=-1)
    ang = ang.astype(np.float32).astype(np.float64)
    half = rot_dim // 2
    t = n_lat + n_ctx
    cos = np.ones((t, total), np.float32)
    s_left = np.zeros((t, total), np.float32)
    s_right = np.zeros((t, total), np.float32)
    starts = [lane_lo] if period == 0 else list(range(lane_lo, total, period))
    for s0 in starts:
        cos[:n_lat, s0:s0 + half] = np.cos(ang)
        cos[:n_lat, s0 + half:s0 + rot_dim] = np.cos(ang)
        s_left[:n_lat, s0:s0 + half] = -np.sin(ang)
        s_right[:n_lat, s0 + half:s0 + rot_dim] = np.sin(ang)
    return jnp.asarray(cos), jnp.asarray(s_left), jnp.asarray(s_right)


def _pad_head_slots(v, width):
    lead = v.shape[:-1]
    v = v.reshape(lead + (-1, width))
    v = jnp.pad(v, [(0, 0)] * len(lead) + [(0, 0), (0, LANES - width)])
    return v.reshape(lead + (-1,))


def kernel(x, c, ctx, c_ctx, w_mod, b_mod, g_norm1, g_norm2, w_in, gq_a, gk_a, lam_a, g_sub_a,
           g_cq, w_uq, g_ckv, w_ukv, gq_b, gk_b, w_pool, s_pool, w_conv, w_branch, w_o,
           w_ff1, w_ff2):
    b, n_lat, d = x.shape
    n_ctx = ctx.shape[1]
    depth = w_mod.shape[0]
    tq = n_ctx
    assert d == D_MODEL and n_lat % ATT_SUB == 0 and n_ctx % ATT_SUB == 0
    assert n_lat % tq == 0 and tq % LANES == 0 and n_lat % GRID_W == 0

    rope_a = _rope_tables(n_lat, n_ctx, A_HD, 0, A_HD)
    rope_b = _rope_tables(n_lat, n_ctx, B_ROPE, B_NOPE, 0)

    mod_rows = -(-(b + 1) // 8) * 8
    cc = jnp.concatenate([c, c_ctx[None, :], jnp.zeros((mod_rows - b - 1, d), F32)], axis=0)
    mods = _modulation(cc, w_mod, b_mod).reshape(depth, mod_rows, 1, N_MOD * d)

    vec = lambda a: a[:, None, :]
    p = {
        "w_head": w_in[:, :, :HEAD_W].astype(BF16),
        "w_kr": jnp.pad(w_in[:, :, HEAD_W:HEAD_W + B_ROPE],
                        ((0, 0), (0, 0), (B_NOPE, LANES - B_NOPE - B_ROPE))).astype(BF16),
        "w_tail": w_in[:, :, HEAD_W + B_ROPE:].astype(BF16),
        "w_uq": _pad_head_slots(w_uq, B_QK).astype(BF16),
        "w_ukv": w_ukv.astype(BF16),
        "w_pool": w_pool.astype(BF16),
        "w_branch": w_branch.astype(BF16),
        "w_o": w_o.astype(BF16),
        "w_ff1": w_ff1.astype(BF16),
        "w_ff2": w_ff2.astype(BF16),
        "g_norm1": vec(g_norm1), "g_norm2": vec(g_norm2),
        "g_cq": vec(g_cq), "g_ckv": vec(g_ckv),
        "gq_a": vec(jnp.tile(gq_a, (1, 2))), "gk_a": vec(jnp.tile(gk_a, (1, 2))),
        "gq_b": vec(jnp.pad(gq_b, ((0, 0), (0, LANES - B_QK)))),
        "gk_b": vec(jnp.pad(gk_b, ((0, 0), (0, LANES - B_QK)))),
        "lam_a": lam_a, "g_sub_a": vec(g_sub_a),
        "s_pool": vec(s_pool), "w_conv": w_conv,
    }

    x_parts = (x, ctx)
    for l in range(depth):
        last = l == depth - 1
        lam_init = 0.8 - 0.6 * math.exp(-0.3 * l)
        gates, cd, qa, ka, vxa, qb, kb, vxb = _inproj(
            l, x_parts, mods, p, rope_a, rope_b, tq, n_lat, n_ctx)
        ya = _attn_a(l, qa, ka, vxa, p, n_lat, lam_init)
        yb = _attn_b(qb, kb, vxb, n_lat)
        yc, yd = _mix_cd(l, cd, p, n_lat)
        x_parts = (_merge_ffn(l, (ya, yb, yc, yd), gates, x_parts, mods, p, tq, n_lat, n_ctx,
                              not last),)
    return x_parts[0]
```

```python
import functools
import math

import numpy as np
import jax
import jax.numpy as jnp
from jax import lax
from jax.experimental import pallas as pl
from jax.experimental.pallas import tpu as pltpu

F32 = jnp.float32
BF16 = jnp.bfloat16

D_MODEL = 1024
GRID_W = 64
ROPE_BASE = 10000.0
EPS = 1e-6
LOG2E = math.log2(math.e)

A_HEADS = 4
A_HD = 64
A_VD = 128
B_HEADS = 8
B_NOPE = 64
B_ROPE = 32
B_QK = B_NOPE + B_ROPE
B_VD = 64
B_QLORA = 384
B_KVLORA = 256
POOL_WINDOWS = (2, 4, 8, 16)
POOL_GROUP = 128
POOL_HALO = 8
BRANCH_W = 512
N_BRANCH = 4
D_FF = 4 * D_MODEL
N_MOD = 6

LANES = 128

QA_OFF = 0
KA_OFF = QA_OFF + 512
VA_OFF = KA_OFF + 512
CQ_OFF = VA_OFF + 512
CKV_OFF = CQ_OFF + B_QLORA
HEAD_W = CKV_OFF + B_KVLORA
CD_OFF = 0
CD_W = 4 * 512
GATE_OFF = CD_OFF + CD_W
TAIL_W = GATE_OFF + N_BRANCH * D_MODEL
IN_FILL = 256

ATT_SUB = 128

VMEM_LIMIT = 56 * 1024 * 1024


def _dot(a, b):
    return jnp.dot(a, b, preferred_element_type=F32)


def _dot_nt(a, b):
    return lax.dot_general(a, b, (((1,), (1,)), ((), ())), preferred_element_type=F32)


def _lane_iota(n=LANES):
    return lax.broadcasted_iota(jnp.int32, (1, n), 1)


def _rms(x):
    return x * lax.rsqrt(jnp.mean(x * x, axis=-1, keepdims=True) + EPS)


def _rope(t, tabs, half):
    cos, s_left, s_right = tabs
    n = t.shape[-1]
    return (t * cos + pltpu.roll(t, n - half, 1) * s_left
            + pltpu.roll(t, half, 1) * s_right)


def _layer_spec(l, shape):
    nd = len(shape)
    return pl.BlockSpec((None,) + tuple(shape), lambda *_: (l,) + (0,) * nd,
                        pipeline_mode=pl.Buffered(1))


def _stream_specs(split, tq, d, ctx_tile):
    if split:
        return [pl.BlockSpec((1, tq, d), lambda i, j: (i, jnp.minimum(j, ctx_tile - 1), 0)),
                pl.BlockSpec((1, tq, d), lambda i, j: (i, 0, 0))]
    return [pl.BlockSpec((1, tq, d), lambda i, j: (i, j, 0))]


def _mod_spec(l, b, ctx_tile):
    return pl.BlockSpec((None, None, 1, N_MOD * D_MODEL),
                        lambda i, j: (l, jnp.where(j == ctx_tile, b, i), 0, 0))


def _mod_kernel(c_ref, w_ref, b_ref, o_ref):
    c = c_ref[...]
    h = (c * jax.nn.sigmoid(c)).astype(BF16)
    o_ref[...] = _dot(h, w_ref[...].astype(BF16)) + b_ref[...]


def _modulation(cc, w_mod, b_mod):
    depth, d, n = w_mod.shape
    rows = cc.shape[0]
    tn = 1536
    return pl.pallas_call(
        _mod_kernel,
        grid=(depth, n // tn),
        in_specs=[
            pl.BlockSpec((rows, d), lambda l, j: (0, 0)),
            pl.BlockSpec((None, d, tn), lambda l, j: (l, 0, j)),
            pl.BlockSpec((None, 1, tn), lambda l, j: (l, 0, j)),
        ],
        out_specs=pl.BlockSpec((None, rows, tn), lambda l, j: (l, 0, j)),
        out_shape=jax.ShapeDtypeStruct((depth, rows, n), F32),
        compiler_params=pltpu.CompilerParams(
            dimension_semantics=("parallel", "parallel"), vmem_limit_bytes=VMEM_LIMIT),
        name="modulation",
    )(cc, w_mod, b_mod.reshape(depth, 1, n))


def _inproj_kernel(*refs, ctx_tile, split):
    if split:
        x_ref, xc_ref = refs[:2]
        refs = refs[2:]
        x = jnp.where(pl.program_id(1) == ctx_tile, xc_ref[0], x_ref[0])
    else:
        x = refs[0][0]
        refs = refs[1:]
    (mod_ref, g_ref, wh_ref, wkr_ref, wt_ref, wuq_ref, wukv_ref, gcq_ref, gckv_ref,
     gqa_ref, gka_ref, gqb_ref, gkb_ref, ca_ref, la_ref, ra_ref, cb_ref, lb_ref, rb_ref,
     gate_ref, cd_ref, qa_ref, ka_ref, vxa_ref, qb_ref, kb_ref, vxb_ref) = refs
    d = D_MODEL
    y = _rms(x) * g_ref[...]
    h = (y * (1.0 + mod_ref[:, d:2 * d]) + mod_ref[:, 0:d]).astype(BF16)
    tq = h.shape[0]

    def head(off, width):
        return _dot(h, wh_ref[:, off:off + width])

    fill = [(gate_ref, GATE_OFF, c0) for c0 in range(0, N_BRANCH * d, IN_FILL)]
    fill += [(cd_ref, CD_OFF, c0) for c0 in range(0, CD_W, IN_FILL)]

    def emit_fill(n=1):
        for _ in range(n):
            if fill:
                o_ref, off, c0 = fill.pop(0)
                o_ref[0, :, c0:c0 + IN_FILL] = _dot(
                    h, wt_ref[:, off + c0:off + c0 + IN_FILL]).astype(o_ref.dtype)

    lane = _lane_iota()
    lo = lane < A_HD
    nope = lane < B_NOPE
    rope_a = (ca_ref[...], la_ref[...], ra_ref[...])
    rope_b = (cb_ref[...], lb_ref[...], rb_ref[...])

    cq = head(CQ_OFF, B_QLORA)
    emit_fill(2)
    q_all = _dot((_rms(cq) * gcq_ref[...]).astype(BF16), wuq_ref[...])
    gq_b = gqb_ref[...] * (B_QK ** -0.5 * LOG2E)
    for hd in range(B_HEADS):
        q = q_all[:, hd * LANES:(hd + 1) * LANES]
        q2 = q * q
        s_n = jnp.sum(jnp.where(nope, q2, 0.0), axis=-1, keepdims=True)
        s_r = jnp.sum(jnp.where(nope, 0.0, q2), axis=-1, keepdims=True)
        inv = jnp.where(nope, lax.rsqrt(s_n * (1.0 / B_NOPE) + EPS),
                        lax.rsqrt(s_r * (1.0 / B_ROPE) + EPS))
        qb_ref[0, :, hd * LANES:(hd + 1) * LANES] = _rope(
            q * inv * gq_b, rope_b, B_ROPE // 2).astype(BF16)
        emit_fill()

    kr = _dot(h, wkr_ref[...])
    krn = kr * lax.rsqrt(jnp.sum(kr * kr, axis=-1, keepdims=True) * (1.0 / B_ROPE) + EPS)
    krn = _rope(krn * jnp.where(nope, 0.0, gkb_ref[...]), rope_b, B_ROPE // 2)
    gk_nope = jnp.where(nope, gkb_ref[...], 0.0)
    ckv = head(CKV_OFF, B_KVLORA)
    emit_fill()
    kv_all = _dot((_rms(ckv) * gckv_ref[...]).astype(BF16), wukv_ref[...])
    for hd in range(B_HEADS):
        kv = kv_all[:, hd * LANES:(hd + 1) * LANES]
        ss = jnp.sum(jnp.where(nope, kv * kv, 0.0), axis=-1, keepdims=True)
        kn = kv * lax.rsqrt(ss * (1.0 / B_NOPE) + EPS) * gk_nope
        kb_ref[0, :, hd * LANES:(hd + 1) * LANES] = (kn + krn).astype(BF16)
        vxb_ref[0, :, hd * LANES:(hd + 1) * LANES] = jnp.where(nope, 1.0, kv).astype(BF16)
        if hd % 2:
            emit_fill()

    def norm_halves(x):
        x2 = x * x
        s_lo = jnp.sum(jnp.where(lo, x2, 0.0), axis=-1, keepdims=True)
        s_hi = jnp.sum(jnp.where(lo, 0.0, x2), axis=-1, keepdims=True)
        return x * jnp.where(lo, lax.rsqrt(s_lo * (1.0 / A_HD) + EPS),
                             lax.rsqrt(s_hi * (1.0 / A_HD) + EPS))

    gq_a = gqa_ref[...] * (A_HD ** -0.5 * LOG2E)
    for off, gain, o_ref in ((QA_OFF, gq_a, qa_ref), (KA_OFF, gka_ref[...], ka_ref)):
        t = head(off, A_HEADS * LANES)
        for hd in range(A_HEADS):
            sl = slice(hd * LANES, (hd + 1) * LANES)
            o_ref[0, :, sl] = _rope(norm_halves(t[:, sl]) * gain, rope_a, A_HD // 2).astype(BF16)
            emit_fill()
    va = head(VA_OFF, A_HEADS * A_VD).astype(BF16)
    ones = jnp.ones((tq, A_VD), BF16)
    for hd in range(A_HEADS):
        vxa_ref[0, :, 2 * hd * A_VD:(2 * hd + 1) * A_VD] = va[:, hd * A_VD:(hd + 1) * A_VD]
        vxa_ref[0, :, (2 * hd + 1) * A_VD:(2 * hd + 2) * A_VD] = ones
    emit_fill(len(fill))


def _inproj(l, x_parts, mods, p, rope_a, rope_b, tq, n_lat, n_ctx):
    b, _, d = x_parts[0].shape
    t = n_lat + n_ctx
    nt = t // tq
    ctx_tile = n_lat // tq
    split = len(x_parts) == 2
    row = lambda i, j: (i, j, 0)
    tab = pl.BlockSpec((tq, LANES), lambda i, j: (j, 0))
    vec = lambda n: _layer_spec(l, (1, n))

    def out(width, dtype):
        return (pl.BlockSpec((1, tq, width), row), jax.ShapeDtypeStruct((b, t, width), dtype))

    outs = [out(N_BRANCH * d, BF16), out(CD_W, BF16), out(A_HEADS * LANES, BF16),
            out(A_HEADS * LANES, BF16), out(2 * A_HEADS * A_VD, BF16),
            out(B_HEADS * LANES, BF16), out(B_HEADS * LANES, BF16), out(B_HEADS * LANES, BF16)]
    return pl.pallas_call(
        functools.partial(_inproj_kernel, ctx_tile=ctx_tile, split=split),
        grid=(b, nt),
        in_specs=_stream_specs(split, tq, d, ctx_tile) + [
            _mod_spec(l, b, ctx_tile),
            vec(d),
            _layer_spec(l, (d, HEAD_W)),
            _layer_spec(l, (d, LANES)),
            _layer_spec(l, (d, TAIL_W)),
            _layer_spec(l, (B_QLORA, B_HEADS * LANES)),
            _layer_spec(l, (B_KVLORA, B_HEADS * LANES)),
            vec(B_QLORA), vec(B_KVLORA), vec(LANES), vec(LANES), vec(LANES), vec(LANES),
            tab, tab, tab, tab, tab, tab,
        ],
        out_specs=[o[0] for o in outs],
        out_shape=[o[1] for o in outs],
        compiler_params=pltpu.CompilerParams(
            dimension_semantics=("parallel", "parallel"), vmem_limit_bytes=VMEM_LIMIT),
        name="inproj",
    )(*x_parts, mods, p["g_norm1"], p["w_head"], p["w_kr"], p["w_tail"], p["w_uq"], p["w_ukv"],
      p["g_cq"], p["g_ckv"], p["gq_a"], p["gk_a"], p["gq_b"], p["gk_b"], *rope_a, *rope_b)


def _softmax_pv(q, k, vx):
    s = _dot_nt(q, k)
    p = jnp.exp2(s - jnp.max(s, axis=-1, keepdims=True)).astype(BF16)
    return _dot(p, vx)


def _sweep(tile, n_lat, t_all):
    for s0 in range(0, t_all, ATT_SUB):
        tile(s0, 0 if s0 < n_lat else n_lat)


def _attn_a_kernel(q_ref, k_ref, vx_ref, lam_ref, gsub_ref, o_ref, *, n_lat, lam_init):
    t_all = k_ref.shape[1]
    lo = _lane_iota() < A_HD
    la = lam_ref[...].astype(F32)
    lam = (jnp.exp(jnp.sum(la[0:1] * la[1:2], axis=-1, keepdims=True))
           - jnp.exp(jnp.sum(la[2:3] * la[3:4], axis=-1, keepdims=True)) + lam_init)
    post = gsub_ref[...] * (1.0 - lam_init)

    def tile(r0, key_lo):
        q = q_ref[0, r0:r0 + ATT_SUB, :]
        k = k_ref[0, key_lo:t_all, :]
        vx = vx_ref[0, key_lo:t_all, :]
        zero = jnp.zeros_like(q)
        r1 = _softmax_pv(jnp.where(lo, q, zero), k, vx)
        r2 = _softmax_pv(jnp.where(lo, zero, q), k, vx)
        o = (r1[:, 0:A_VD] / r1[:, A_VD:A_VD + 1]
             - lam * (r2[:, 0:A_VD] / r2[:, A_VD:A_VD + 1]))
        o_ref[0, r0:r0 + ATT_SUB, :] = (_rms(o) * post).astype(o_ref.dtype)

    _sweep(tile, n_lat, t_all)


def _attn_a(l, qa, ka, vxa, p, n_lat, lam_init):
    b, t, _ = qa.shape
    head = lambda i, h: (i, 0, h)
    return pl.pallas_call(
        functools.partial(_attn_a_kernel, n_lat=n_lat, lam_init=lam_init),
        grid=(b, A_HEADS),
        in_specs=[
            pl.BlockSpec((1, t, LANES), head),
            pl.BlockSpec((1, t, LANES), head),
            pl.BlockSpec((1, t, 2 * A_VD), head),
            _layer_spec(l, (4, A_HD)),
            _layer_spec(l, (1, LANES)),
        ],
        out_specs=pl.BlockSpec((1, t, LANES), head),
        out_shape=jax.ShapeDtypeStruct((b, t, BRANCH_W), BF16),
        compiler_params=pltpu.CompilerParams(
            dimension_semantics=("parallel", "parallel"), vmem_limit_bytes=VMEM_LIMIT),
        name="attn_a",
    )(qa, ka, vxa, p["lam_a"], p["g_sub_a"])


def _attn_b_kernel(q_ref, k_ref, vx_ref, o_ref, *, n_lat):
    t_all = k_ref.shape[1]
    nope = _lane_iota() < B_NOPE

    def tile(r0, key_lo):
        outs = []
        for hh in range(2):
            sl = slice(hh * LANES, (hh + 1) * LANES)
            r = _softmax_pv(q_ref[0, r0:r0 + ATT_SUB, sl], k_ref[0, key_lo:t_all, sl],
                            vx_ref[0, key_lo:t_all, :])[:, sl]
            outs.append(r / r[:, 0:1])
        o = jnp.where(nope, pltpu.roll(outs[0], B_VD, 1), outs[1])
        o_ref[0, r0:r0 + ATT_SUB, :] = o.astype(o_ref.dtype)

    _sweep(tile, n_lat, t_all)


def _attn_b(qb, kb, vxb, n_lat):
    b, t, _ = qb.shape
    pair = lambda i, h: (i, 0, h)
    spec = pl.BlockSpec((1, t, 2 * LANES), pair)
    return pl.pallas_call(
        functools.partial(_attn_b_kernel, n_lat=n_lat),
        grid=(b, B_HEADS // 2),
        in_specs=[spec, spec, spec],
        out_specs=pl.BlockSpec((1, t, LANES), pair),
        out_shape=jax.ShapeDtypeStruct((b, t, BRANCH_W), BF16),
        compiler_params=pltpu.CompilerParams(
            dimension_semantics=("parallel", "parallel"), vmem_limit_bytes=VMEM_LIMIT),
        name="attn_b",
    )(qb, kb, vxb)


def _mix_cd_kernel(u_ref, pb_ref, pc_ref, pxx_ref, wp_ref, sp_ref, wc_ref,
                   yc_ref, yd_ref, pad_s, *, segments):
    g = pl.program_id(1)
    zeros_halo = jnp.zeros((POOL_HALO, LANES), F32)
    wc = wc_ref[...]

    for start, length in segments:
        rows = lax.broadcasted_iota(jnp.int32, (length, 1), 0)
        u = u_ref[0, start:start + length, :].astype(F32)
        pad_s[0:POOL_HALO, :] = zeros_halo
        pad_s[POOL_HALO:POOL_HALO + length, :] = u
        pad_s[POOL_HALO + length:2 * POOL_HALO + length, :] = zeros_halo

        for gi, w in enumerate(POOL_WINDOWS):
            @pl.when(g == gi)
            def _pool(w=w):
                acc = pad_s[POOL_HALO - w // 2:POOL_HALO - w // 2 + length, :]
                for j in range(1 - w // 2, w // 2):
                    acc = acc + pad_s[POOL_HALO + j:POOL_HALO + j + length, :]
                cnt = (jnp.minimum(rows + w // 2, length) - jnp.maximum(rows - w // 2, 0)).astype(F32)
                dd = acc / cnt - u
                y = _dot(dd.astype(BF16), wp_ref[...]) * sp_ref[...]
                yc_ref[0, start:start + length, :] = y.astype(yc_ref.dtype)

        uu = (pc_ref[0, start:start + length, :].astype(F32)
              * pxx_ref[0, start:start + length, :].astype(F32))
        pad_s[POOL_HALO:POOL_HALO + length, :] = uu
        y = (pad_s[POOL_HALO - 1:POOL_HALO - 1 + length, :] * wc[0:1]
             + uu * wc[1:2]
             + pad_s[POOL_HALO + 1:POOL_HALO + 1 + length, :] * wc[2:3])
        yd_ref[0, start:start + length, :] = (
            pb_ref[0, start:start + length, :].astype(F32) * y).astype(yd_ref.dtype)


def _mix_cd(l, cd, p, n_lat):
    b, t, _ = cd.shape
    segments = ((0, n_lat), (n_lat, t - n_lat))
    n_g = len(POOL_WINDOWS)

    def col(k):
        return pl.BlockSpec((1, t, LANES), lambda i, g: (i, 0, k * n_g + g))

    out_spec = pl.BlockSpec((1, t, LANES), lambda i, g: (i, 0, g))
    return pl.pallas_call(
        functools.partial(_mix_cd_kernel, segments=segments),
        grid=(b, n_g),
        in_specs=[
            col(0), col(1), col(2), col(3),
            pl.BlockSpec((None, None, POOL_GROUP, POOL_GROUP), lambda i, g: (l, g, 0, 0)),
            pl.BlockSpec((None, 1, LANES), lambda i, g: (l, 0, g)),
            pl.BlockSpec((None, 3, LANES), lambda i, g: (l, 0, g)),
        ],
        out_specs=[out_spec, out_spec],
        out_shape=[jax.ShapeDtypeStruct((b, t, BRANCH_W), BF16)] * 2,
        scratch_shapes=[pltpu.VMEM((max(n_lat, t - n_lat) + 2 * POOL_HALO, LANES), F32)],
        compiler_params=pltpu.CompilerParams(
            dimension_semantics=("parallel", "parallel"), vmem_limit_bytes=VMEM_LIMIT),
        name="mix_cd",
    )(cd, cd, cd, cd, p["w_pool"], p["s_pool"], p["w_conv"])


def _merge_ffn_kernel(*refs, ctx_tile, split):
    ya_ref, yb_ref, yc_ref, yd_ref, gate_ref = refs[:5]
    if split:
        x_ref, xc_ref = refs[5:7]
        refs = refs[7:]
        x = jnp.where(pl.program_id(1) == ctx_tile, xc_ref[0], x_ref[0])
    else:
        x = refs[5][0]
        refs = refs[6:]
    mod_ref, wb_ref, wo_ref, g2_ref, w1_ref, w2_ref, o_ref = refs
    d = D_MODEL
    merged = None
    for n, y_ref in enumerate((ya_ref, yb_ref, yc_ref, yd_ref)):
        proj = _dot(y_ref[0], wb_ref[n])
        gate = jax.nn.sigmoid(gate_ref[0, :, n * d:(n + 1) * d].astype(F32))
        merged = gate * proj if merged is None else merged + gate * proj
    mix = _dot(merged.astype(BF16), wo_ref[...])
    x1 = x + mod_ref[:, 2 * d:3 * d] * mix

    y = _rms(x1) * g2_ref[...]
    h = (y * (1.0 + mod_ref[:, 4 * d:5 * d]) + mod_ref[:, 3 * d:4 * d]).astype(BF16)
    f = None
    for c0 in range(0, D_FF, d):
        a = jnp.maximum(_dot(h, w1_ref[:, c0:c0 + d]), 0.0)
        part = _dot((a * a).astype(BF16), w2_ref[c0:c0 + d, :])
        f = part if f is None else f + part
    o_ref[0] = x1 + mod_ref[:, 5 * d:6 * d] * f


def _merge_ffn(l, ys, gates, x_parts, mods, p, tq, n_lat, n_ctx, with_ctx):
    b, _, d = x_parts[0].shape
    t = n_lat + n_ctx
    ctx_tile = n_lat // tq
    split = len(x_parts) == 2
    row = lambda i, j: (i, j, 0)
    y_spec = pl.BlockSpec((1, tq, BRANCH_W), row)
    nq, out_rows = (t // tq, t) if with_ctx else (ctx_tile, n_lat)
    aliases = {5: 0} if (with_ctx and not split) else {}
    return pl.pallas_call(
        functools.partial(_merge_ffn_kernel, ctx_tile=ctx_tile, split=split),
        grid=(b, nq),
        in_specs=[
            y_spec, y_spec, y_spec, y_spec,
            pl.BlockSpec((1, tq, N_BRANCH * d), row),
        ] + _stream_specs(split, tq, d, ctx_tile) + [
            _mod_spec(l, b, ctx_tile),
            _layer_spec(l, (N_BRANCH, BRANCH_W, d)),
            _layer_spec(l, (d, d)),
            _layer_spec(l, (1, d)),
            _layer_spec(l, (d, D_FF)),
            _layer_spec(l, (D_FF, d)),
        ],
        out_specs=pl.BlockSpec((1, tq, d), row),
        out_shape=jax.ShapeDtypeStruct((b, out_rows, d), F32),
        input_output_aliases=aliases,
        compiler_params=pltpu.CompilerParams(
            dimension_semantics=("parallel", "parallel"), vmem_limit_bytes=VMEM_LIMIT),
        name="merge_ffn",
    )(*ys, gates, *x_parts, mods, p["w_branch"], p["w_o"], p["g_norm2"], p["w_ff1"], p["w_ff2"])


def _rope_tables(n_lat, n_ctx, rot_dim, lane_lo, period, total=LANES):
    rows = n_lat // GRID_W
    row = np.repeat(np.arange(rows, dtype=np.float64), GRID_W)
    col = np.tile(np.arange(GRID_W, dtype=np.float64), rows)
    n_freq = rot_dim // 4
    inv = ROPE_BASE ** (-np.arange(n_freq, dtype=np.float64) / n_freq)
    inv = inv.astype(np.float32).astype(np.float64)
    ang = np.concatenate([row[:, None] * inv, col[:, None] * inv], axis=-1)
    ang = ang.astype(np.float32).astype(np.float64)
    half = rot_dim // 2
    t = n_lat + n_ctx
    cos = np.ones((t, total), np.float32)
    s_left = np.zeros((t, total), np.float32)
    s_right = np.zeros((t, total), np.float32)
    starts = [lane_lo] if period == 0 else list(range(lane_lo, total, period))
    for s0 in starts:
        cos[:n_lat, s0:s0 + half] = np.cos(ang)
        cos[:n_lat, s0 + half:s0 + rot_dim] = np.cos(ang)
        s_left[:n_lat, s0:s0 + half] = -np.sin(ang)
        s_right[:n_lat, s0 + half:s0 + rot_dim] = np.sin(ang)
    return jnp.asarray(cos), jnp.asarray(s_left), jnp.asarray(s_right)


def _pad_head_slots(v, width):
    lead = v.shape[:-1]
    v = v.reshape(lead + (-1, width))
    v = jnp.pad(v, [(0, 0)] * len(lead) + [(0, 0), (0, LANES - width)])
    return v.reshape(lead + (-1,))


def kernel(x, c, ctx, c_ctx, w_mod, b_mod, g_norm1, g_norm2, w_in, gq_a, gk_a, lam_a, g_sub_a,
           g_cq, w_uq, g_ckv, w_ukv, gq_b, gk_b, w_pool, s_pool, w_conv, w_branch, w_o,
           w_ff1, w_ff2):
    b, n_lat, d = x.shape
    n_ctx = ctx.shape[1]
    depth = w_mod.shape[0]
    tq = n_ctx
    assert d == D_MODEL and n_lat % ATT_SUB == 0 and n_ctx % ATT_SUB == 0
    assert n_lat % tq == 0 and tq % LANES == 0 and n_lat % GRID_W == 0

    rope_a = _rope_tables(n_lat, n_ctx, A_HD, 0, A_HD)
    rope_b = _rope_tables(n_lat, n_ctx, B_ROPE, B_NOPE, 0)

    mod_rows = -(-(b + 1) // 8) * 8
    cc = jnp.concatenate([c, c_ctx[None, :], jnp.zeros((mod_rows - b - 1, d), F32)], axis=0)
    mods = _modulation(cc, w_mod, b_mod).reshape(depth, mod_rows, 1, N_MOD * d)

    vec = lambda a: a[:, None, :]
    p = {
        "w_head": w_in[:, :, :HEAD_W].astype(BF16),
        "w_kr": jnp.pad(w_in[:, :, HEAD_W:HEAD_W + B_ROPE],
                        ((0, 0), (0, 0), (B_NOPE, LANES - B_NOPE - B_ROPE))).astype(BF16),
        "w_tail": w_in[:, :, HEAD_W + B_ROPE:].astype(BF16),
        "w_uq": _pad_head_slots(w_uq, B_QK).astype(BF16),
        "w_ukv": w_ukv.astype(BF16),
        "w_pool": w_pool.astype(BF16),
        "w_branch": w_branch.astype(BF16),
        "w_o": w_o.astype(BF16),
        "w_ff1": w_ff1.astype(BF16),
        "w_ff2": w_ff2.astype(BF16),
        "g_norm1": vec(g_norm1), "g_norm2": vec(g_norm2),
        "g_cq": vec(g_cq), "g_ckv": vec(g_ckv),
        "gq_a": vec(jnp.tile(gq_a, (1, 2))), "gk_a": vec(jnp.tile(gk_a, (1, 2))),
        "gq_b": vec(jnp.pad(gq_b, ((0, 0), (0, LANES - B_QK)))),
        "gk_b": vec(jnp.pad(gk_b, ((0, 0), (0, LANES - B_QK)))),
        "lam_a": lam_a, "g_sub_a": vec(g_sub_a),
        "s_pool": vec(s_pool), "w_conv": w_conv,
    }

    x_parts = (x, ctx)
    for l in range(depth):
        last = l == depth - 1
        lam_init = 0.8 - 0.6 * math.exp(-0.3 * l)
        gates, cd, qa, ka, vxa, qb, kb, vxb = _inproj(
            l, x_parts, mods, p, rope_a, rope_b, tq, n_lat, n_ctx)
        ya = _attn_a(l, qa, ka, vxa, p, n_lat, lam_init)
        yb = _attn_b(qb, kb, vxb, n_lat)
        yc, yd = _mix_cd(l, cd, p, n_lat)
        x_parts = (_merge_ffn(l, (ya, yb, yc, yd), gates, x_parts, mods, p, tq, n_lat, n_ctx,
                              not last),)
    return x_parts[0]
```

```python
import functools
import math

import numpy as np
import jax
import jax.numpy as jnp
from jax import lax
from jax.experimental import pallas as pl
from jax.experimental.pallas import tpu as pltpu

F32 = jnp.float32
BF16 = jnp.bfloat16

D_MODEL = 1024
GRID_W = 64
ROPE_BASE = 10000.0
EPS = 1e-6
LOG2E = math.log2(math.e)

A_HEADS = 4
A_HD = 64
A_VD = 128
B_HEADS = 8
B_NOPE = 64
B_ROPE = 32
B_QK = B_NOPE + B_ROPE
B_VD = 64
B_QLORA = 384
B_KVLORA = 256
POOL_WINDOWS = (2, 4, 8, 16)
POOL_GROUP = 128
POOL_HALO = 8
BRANCH_W = 512
N_BRANCH = 4
D_FF = 4 * D_MODEL
N_MOD = 6

LANES = 128

QA_OFF = 0
KA_OFF = QA_OFF + 512
VA_OFF = KA_OFF + 512
CQ_OFF = VA_OFF + 512
CKV_OFF = CQ_OFF + B_QLORA
HEAD_W = CKV_OFF + B_KVLORA
CD_OFF = 0
CD_W = 4 * 512
GATE_OFF = CD_OFF + CD_W
TAIL_W = GATE_OFF + N_BRANCH * D_MODEL
IN_FILL = 256

ATT_SUB = 128
SAFE_LOG2 = 40.0

VMEM_LIMIT = 56 * 1024 * 1024


def _dot(a, b):
    return jnp.dot(a, b, preferred_element_type=F32)


def _dot_nt(a, b):
    return lax.dot_general(a, b, (((1,), (1,)), ((), ())), preferred_element_type=F32)


def _lane_iota(n=LANES):
    return lax.broadcasted_iota(jnp.int32, (1, n), 1)


def _rms(x):
    return x * lax.rsqrt(jnp.mean(x * x, axis=-1, keepdims=True) + EPS)


def _rope(t, tabs, half):
    cos, s_left, s_right = tabs
    n = t.shape[-1]
    return (t * cos + pltpu.roll(t, n - half, 1) * s_left
            + pltpu.roll(t, half, 1) * s_right)


def _layer_spec(l, shape):
    nd = len(shape)
    return pl.BlockSpec((None,) + tuple(shape), lambda *_: (l,) + (0,) * nd,
                        pipeline_mode=pl.Buffered(1))


def _stream_specs(split, tq, d, ctx_tile):
    if split:
        return [pl.BlockSpec((1, tq, d), lambda i, j: (i, jnp.minimum(j, ctx_tile - 1), 0)),
                pl.BlockSpec((1, tq, d), lambda i, j: (i, 0, 0))]
    return [pl.BlockSpec((1, tq, d), lambda i, j: (i, j, 0))]


def _mod_spec(l, b, ctx_tile):
    return pl.BlockSpec((None, None, 1, N_MOD * D_MODEL),
                        lambda i, j: (l, jnp.where(j == ctx_tile, b, i), 0, 0))


def _mod_kernel(c_ref, w_ref, b_ref, o_ref):
    c = c_ref[...]
    h = (c * jax.nn.sigmoid(c)).astype(BF16)
    o_ref[...] = _dot(h, w_ref[...].astype(BF16)) + b_ref[...]


def _modulation(cc, w_mod, b_mod):
    depth, d, n = w_mod.shape
    rows = cc.shape[0]
    tn = 1536
    return pl.pallas_call(
        _mod_kernel,
        grid=(depth, n // tn),
        in_specs=[
            pl.BlockSpec((rows, d), lambda l, j: (0, 0)),
            pl.BlockSpec((None, d, tn), lambda l, j: (l, 0, j)),
            pl.BlockSpec((None, 1, tn), lambda l, j: (l, 0, j)),
        ],
        out_specs=pl.BlockSpec((None, rows, tn), lambda l, j: (l, 0, j)),
        out_shape=jax.ShapeDtypeStruct((depth, rows, n), F32),
        compiler_params=pltpu.CompilerParams(
            dimension_semantics=("parallel", "parallel"), vmem_limit_bytes=VMEM_LIMIT),
        name="modulation",
    )(cc, w_mod, b_mod.reshape(depth, 1, n))


def _inproj_kernel(*refs, ctx_tile, split):
    if split:
        x_ref, xc_ref = refs[:2]
        refs = refs[2:]
        x = jnp.where(pl.program_id(1) == ctx_tile, xc_ref[0], x_ref[0])
    else:
        x = refs[0][0]
        refs = refs[1:]
    (mod_ref, g_ref, wh_ref, wkr_ref, wt_ref, wuq_ref, wukv_ref, gcq_ref, gckv_ref,
     gqa_ref, gka_ref, gqb_ref, gkb_ref, ca_ref, la_ref, ra_ref, cb_ref, lb_ref, rb_ref,
     gate_ref, cd_ref, qa_ref, ka_ref, vxa_ref, qb_ref, kb_ref, vxb_ref) = refs
    d = D_MODEL
    y = _rms(x) * g_ref[...]
    h = (y * (1.0 + mod_ref[:, d:2 * d]) + mod_ref[:, 0:d]).astype(BF16)
    tq = h.shape[0]

    def head(off, width):
        return _dot(h, wh_ref[:, off:off + width])

    fill = [(gate_ref, GATE_OFF, c0) for c0 in range(0, N_BRANCH * d, IN_FILL)]
    fill += [(cd_ref, CD_OFF, c0) for c0 in range(0, CD_W, IN_FILL)]

    def emit_fill(n=1):
        for _ in range(n):
            if fill:
                o_ref, off, c0 = fill.pop(0)
                o_ref[0, :, c0:c0 + IN_FILL] = _dot(
                    h, wt_ref[:, off + c0:off + c0 + IN_FILL]).astype(o_ref.dtype)

    lane = _lane_iota()
    lo = lane < A_HD
    nope = lane < B_NOPE
    rope_a = (ca_ref[...], la_ref[...], ra_ref[...])
    rope_b = (cb_ref[...], lb_ref[...], rb_ref[...])

    cq = head(CQ_OFF, B_QLORA)
    emit_fill(2)
    q_all = _dot((_rms(cq) * gcq_ref[...]).astype(BF16), wuq_ref[...])
    gq_b = gqb_ref[...] * (B_QK ** -0.5 * LOG2E)
    for hd in range(B_HEADS):
        q = q_all[:, hd * LANES:(hd + 1) * LANES]
        q2 = q * q
        s_n = jnp.sum(jnp.where(nope, q2, 0.0), axis=-1, keepdims=True)
        s_r = jnp.sum(jnp.where(nope, 0.0, q2), axis=-1, keepdims=True)
        inv = jnp.where(nope, lax.rsqrt(s_n * (1.0 / B_NOPE) + EPS),
                        lax.rsqrt(s_r * (1.0 / B_ROPE) + EPS))
        qb_ref[0, :, hd * LANES:(hd + 1) * LANES] = _rope(
            q * inv * gq_b, rope_b, B_ROPE // 2).astype(BF16)
        emit_fill()

    kr = _dot(h, wkr_ref[...])
    krn = kr * lax.rsqrt(jnp.sum(kr * kr, axis=-1, keepdims=True) * (1.0 / B_ROPE) + EPS)
    krn = _rope(krn * jnp.where(nope, 0.0, gkb_ref[...]), rope_b, B_ROPE // 2)
    gk_nope = jnp.where(nope, gkb_ref[...], 0.0)
    ckv = head(CKV_OFF, B_KVLORA)
    emit_fill()
    kv_all = _dot((_rms(ckv) * gckv_ref[...]).astype(BF16), wukv_ref[...])
    for hd in range(B_HEADS):
        kv = kv_all[:, hd * LANES:(hd + 1) * LANES]
        ss = jnp.sum(jnp.where(nope, kv * kv, 0.0), axis=-1, keepdims=True)
        kn = kv * lax.rsqrt(ss * (1.0 / B_NOPE) + EPS) * gk_nope
        kb_ref[0, :, hd * LANES:(hd + 1) * LANES] = (kn + krn).astype(BF16)
        vxb_ref[0, :, hd * LANES:(hd + 1) * LANES] = jnp.where(nope, 1.0, kv).astype(BF16)
        if hd % 2:
            emit_fill()

    def norm_halves(x):
        x2 = x * x
        s_lo = jnp.sum(jnp.where(lo, x2, 0.0), axis=-1, keepdims=True)
        s_hi = jnp.sum(jnp.where(lo, 0.0, x2), axis=-1, keepdims=True)
        return x * jnp.where(lo, lax.rsqrt(s_lo * (1.0 / A_HD) + EPS),
                             lax.rsqrt(s_hi * (1.0 / A_HD) + EPS))

    gq_a = gqa_ref[...] * (A_HD ** -0.5 * LOG2E)
    for off, gain, o_ref in ((QA_OFF, gq_a, qa_ref), (KA_OFF, gka_ref[...], ka_ref)):
        t = head(off, A_HEADS * LANES)
        for hd in range(A_HEADS):
            sl = slice(hd * LANES, (hd + 1) * LANES)
            o_ref[0, :, sl] = _rope(norm_halves(t[:, sl]) * gain, rope_a, A_HD // 2).astype(BF16)
            emit_fill()
    va = head(VA_OFF, A_HEADS * A_VD).astype(BF16)
    ones = jnp.ones((tq, A_VD), BF16)
    for hd in range(A_HEADS):
        vxa_ref[0, :, 2 * hd * A_VD:(2 * hd + 1) * A_VD] = va[:, hd * A_VD:(hd + 1) * A_VD]
        vxa_ref[0, :, (2 * hd + 1) * A_VD:(2 * hd + 2) * A_VD] = ones
    emit_fill(len(fill))


def _inproj(l, x_parts, mods, p, rope_a, rope_b, tq, n_lat, n_ctx):
    b, _, d = x_parts[0].shape
    t = n_lat + n_ctx
    nt = t // tq
    ctx_tile = n_lat // tq
    split = len(x_parts) == 2
    row = lambda i, j: (i, j, 0)
    tab = pl.BlockSpec((tq, LANES), lambda i, j: (j, 0))
    vec = lambda n: _layer_spec(l, (1, n))

    def out(width, dtype):
        return (pl.BlockSpec((1, tq, width), row), jax.ShapeDtypeStruct((b, t, width), dtype))

    outs = [out(N_BRANCH * d, BF16), out(CD_W, BF16), out(A_HEADS * LANES, BF16),
            out(A_HEADS * LANES, BF16), out(2 * A_HEADS * A_VD, BF16),
            out(B_HEADS * LANES, BF16), out(B_HEADS * LANES, BF16), out(B_HEADS * LANES, BF16)]
    return pl.pallas_call(
        functools.partial(_inproj_kernel, ctx_tile=ctx_tile, split=split),
        grid=(b, nt),
        in_specs=_stream_specs(split, tq, d, ctx_tile) + [
            _mod_spec(l, b, ctx_tile),
            vec(d),
            _layer_spec(l, (d, HEAD_W)),
            _layer_spec(l, (d, LANES)),
            _layer_spec(l, (d, TAIL_W)),
            _layer_spec(l, (B_QLORA, B_HEADS * LANES)),
            _layer_spec(l, (B_KVLORA, B_HEADS * LANES)),
            vec(B_QLORA), vec(B_KVLORA), vec(LANES), vec(LANES), vec(LANES), vec(LANES),
            tab, tab, tab, tab, tab, tab,
        ],
        out_specs=[o[0] for o in outs],
        out_shape=[o[1] for o in outs],
        compiler_params=pltpu.CompilerParams(
            dimension_semantics=("parallel", "parallel"), vmem_limit_bytes=VMEM_LIMIT),
        name="inproj",
    )(*x_parts, mods, p["g_norm1"], p["w_head"], p["w_kr"], p["w_tail"], p["w_uq"], p["w_ukv"],
      p["g_cq"], p["g_ckv"], p["gq_a"], p["gk_a"], p["gq_b"], p["gk_b"], *rope_a, *rope_b)


def _softmax_pv(q, k, vx, shift_max):
    s = _dot_nt(q, k)
    if shift_max:
        s = s - jnp.max(s, axis=-1, keepdims=True)
    return _dot(jnp.exp2(s).astype(BF16), vx)


def _sweep(make_tile, bounded, n_lat, t_all):
    def run(shift_max):
        tile = make_tile(shift_max)
        for s0 in range(0, t_all, ATT_SUB):
            tile(s0, 0 if s0 < n_lat else n_lat)

    pl.when(bounded)(lambda: run(False))
    pl.when(jnp.logical_not(bounded))(lambda: run(True))


def _attn_a_kernel(safe_ref, q_ref, k_ref, vx_ref, lam_ref, gsub_ref, o_ref, *, l, n_lat, lam_init):
    t_all = k_ref.shape[1]
    lo = _lane_iota() < A_HD
    la = lam_ref[...].astype(F32)
    lam = (jnp.exp(jnp.sum(la[0:1] * la[1:2], axis=-1, keepdims=True))
           - jnp.exp(jnp.sum(la[2:3] * la[3:4], axis=-1, keepdims=True)) + lam_init)
    post = gsub_ref[...] * (1.0 - lam_init)

    def make_tile(shift_max):
        def tile(r0, key_lo):
            q = q_ref[0, r0:r0 + ATT_SUB, :]
            k = k_ref[0, key_lo:t_all, :]
            vx = vx_ref[0, key_lo:t_all, :]
            zero = jnp.zeros_like(q)
            r1 = _softmax_pv(jnp.where(lo, q, zero), k, vx, shift_max)
            r2 = _softmax_pv(jnp.where(lo, zero, q), k, vx, shift_max)
            o = (r1[:, 0:A_VD] / r1[:, A_VD:A_VD + 1]
                 - lam * (r2[:, 0:A_VD] / r2[:, A_VD:A_VD + 1]))
            o_ref[0, r0:r0 + ATT_SUB, :] = (_rms(o) * post).astype(o_ref.dtype)
        return tile

    _sweep(make_tile, safe_ref[l] != 0, n_lat, t_all)


def _attn_a(l, qa, ka, vxa, p, n_lat, lam_init):
    b, t, _ = qa.shape
    head = lambda i, h: (i, 0, h)
    return pl.pallas_call(
        functools.partial(_attn_a_kernel, l=l, n_lat=n_lat, lam_init=lam_init),
        grid=(b, A_HEADS),
        in_specs=[
            pl.BlockSpec(memory_space=pltpu.SMEM),
            pl.BlockSpec((1, t, LANES), head),
            pl.BlockSpec((1, t, LANES), head),
            pl.BlockSpec((1, t, 2 * A_VD), head),
            _layer_spec(l, (4, A_HD)),
            _layer_spec(l, (1, LANES)),
        ],
        out_specs=pl.BlockSpec((1, t, LANES), head),
        out_shape=jax.ShapeDtypeStruct((b, t, BRANCH_W), BF16),
        compiler_params=pltpu.CompilerParams(
            dimension_semantics=("parallel", "parallel"), vmem_limit_bytes=VMEM_LIMIT),
        name="attn_a",
    )(p["safe_a"], qa, ka, vxa, p["lam_a"], p["g_sub_a"])


def _attn_b_kernel(safe_ref, q_ref, k_ref, vx_ref, o_ref, *, l, n_lat):
    t_all = k_ref.shape[1]
    nope = _lane_iota() < B_NOPE

    def make_tile(shift_max):
        def tile(r0, key_lo):
            outs = []
            for hh in range(2):
                sl = slice(hh * LANES, (hh + 1) * LANES)
                r = _softmax_pv(q_ref[0, r0:r0 + ATT_SUB, sl], k_ref[0, key_lo:t_all, sl],
                                vx_ref[0, key_lo:t_all, :], shift_max)[:, sl]
                outs.append(r / r[:, 0:1])
            o = jnp.where(nope, pltpu.roll(outs[0], B_VD, 1), outs[1])
            o_ref[0, r0:r0 + ATT_SUB, :] = o.astype(o_ref.dtype)
        return tile

    _sweep(make_tile, safe_ref[l] != 0, n_lat, t_all)


def _attn_b(l, qb, kb, vxb, p, n_lat):
    b, t, _ = qb.shape
    pair = lambda i, h: (i, 0, h)
    spec = pl.BlockSpec((1, t, 2 * LANES), pair)
    return pl.pallas_call(
        functools.partial(_attn_b_kernel, l=l, n_lat=n_lat),
        grid=(b, B_HEADS // 2),
        in_specs=[pl.BlockSpec(memory_space=pltpu.SMEM), spec, spec, spec],
        out_specs=pl.BlockSpec((1, t, LANES), pair),
        out_shape=jax.ShapeDtypeStruct((b, t, BRANCH_W), BF16),
        compiler_params=pltpu.CompilerParams(
            dimension_semantics=("parallel", "parallel"), vmem_limit_bytes=VMEM_LIMIT),
        name="attn_b",
    )(p["safe_b"], qb, kb, vxb)


def _mix_cd_kernel(u_ref, pb_ref, pc_ref, pxx_ref, wp_ref, sp_ref, wc_ref,
                   yc_ref, yd_ref, pad_s, *, segments):
    g = pl.program_id(1)
    zeros_halo = jnp.zeros((POOL_HALO, LANES), F32)
    wc = wc_ref[...]

    for start, length in segments:
        rows = lax.broadcasted_iota(jnp.int32, (length, 1), 0)
        u = u_ref[0, start:start + length, :].astype(F32)
        pad_s[0:POOL_HALO, :] = zeros_halo
        pad_s[POOL_HALO:POOL_HALO + length, :] = u
        pad_s[POOL_HALO + length:2 * POOL_HALO + length, :] = zeros_halo

        for gi, w in enumerate(POOL_WINDOWS):
            @pl.when(g == gi)
            def _pool(w=w):
                acc = pad_s[POOL_HALO - w // 2:POOL_HALO - w // 2 + length, :]
                for j in range(1 - w // 2, w // 2):
                    acc = acc + pad_s[POOL_HALO + j:POOL_HALO + j + length, :]
                cnt = (jnp.minimum(rows + w // 2, length) - jnp.maximum(rows - w // 2, 0)).astype(F32)
                dd = acc / cnt - u
                y = _dot(dd.astype(BF16), wp_ref[...]) * sp_ref[...]
                yc_ref[0, start:start + length, :] = y.astype(yc_ref.dtype)

        uu = (pc_ref[0, start:start + length, :].astype(F32)
              * pxx_ref[0, start:start + length, :].astype(F32))
        pad_s[POOL_HALO:POOL_HALO + length, :] = uu
        y = (pad_s[POOL_HALO - 1:POOL_HALO - 1 + length, :] * wc[0:1]
             + uu * wc[1:2]
             + pad_s[POOL_HALO + 1:POOL_HALO + 1 + length, :] * wc[2:3])
        yd_ref[0, start:start + length, :] = (
            pb_ref[0, start:start + length, :].astype(F32) * y).astype(yd_ref.dtype)


def _mix_cd(l, cd, p, n_lat):
    b, t, _ = cd.shape
    segments = ((0, n_lat), (n_lat, t - n_lat))
    n_g = len(POOL_WINDOWS)

    def col(k):
        return pl.BlockSpec((1, t, LANES), lambda i, g: (i, 0, k * n_g + g))

    out_spec = pl.BlockSpec((1, t, LANES), lambda i, g: (i, 0, g))
    return pl.pallas_call(
        functools.partial(_mix_cd_kernel, segments=segments),
        grid=(b, n_g),
        in_specs=[
            col(0), col(1), col(2), col(3),
            pl.BlockSpec((None, None, POOL_GROUP, POOL_GROUP), lambda i, g: (l, g, 0, 0)),
            pl.BlockSpec((None, 1, LANES), lambda i, g: (l, 0, g)),
            pl.BlockSpec((None, 3, LANES), lambda i, g: (l, 0, g)),
        ],
        out_specs=[out_spec, out_spec],
        out_shape=[jax.ShapeDtypeStruct((b, t, BRANCH_W), BF16)] * 2,
        scratch_shapes=[pltpu.VMEM((max(n_lat, t - n_lat) + 2 * POOL_HALO, LANES), F32)],
        compiler_params=pltpu.CompilerParams(
            dimension_semantics=("parallel", "parallel"), vmem_limit_bytes=VMEM_LIMIT),
        name="mix_cd",
    )(cd, cd, cd, cd, p["w_pool"], p["s_pool"], p["w_conv"])


def _merge_ffn_kernel(*refs, ctx_tile, split):
    ya_ref, yb_ref, yc_ref, yd_ref, gate_ref = refs[:5]
    if split:
        x_ref, xc_ref = refs[5:7]
        refs = refs[7:]
        x = jnp.where(pl.program_id(1) == ctx_tile, xc_ref[0], x_ref[0])
    else:
        x = refs[5][0]
        refs = refs[6:]
    mod_ref, wb_ref, wo_ref, g2_ref, w1_ref, w2_ref, o_ref = refs
    d = D_MODEL
    merged = None
    for n, y_ref in enumerate((ya_ref, yb_ref, yc_ref, yd_ref)):
        proj = _dot(y_ref[0], wb_ref[n])
        gate = jax.nn.sigmoid(gate_ref[0, :, n * d:(n + 1) * d].astype(F32))
        merged = gate * proj if merged is None else merged + gate * proj
    mix = _dot(merged.astype(BF16), wo_ref[...])
    x1 = x + mod_ref[:, 2 * d:3 * d] * mix

    y = _rms(x1) * g2_ref[...]
    h = (y * (1.0 + mod_ref[:, 4 * d:5 * d]) + mod_ref[:, 3 * d:4 * d]).astype(BF16)
    f = None
    for c0 in range(0, D_FF, d):
        a = jnp.maximum(_dot(h, w1_ref[:, c0:c0 + d]), 0.0)
        part = _dot((a * a).astype(BF16), w2_ref[c0:c0 + d, :])
        f = part if f is None else f + part
    o_ref[0] = x1 + mod_ref[:, 5 * d:6 * d] * f


def _merge_ffn(l, ys, gates, x_parts, mods, p, tq, n_lat, n_ctx, with_ctx):
    b, _, d = x_parts[0].shape
    t = n_lat + n_ctx
    ctx_tile = n_lat // tq
    split = len(x_parts) == 2
    row = lambda i, j: (i, j, 0)
    y_spec = pl.BlockSpec((1, tq, BRANCH_W), row)
    nq, out_rows = (t // tq, t) if with_ctx else (ctx_tile, n_lat)
    aliases = {5: 0} if (with_ctx and not split) else {}
    return pl.pallas_call(
        functools.partial(_merge_ffn_kernel, ctx_tile=ctx_tile, split=split),
        grid=(b, nq),
        in_specs=[
            y_spec, y_spec, y_spec, y_spec,
            pl.BlockSpec((1, tq, N_BRANCH * d), row),
        ] + _stream_specs(split, tq, d, ctx_tile) + [
            _mod_spec(l, b, ctx_tile),
            _layer_spec(l, (N_BRANCH, BRANCH_W, d)),
            _layer_spec(l, (d, d)),
            _layer_spec(l, (1, d)),
            _layer_spec(l, (d, D_FF)),
            _layer_spec(l, (D_FF, d)),
        ],
        out_specs=pl.BlockSpec((1, tq, d), row),
        out_shape=jax.ShapeDtypeStruct((b, out_rows, d), F32),
        input_output_aliases=aliases,
        compiler_params=pltpu.CompilerParams(
            dimension_semantics=("parallel", "parallel"), vmem_limit_bytes=VMEM_LIMIT),
        name="merge_ffn",
    )(*ys, gates, *x_parts, mods, p["w_branch"], p["w_o"], p["g_norm2"], p["w_ff1"], p["w_ff2"])


def _rope_tables(n_lat, n_ctx, rot_dim, lane_lo, period, total=LANES):
    rows = n_lat // GRID_W
    row = np.repeat(np.arange(rows, dtype=np.float64), GRID_W)
    col = np.tile(np.arange(GRID_W, dtype=np.float64), rows)
    n_freq = rot_dim // 4
    inv = ROPE_BASE ** (-np.arange(n_freq, dtype=np.float64) / n_freq)
    inv = inv.astype(np.float32).astype(np.float64)
    ang = np.concatenate([row[:, None] * inv, col[:, None] * inv], axis=-1)
    ang = ang.astype(np.float32).astype(np.float64)
    half = rot_dim // 2
    t = n_lat + n_ctx
    cos = np.ones((t, total), np.float32)
    s_left = np.zeros((t, total), np.float32)
    s_right = np.zeros((t, total), np.float32)
    starts = [lane_lo] if period == 0 else list(range(lane_lo, total, period))
    for s0 in starts:
        cos[:n_lat, s0:s0 + half] = np.cos(ang)
        cos[:n_lat, s0 + half:s0 + rot_dim] = np.cos(ang)
        s_left[:n_lat, s0:s0 + half] = -np.sin(ang)
        s_right[:n_lat, s0 + half:s0 + rot_dim] = np.sin(ang)
    return jnp.asarray(cos), jnp.asarray(s_left), jnp.asarray(s_right)


def _score_bounds(gq_a, gk_a, gq_b, gk_b):
    amax = lambda g: jnp.max(jnp.abs(g), axis=-1)
    bound_a = A_HD * amax(gq_a) * amax(gk_a) * (A_HD ** -0.5 * LOG2E)
    nq = jnp.sqrt(B_NOPE * amax(gq_b[:, :B_NOPE]) ** 2 + B_ROPE * amax(gq_b[:, B_NOPE:]) ** 2)
    nk = jnp.sqrt(B_NOPE * amax(gk_b[:, :B_NOPE]) ** 2 + B_ROPE * amax(gk_b[:, B_NOPE:]) ** 2)
    bound_b = nq * nk * (B_QK ** -0.5 * LOG2E)
    margin = 1.05
    return ((bound_a * margin < SAFE_LOG2).astype(jnp.int32),
            (bound_b * margin < SAFE_LOG2).astype(jnp.int32))


def _pad_head_slots(v, width):
    lead = v.shape[:-1]
    v = v.reshape(lead + (-1, width))
    v = jnp.pad(v, [(0, 0)] * len(lead) + [(0, 0), (0, LANES - width)])
    return v.reshape(lead + (-1,))


def kernel(x, c, ctx, c_ctx, w_mod, b_mod, g_norm1, g_norm2, w_in, gq_a, gk_a, lam_a, g_sub_a,
           g_cq, w_uq, g_ckv, w_ukv, gq_b, gk_b, w_pool, s_pool, w_conv, w_branch, w_o,
           w_ff1, w_ff2):
    b, n_lat, d = x.shape
    n_ctx = ctx.shape[1]
    depth = w_mod.shape[0]
    tq = n_ctx
    assert d == D_MODEL and n_lat % ATT_SUB == 0 and n_ctx % ATT_SUB == 0
    assert n_lat % tq == 0 and tq % LANES == 0 and n_lat % GRID_W == 0

    rope_a = _rope_tables(n_lat, n_ctx, A_HD, 0, A_HD)
    rope_b = _rope_tables(n_lat, n_ctx, B_ROPE, B_NOPE, 0)

    mod_rows = -(-(b + 1) // 8) * 8
    cc = jnp.concatenate([c, c_ctx[None, :], jnp.zeros((mod_rows - b - 1, d), F32)], axis=0)
    mods = _modulation(cc, w_mod, b_mod).reshape(depth, mod_rows, 1, N_MOD * d)

    vec = lambda a: a[:, None, :]
    safe_a, safe_b = _score_bounds(gq_a, gk_a, gq_b, gk_b)
    p = {
        "safe_a": safe_a, "safe_b": safe_b,
        "w_head": w_in[:, :, :HEAD_W].astype(BF16),
        "w_kr": jnp.pad(w_in[:, :, HEAD_W:HEAD_W + B_ROPE],
                        ((0, 0), (0, 0), (B_NOPE, LANES - B_NOPE - B_ROPE))).astype(BF16),
        "w_tail": w_in[:, :, HEAD_W + B_ROPE:].astype(BF16),
        "w_uq": _pad_head_slots(w_uq, B_QK).astype(BF16),
        "w_ukv": w_ukv.astype(BF16),
        "w_pool": w_pool.astype(BF16),
        "w_branch": w_branch.astype(BF16),
        "w_o": w_o.astype(BF16),
        "w_ff1": w_ff1.astype(BF16),
        "w_ff2": w_ff2.astype(BF16),
        "g_norm1": vec(g_norm1), "g_norm2": vec(g_norm2),
        "g_cq": vec(g_cq), "g_ckv": vec(g_ckv),
        "gq_a": vec(jnp.tile(gq_a, (1, 2))), "gk_a": vec(jnp.tile(gk_a, (1, 2))),
        "gq_b": vec(jnp.pad(gq_b, ((0, 0), (0, LANES - B_QK)))),
        "gk_b": vec(jnp.pad(gk_b, ((0, 0), (0, LANES - B_QK)))),
        "lam_a": lam_a, "g_sub_a": vec(g_sub_a),
        "s_pool": vec(s_pool), "w_conv": w_conv,
    }

    x_parts = (x, ctx)
    for l in range(depth):
        last = l == depth - 1
        lam_init = 0.8 - 0.6 * math.exp(-0.3 * l)
        gates, cd, qa, ka, vxa, qb, kb, vxb = _inproj(
            l, x_parts, mods, p, rope_a, rope_b, tq, n_lat, n_ctx)
        ya = _attn_a(l, qa, ka, vxa, p, n_lat, lam_init)
        yb = _attn_b(l, qb, kb, vxb, p, n_lat)
        yc, yd = _mix_cd(l, cd, p, n_lat)
        x_parts = (_merge_ffn(l, (ya, yb, yc, yd), gates, x_parts, mods, p, tq, n_lat, n_ctx,
                              not last),)
    return x_parts[0]
```

```python
import functools
import math

import numpy as np
import jax
import jax.numpy as jnp
from jax import lax
from jax.experimental import pallas as pl
from jax.experimental.pallas import tpu as pltpu

F32 = jnp.float32
BF16 = jnp.bfloat16

D_MODEL = 1024
GRID_W = 64
ROPE_BASE = 10000.0
EPS = 1e-6
LOG2E = math.log2(math.e)

A_HEADS = 4
A_HD = 64
A_VD = 128
B_HEADS = 8
B_NOPE = 64
B_ROPE = 32
B_QK = B_NOPE + B_ROPE
B_VD = 64
B_QLORA = 384
B_KVLORA = 256
POOL_WINDOWS = (2, 4, 8, 16)
POOL_GROUP = 128
POOL_HALO = 8
BRANCH_W = 512
N_BRANCH = 4
D_FF = 4 * D_MODEL
N_MOD = 6

LANES = 128

QA_OFF = 0
KA_OFF = QA_OFF + 512
VA_OFF = KA_OFF + 512
CQ_OFF = VA_OFF + 512
CKV_OFF = CQ_OFF + B_QLORA
HEAD_W = CKV_OFF + B_KVLORA
CD_OFF = 0
CD_W = 4 * 512
GATE_OFF = CD_OFF + CD_W
TAIL_W = GATE_OFF + N_BRANCH * D_MODEL
IN_FILL = 256
RAW_QB = CQ_OFF
RAW_KVB = RAW_QB + B_HEADS * LANES
RAW_KR = RAW_KVB + B_HEADS * LANES
RAW_W = RAW_KR + LANES

ATT_SUB = 128
SAFE_LOG2 = 40.0

VMEM_LIMIT = 56 * 1024 * 1024


def _dot(a, b):
    return jnp.dot(a, b, preferred_element_type=F32)


def _dot_nt(a, b):
    return lax.dot_general(a, b, (((1,), (1,)), ((), ())), preferred_element_type=F32)


def _lane_iota(n=LANES):
    return lax.broadcasted_iota(jnp.int32, (1, n), 1)


def _rms(x):
    return x * lax.rsqrt(jnp.mean(x * x, axis=-1, keepdims=True) + EPS)


def _rope(t, tabs, half):
    cos, s_left, s_right = tabs
    n = t.shape[-1]
    return (t * cos + pltpu.roll(t, n - half, 1) * s_left
            + pltpu.roll(t, half, 1) * s_right)


def _layer_spec(l, shape):
    nd = len(shape)
    return pl.BlockSpec((None,) + tuple(shape), lambda *_: (l,) + (0,) * nd,
                        pipeline_mode=pl.Buffered(1))


def _stream_specs(split, tq, d, ctx_tile):
    if split:
        return [pl.BlockSpec((1, tq, d), lambda i, j: (i, jnp.minimum(j, ctx_tile - 1), 0)),
                pl.BlockSpec((1, tq, d), lambda i, j: (i, 0, 0))]
    return [pl.BlockSpec((1, tq, d), lambda i, j: (i, j, 0))]


def _mod_spec(l, b, ctx_tile):
    return pl.BlockSpec((None, None, 1, N_MOD * D_MODEL),
                        lambda i, j: (l, jnp.where(j == ctx_tile, b, i), 0, 0))


def _mod_kernel(c_ref, w_ref, b_ref, o_ref):
    c = c_ref[...]
    h = (c * jax.nn.sigmoid(c)).astype(BF16)
    o_ref[...] = _dot(h, w_ref[...].astype(BF16)) + b_ref[...]


def _modulation(cc, w_mod, b_mod):
    depth, d, n = w_mod.shape
    rows = cc.shape[0]
    tn = 1536
    return pl.pallas_call(
        _mod_kernel,
        grid=(depth, n // tn),
        in_specs=[
            pl.BlockSpec((rows, d), lambda l, j: (0, 0)),
            pl.BlockSpec((None, d, tn), lambda l, j: (l, 0, j)),
            pl.BlockSpec((None, 1, tn), lambda l, j: (l, 0, j)),
        ],
        out_specs=pl.BlockSpec((None, rows, tn), lambda l, j: (l, 0, j)),
        out_shape=jax.ShapeDtypeStruct((depth, rows, n), F32),
        compiler_params=pltpu.CompilerParams(
            dimension_semantics=("parallel", "parallel"), vmem_limit_bytes=VMEM_LIMIT),
        name="modulation",
    )(cc, w_mod, b_mod.reshape(depth, 1, n))


def _inproj_kernel(*refs, nt, ctx_tile, n_tiles, split):
    s = pl.program_id(0)
    n_x = 2 if split else 1
    x_refs, refs = refs[:n_x], refs[n_x:]

    def load_x():
        if split:
            is_ctx = jnp.minimum(s, n_tiles - 1) % nt == ctx_tile
            return jnp.where(is_ctx, x_refs[1][0], x_refs[0][0])
        return x_refs[0][0]

    (mod_ref, g_ref, wh_ref, wkr_ref, wt_ref, wuq_ref, wukv_ref, gcq_ref, gckv_ref,
     gqa_ref, gka_ref, gqb_ref, gkb_ref, ca_ref, la_ref, ra_ref, cb_ref, lb_ref, rb_ref,
     gate_ref, cd_ref, qa_ref, ka_ref, vxa_ref, qb_ref, kb_ref, vxb_ref,
     h0, h1, raw0, raw1) = refs
    d = D_MODEL
    tq = h0.shape[0]

    @pl.when(s == 0)
    def _warm_up():
        for ref in (h0, h1, raw0, raw1):
            ref[...] = jnp.zeros(ref.shape, ref.dtype)

    def stages(h_w, h_r, raw_w, raw_r):
        h = h_r[...]
        cq = _dot(h, wh_ref[:, CQ_OFF:CQ_OFF + B_QLORA])
        ckv = _dot(h, wh_ref[:, CKV_OFF:CKV_OFF + B_KVLORA])
        cqn = (_rms(cq) * gcq_ref[...]).astype(BF16)
        ckvn = (_rms(ckv) * gckv_ref[...]).astype(BF16)

        def passthrough(o_ref, off, width):
            for c0 in range(0, width, IN_FILL):
                o_ref[0, :, c0:c0 + IN_FILL] = _dot(
                    h, wt_ref[:, off + c0:off + c0 + IN_FILL]).astype(o_ref.dtype)

        passthrough(cd_ref, CD_OFF, CD_W)
        raw_w[:, RAW_QB:RAW_QB + B_HEADS * LANES] = _dot(cqn, wuq_ref[...])
        raw_w[:, RAW_KVB:RAW_KVB + B_HEADS * LANES] = _dot(ckvn, wukv_ref[...])
        for c0 in range(0, CQ_OFF, IN_FILL):
            raw_w[:, c0:c0 + IN_FILL] = _dot(h, wh_ref[:, c0:c0 + IN_FILL])
        raw_w[:, RAW_KR:RAW_KR + LANES] = _dot(h, wkr_ref[...])
        passthrough(gate_ref, GATE_OFF, N_BRANCH * d)

        lane = _lane_iota()
        lo = lane < A_HD
        nope = lane < B_NOPE
        rope_a = (ca_ref[...], la_ref[...], ra_ref[...])
        rope_b = (cb_ref[...], lb_ref[...], rb_ref[...])

        gq_b = gqb_ref[...] * (B_QK ** -0.5 * LOG2E)
        for hd in range(B_HEADS):
            q = raw_r[:, RAW_QB + hd * LANES:RAW_QB + (hd + 1) * LANES]
            q2 = q * q
            s_n = jnp.sum(jnp.where(nope, q2, 0.0), axis=-1, keepdims=True)
            s_r = jnp.sum(jnp.where(nope, 0.0, q2), axis=-1, keepdims=True)
            inv = jnp.where(nope, lax.rsqrt(s_n * (1.0 / B_NOPE) + EPS),
                            lax.rsqrt(s_r * (1.0 / B_ROPE) + EPS))
            qb_ref[0, :, hd * LANES:(hd + 1) * LANES] = _rope(
                q * inv * gq_b, rope_b, B_ROPE // 2).astype(BF16)

        kr = raw_r[:, RAW_KR:RAW_KR + LANES]
        krn = kr * lax.rsqrt(jnp.sum(kr * kr, axis=-1, keepdims=True) * (1.0 / B_ROPE) + EPS)
        krn = _rope(krn * jnp.where(nope, 0.0, gkb_ref[...]), rope_b, B_ROPE // 2)
        gk_nope = jnp.where(nope, gkb_ref[...], 0.0)
        for hd in range(B_HEADS):
            kv = raw_r[:, RAW_KVB + hd * LANES:RAW_KVB + (hd + 1) * LANES]
            ss = jnp.sum(jnp.where(nope, kv * kv, 0.0), axis=-1, keepdims=True)
            kn = kv * lax.rsqrt(ss * (1.0 / B_NOPE) + EPS) * gk_nope
            kb_ref[0, :, hd * LANES:(hd + 1) * LANES] = (kn + krn).astype(BF16)
            vxb_ref[0, :, hd * LANES:(hd + 1) * LANES] = jnp.where(nope, 1.0, kv).astype(BF16)

        def norm_halves(t):
            t2 = t * t
            s_lo = jnp.sum(jnp.where(lo, t2, 0.0), axis=-1, keepdims=True)
            s_hi = jnp.sum(jnp.where(lo, 0.0, t2), axis=-1, keepdims=True)
            return t * jnp.where(lo, lax.rsqrt(s_lo * (1.0 / A_HD) + EPS),
                                 lax.rsqrt(s_hi * (1.0 / A_HD) + EPS))

        gq_a = gqa_ref[...] * (A_HD ** -0.5 * LOG2E)
        for off, gain, o_ref in ((QA_OFF, gq_a, qa_ref), (KA_OFF, gka_ref[...], ka_ref)):
            for hd in range(A_HEADS):
                t = raw_r[:, off + hd * LANES:off + (hd + 1) * LANES]
                o_ref[0, :, hd * LANES:(hd + 1) * LANES] = _rope(
                    norm_halves(t) * gain, rope_a, A_HD // 2).astype(BF16)
        ones = jnp.ones((tq, A_VD), BF16)
        for hd in range(A_HEADS):
            vxa_ref[0, :, 2 * hd * A_VD:(2 * hd + 1) * A_VD] = raw_r[
                :, VA_OFF + hd * A_VD:VA_OFF + (hd + 1) * A_VD].astype(BF16)
            vxa_ref[0, :, (2 * hd + 1) * A_VD:(2 * hd + 2) * A_VD] = ones

        y = _rms(load_x()) * g_ref[...]
        h_w[...] = (y * (1.0 + mod_ref[:, d:2 * d]) + mod_ref[:, 0:d]).astype(BF16)

    pl.when(s % 2 == 0)(lambda: stages(h0, h1, raw1, raw0))
    pl.when(s % 2 == 1)(lambda: stages(h1, h0, raw0, raw1))


def _inproj(l, x_parts, mods, p, rope_a, rope_b, tq, n_lat, n_ctx):
    b, _, d = x_parts[0].shape
    t = n_lat + n_ctx
    nt = t // tq
    ctx_tile = n_lat // tq
    n_tiles = b * nt
    split = len(x_parts) == 2
    tile_a = lambda s: jnp.minimum(s, n_tiles - 1)
    tile_b = lambda s: jnp.clip(s - 1, 0, n_tiles - 1)
    tile_c = lambda s: jnp.clip(s - 2, 0, n_tiles - 1)
    tab = pl.BlockSpec((tq, LANES), lambda s: (tile_c(s) % nt, 0))
    vec = lambda n: _layer_spec(l, (1, n))
    if split:
        x_specs = [pl.BlockSpec((1, tq, d), lambda s: (tile_a(s) // nt,
                                                       jnp.minimum(tile_a(s) % nt, ctx_tile - 1), 0)),
                   pl.BlockSpec((1, tq, d), lambda s: (tile_a(s) // nt, 0, 0))]
    else:
        x_specs = [pl.BlockSpec((1, tq, d), lambda s: (tile_a(s) // nt, tile_a(s) % nt, 0))]

    def out(width, dtype, tile):
        return (pl.BlockSpec((1, tq, width), lambda s: (tile(s) // nt, tile(s) % nt, 0)),
                jax.ShapeDtypeStruct((b, t, width), dtype))

    outs = [out(N_BRANCH * d, BF16, tile_b), out(CD_W, BF16, tile_b),
            out(A_HEADS * LANES, BF16, tile_c), out(A_HEADS * LANES, BF16, tile_c),
            out(2 * A_HEADS * A_VD, BF16, tile_c), out(B_HEADS * LANES, BF16, tile_c),
            out(B_HEADS * LANES, BF16, tile_c), out(B_HEADS * LANES, BF16, tile_c)]
    return pl.pallas_call(
        functools.partial(_inproj_kernel, nt=nt, ctx_tile=ctx_tile, n_tiles=n_tiles, split=split),
        grid=(n_tiles + 2,),
        in_specs=x_specs + [
            pl.BlockSpec((None, None, 1, N_MOD * d),
                         lambda s: (l, jnp.where(tile_a(s) % nt == ctx_tile, b, tile_a(s) // nt), 0, 0)),
            vec(d),
            _layer_spec(l, (d, HEAD_W)),
            _layer_spec(l, (d, LANES)),
            _layer_spec(l, (d, TAIL_W)),
            _layer_spec(l, (B_QLORA, B_HEADS * LANES)),
            _layer_spec(l, (B_KVLORA, B_HEADS * LANES)),
            vec(B_QLORA), vec(B_KVLORA), vec(LANES), vec(LANES), vec(LANES), vec(LANES),
            tab, tab, tab, tab, tab, tab,
        ],
        out_specs=[o[0] for o in outs],
        out_shape=[o[1] for o in outs],
        scratch_shapes=[pltpu.VMEM((tq, d), BF16), pltpu.VMEM((tq, d), BF16),
                        pltpu.VMEM((tq, RAW_W), F32), pltpu.VMEM((tq, RAW_W), F32)],
        compiler_params=pltpu.CompilerParams(
            dimension_semantics=("arbitrary",), vmem_limit_bytes=VMEM_LIMIT),
        name="inproj",
    )(*x_parts, mods, p["g_norm1"], p["w_head"], p["w_kr"], p["w_tail"], p["w_uq"], p["w_ukv"],
      p["g_cq"], p["g_ckv"], p["gq_a"], p["gk_a"], p["gq_b"], p["gk_b"], *rope_a, *rope_b)


def _softmax_pv(q, k, vx, shift_max):
    s = _dot_nt(q, k)
    if shift_max:
        s = s - jnp.max(s, axis=-1, keepdims=True)
    return _dot(jnp.exp2(s).astype(BF16), vx)


def _sweep(make_tile, bounded, n_lat, t_all):
    def run(shift_max):
        tile = make_tile(shift_max)
        for s0 in range(0, t_all, ATT_SUB):
            tile(s0, 0 if s0 < n_lat else n_lat)

    pl.when(bounded)(lambda: run(False))
    pl.when(jnp.logical_not(bounded))(lambda: run(True))


def _attn_a_kernel(safe_ref, q_ref, k_ref, vx_ref, lam_ref, gsub_ref, o_ref, *, l, n_lat, lam_init):
    t_all = k_ref.shape[1]
    lo = _lane_iota() < A_HD
    la = lam_ref[...].astype(F32)
    lam = (jnp.exp(jnp.sum(la[0:1] * la[1:2], axis=-1, keepdims=True))
           - jnp.exp(jnp.sum(la[2:3] * la[3:4], axis=-1, keepdims=True)) + lam_init)
    post = gsub_ref[...] * (1.0 - lam_init)

    def make_tile(shift_max):
        def tile(r0, key_lo):
            q = q_ref[0, r0:r0 + ATT_SUB, :]
            k = k_ref[0, key_lo:t_all, :]
            vx = vx_ref[0, key_lo:t_all, :]
            zero = jnp.zeros_like(q)
            r1 = _softmax_pv(jnp.where(lo, q, zero), k, vx, shift_max)
            r2 = _softmax_pv(jnp.where(lo, zero, q), k, vx, shift_max)
            o = (r1[:, 0:A_VD] / r1[:, A_VD:A_VD + 1]
                 - lam * (r2[:, 0:A_VD] / r2[:, A_VD:A_VD + 1]))
            o_ref[0, r0:r0 + ATT_SUB, :] = (_rms(o) * post).astype(o_ref.dtype)
        return tile

    _sweep(make_tile, safe_ref[l] != 0, n_lat, t_all)


def _attn_a(l, qa, ka, vxa, p, n_lat, lam_init):
    b, t, _ = qa.shape
    head = lambda i, h: (i, 0, h)
    return pl.pallas_call(
        functools.partial(_attn_a_kernel, l=l, n_lat=n_lat, lam_init=lam_init),
        grid=(b, A_HEADS),
        in_specs=[
            pl.BlockSpec(memory_space=pltpu.SMEM),
            pl.BlockSpec((1, t, LANES), head),
            pl.BlockSpec((1, t, LANES), head),
            pl.BlockSpec((1, t, 2 * A_VD), head),
            _layer_spec(l, (4, A_HD)),
            _layer_spec(l, (1, LANES)),
        ],
        out_specs=pl.BlockSpec((1, t, LANES), head),
        out_shape=jax.ShapeDtypeStruct((b, t, BRANCH_W), BF16),
        compiler_params=pltpu.CompilerParams(
            dimension_semantics=("parallel", "parallel"), vmem_limit_bytes=VMEM_LIMIT),
        name="attn_a",
    )(p["safe_a"], qa, ka, vxa, p["lam_a"], p["g_sub_a"])


def _attn_b_kernel(safe_ref, q_ref, k_ref, vx_ref, o_ref, *, l, n_lat):
    t_all = k_ref.shape[1]
    nope = _lane_iota() < B_NOPE

    def make_tile(shift_max):
        def tile(r0, key_lo):
            outs = []
            for hh in range(2):
                sl = slice(hh * LANES, (hh + 1) * LANES)
                r = _softmax_pv(q_ref[0, r0:r0 + ATT_SUB, sl], k_ref[0, key_lo:t_all, sl],
                                vx_ref[0, key_lo:t_all, :], shift_max)[:, sl]
                outs.append(r / r[:, 0:1])
            o = jnp.where(nope, pltpu.roll(outs[0], B_VD, 1), outs[1])
            o_ref[0, r0:r0 + ATT_SUB, :] = o.astype(o_ref.dtype)
        return tile

    _sweep(make_tile, safe_ref[l] != 0, n_lat, t_all)


def _attn_b(l, qb, kb, vxb, p, n_lat):
    b, t, _ = qb.shape
    pair = lambda i, h: (i, 0, h)
    spec = pl.BlockSpec((1, t, 2 * LANES), pair)
    return pl.pallas_call(
        functools.partial(_attn_b_kernel, l=l, n_lat=n_lat),
        grid=(b, B_HEADS // 2),
        in_specs=[pl.BlockSpec(memory_space=pltpu.SMEM), spec, spec, spec],
        out_specs=pl.BlockSpec((1, t, LANES), pair),
        out_shape=jax.ShapeDtypeStruct((b, t, BRANCH_W), BF16),
        compiler_params=pltpu.CompilerParams(
            dimension_semantics=("parallel", "parallel"), vmem_limit_bytes=VMEM_LIMIT),
        name="attn_b",
    )(p["safe_b"], qb, kb, vxb)


def _mix_cd_kernel(u_ref, pb_ref, pc_ref, pxx_ref, wp_ref, sp_ref, wc_ref,
                   yc_ref, yd_ref, pad_s, *, segments):
    g = pl.program_id(1)
    zeros_halo = jnp.zeros((POOL_HALO, LANES), F32)
    wc = wc_ref[...]

    for start, length in segments:
        rows = lax.broadcasted_iota(jnp.int32, (length, 1), 0)
        u = u_ref[0, start:start + length, :].astype(F32)
        pad_s[0:POOL_HALO, :] = zeros_halo
        pad_s[POOL_HALO:POOL_HALO + length, :] = u
        pad_s[POOL_HALO + length:2 * POOL_HALO + length, :] = zeros_halo

        for gi, w in enumerate(POOL_WINDOWS):
            @pl.when(g == gi)
            def _pool(w=w):
                acc = pad_s[POOL_HALO - w // 2:POOL_HALO - w // 2 + length, :]
                for j in range(1 - w // 2, w // 2):
                    acc = acc + pad_s[POOL_HALO + j:POOL_HALO + j + length, :]
                cnt = (jnp.minimum(rows + w // 2, length) - jnp.maximum(rows - w // 2, 0)).astype(F32)
                dd = acc / cnt - u
                y = _dot(dd.astype(BF16), wp_ref[...]) * sp_ref[...]
                yc_ref[0, start:start + length, :] = y.astype(yc_ref.dtype)

        uu = (pc_ref[0, start:start + length, :].astype(F32)
              * pxx_ref[0, start:start + length, :].astype(F32))
        pad_s[POOL_HALO:POOL_HALO + length, :] = uu
        y = (pad_s[POOL_HALO - 1:POOL_HALO - 1 + length, :] * wc[0:1]
             + uu * wc[1:2]
             + pad_s[POOL_HALO + 1:POOL_HALO + 1 + length, :] * wc[2:3])
        yd_ref[0, start:start + length, :] = (
            pb_ref[0, start:start + length, :].astype(F32) * y).astype(yd_ref.dtype)


def _mix_cd(l, cd, p, n_lat):
    b, t, _ = cd.shape
    segments = ((0, n_lat), (n_lat, t - n_lat))
    n_g = len(POOL_WINDOWS)

    def col(k):
        return pl.BlockSpec((1, t, LANES), lambda i, g: (i, 0, k * n_g + g))

    out_spec = pl.BlockSpec((1, t, LANES), lambda i, g: (i, 0, g))
    return pl.pallas_call(
        functools.partial(_mix_cd_kernel, segments=segments),
        grid=(b, n_g),
        in_specs=[
            col(0), col(1), col(2), col(3),
            pl.BlockSpec((None, None, POOL_GROUP, POOL_GROUP), lambda i, g: (l, g, 0, 0)),
            pl.BlockSpec((None, 1, LANES), lambda i, g: (l, 0, g)),
            pl.BlockSpec((None, 3, LANES), lambda i, g: (l, 0, g)),
        ],
        out_specs=[out_spec, out_spec],
        out_shape=[jax.ShapeDtypeStruct((b, t, BRANCH_W), BF16)] * 2,
        scratch_shapes=[pltpu.VMEM((max(n_lat, t - n_lat) + 2 * POOL_HALO, LANES), F32)],
        compiler_params=pltpu.CompilerParams(
            dimension_semantics=("parallel", "parallel"), vmem_limit_bytes=VMEM_LIMIT),
        name="mix_cd",
    )(cd, cd, cd, cd, p["w_pool"], p["s_pool"], p["w_conv"])


def _merge_ffn_kernel(*refs, ctx_tile, split):
    ya_ref, yb_ref, yc_ref, yd_ref, gate_ref = refs[:5]
    if split:
        x_ref, xc_ref = refs[5:7]
        refs = refs[7:]
        x = jnp.where(pl.program_id(1) == ctx_tile, xc_ref[0], x_ref[0])
    else:
        x = refs[5][0]
        refs = refs[6:]
    mod_ref, wb_ref, wo_ref, g2_ref, w1_ref, w2_ref, o_ref = refs
    d = D_MODEL
    merged = None
    for n, y_ref in enumerate((ya_ref, yb_ref, yc_ref, yd_ref)):
        proj = _dot(y_ref[0], wb_ref[n])
        gate = jax.nn.sigmoid(gate_ref[0, :, n * d:(n + 1) * d].astype(F32))
        merged = gate * proj if merged is None else merged + gate * proj
    mix = _dot(merged.astype(BF16), wo_ref[...])
    x1 = x + mod_ref[:, 2 * d:3 * d] * mix

    y = _rms(x1) * g2_ref[...]
    h = (y * (1.0 + mod_ref[:, 4 * d:5 * d]) + mod_ref[:, 3 * d:4 * d]).astype(BF16)
    f = None
    for c0 in range(0, D_FF, d):
        a = jnp.maximum(_dot(h, w1_ref[:, c0:c0 + d]), 0.0)
        part = _dot((a * a).astype(BF16), w2_ref[c0:c0 + d, :])
        f = part if f is None else f + part
    o_ref[0] = x1 + mod_ref[:, 5 * d:6 * d] * f


def _merge_ffn(l, ys, gates, x_parts, mods, p, tq, n_lat, n_ctx, with_ctx):
    b, _, d = x_parts[0].shape
    t = n_lat + n_ctx
    ctx_tile = n_lat // tq
    split = len(x_parts) == 2
    row = lambda i, j: (i, j, 0)
    y_spec = pl.BlockSpec((1, tq, BRANCH_W), row)
    nq, out_rows = (t // tq, t) if with_ctx else (ctx_tile, n_lat)
    aliases = {5: 0} if (with_ctx and not split) else {}
    return pl.pallas_call(
        functools.partial(_merge_ffn_kernel, ctx_tile=ctx_tile, split=split),
        grid=(b, nq),
        in_specs=[
            y_spec, y_spec, y_spec, y_spec,
            pl.BlockSpec((1, tq, N_BRANCH * d), row),
        ] + _stream_specs(split, tq, d, ctx_tile) + [
            _mod_spec(l, b, ctx_tile),
            _layer_spec(l, (N_BRANCH, BRANCH_W, d)),
            _layer_spec(l, (d, d)),
            _layer_spec(l, (1, d)),
            _layer_spec(l, (d, D_FF)),
            _layer_spec(l, (D_FF, d)),
        ],
        out_specs=pl.BlockSpec((1, tq, d), row),
        out_shape=jax.ShapeDtypeStruct((b, out_rows, d), F32),
        input_output_aliases=aliases,
        compiler_params=pltpu.CompilerParams(
            dimension_semantics=("parallel", "parallel"), vmem_limit_bytes=VMEM_LIMIT),
        name="merge_ffn",
    )(*ys, gates, *x_parts, mods, p["w_branch"], p["w_o"], p["g_norm2"], p["w_ff1"], p["w_ff2"])


def _rope_tables(n_lat, n_ctx, rot_dim, lane_lo, period, total=LANES):
    rows = n_lat // GRID_W
    row = np.repeat(np.arange(rows, dtype=np.float64), GRID_W)
    col = np.tile(np.arange(GRID_W, dtype=np.float64), rows)
    n_freq = rot_dim // 4
    inv = ROPE_BASE ** (-np.arange(n_freq, dtype=np.float64) / n_freq)
    inv = inv.astype(np.float32).astype(np.float64)
    ang = np.concatenate([row[:, None] * inv, col[:, None] * inv], axis=-1)
    ang = ang.astype(np.float32).astype(np.float64)
    half = rot_dim // 2
    t = n_lat + n_ctx
    cos = np.ones((t, total), np.float32)
    s_left = np.zeros((t, total), np.float32)
    s_right = np.zeros((t, total), np.float32)
    starts = [lane_lo] if period == 0 else list(range(lane_lo, total, period))
    for s0 in starts:
        cos[:n_lat, s0:s0 + half] = np.cos(ang)
        cos[:n_lat, s0 + half:s0 + rot_dim] = np.cos(ang)
        s_left[:n_lat, s0:s0 + half] = -np.sin(ang)
        s_right[:n_lat, s0 + half:s0 + rot_dim] = np.sin(ang)
    return jnp.asarray(cos), jnp.asarray(s_left), jnp.asarray(s_right)


def _score_bounds(gq_a, gk_a, gq_b, gk_b):
    amax = lambda g: jnp.max(jnp.abs(g), axis=-1)
    bound_a = A_HD * amax(gq_a) * amax(gk_a) * (A_HD ** -0.5 * LOG2E)
    nq = jnp.sqrt(B_NOPE * amax(gq_b[:, :B_NOPE]) ** 2 + B_ROPE * amax(gq_b[:, B_NOPE:]) ** 2)
    nk = jnp.sqrt(B_NOPE * amax(gk_b[:, :B_NOPE]) ** 2 + B_ROPE * amax(gk_b[:, B_NOPE:]) ** 2)
    bound_b = nq * nk * (B_QK ** -0.5 * LOG2E)
    margin = 1.05
    return ((bound_a * margin < SAFE_LOG2).astype(jnp.int32),
            (bound_b * margin < SAFE_LOG2).astype(jnp.int32))


def _pad_head_slots(v, width):
    lead = v.shape[:-1]
    v = v.reshape(lead + (-1, width))
    v = jnp.pad(v, [(0, 0)] * len(lead) + [(0, 0), (0, LANES - width)])
    return v.reshape(lead + (-1,))


def kernel(x, c, ctx, c_ctx, w_mod, b_mod, g_norm1, g_norm2, w_in, gq_a, gk_a, lam_a, g_sub_a,
           g_cq, w_uq, g_ckv, w_ukv, gq_b, gk_b, w_pool, s_pool, w_conv, w_branch, w_o,
           w_ff1, w_ff2):
    b, n_lat, d = x.shape
    n_ctx = ctx.shape[1]
    depth = w_mod.shape[0]
    tq = n_ctx
    assert d == D_MODEL and n_lat % ATT_SUB == 0 and n_ctx % ATT_SUB == 0
    assert n_lat % tq == 0 and tq % LANES == 0 and n_lat % GRID_W == 0

    rope_a = _rope_tables(n_lat, n_ctx, A_HD, 0, A_HD)
    rope_b = _rope_tables(n_lat, n_ctx, B_ROPE, B_NOPE, 0)

    mod_rows = -(-(b + 1) // 8) * 8
    cc = jnp.concatenate([c, c_ctx[None, :], jnp.zeros((mod_rows - b - 1, d), F32)], axis=0)
    mods = _modulation(cc, w_mod, b_mod).reshape(depth, mod_rows, 1, N_MOD * d)

    vec = lambda a: a[:, None, :]
    safe_a, safe_b = _score_bounds(gq_a, gk_a, gq_b, gk_b)
    p = {
        "safe_a": safe_a, "safe_b": safe_b,
        "w_head": w_in[:, :, :HEAD_W].astype(BF16),
        "w_kr": jnp.pad(w_in[:, :, HEAD_W:HEAD_W + B_ROPE],
                        ((0, 0), (0, 0), (B_NOPE, LANES - B_NOPE - B_ROPE))).astype(BF16),
        "w_tail": w_in[:, :, HEAD_W + B_ROPE:].astype(BF16),
        "w_uq": _pad_head_slots(w_uq, B_QK).astype(BF16),
        "w_ukv": w_ukv.astype(BF16),
        "w_pool": w_pool.astype(BF16),
        "w_branch": w_branch.astype(BF16),
        "w_o": w_o.astype(BF16),
        "w_ff1": w_ff1.astype(BF16),
        "w_ff2": w_ff2.astype(BF16),
        "g_norm1": vec(g_norm1), "g_norm2": vec(g_norm2),
        "g_cq": vec(g_cq), "g_ckv": vec(g_ckv),
        "gq_a": vec(jnp.tile(gq_a, (1, 2))), "gk_a": vec(jnp.tile(gk_a, (1, 2))),
        "gq_b": vec(jnp.pad(gq_b, ((0, 0), (0, LANES - B_QK)))),
        "gk_b": vec(jnp.pad(gk_b, ((0, 0), (0, LANES - B_QK)))),
        "lam_a": lam_a, "g_sub_a": vec(g_sub_a),
        "s_pool": vec(s_pool), "w_conv": w_conv,
    }

    x_parts = (x, ctx)
    for l in range(depth):
        last = l == depth - 1
        lam_init = 0.8 - 0.6 * math.exp(-0.3 * l)
        gates, cd, qa, ka, vxa, qb, kb, vxb = _inproj(
            l, x_parts, mods, p, rope_a, rope_b, tq, n_lat, n_ctx)
        ya = _attn_a(l, qa, ka, vxa, p, n_lat, lam_init)
        yb = _attn_b(l, qb, kb, vxb, p, n_lat)
        yc, yd = _mix_cd(l, cd, p, n_lat)
        x_parts = (_merge_ffn(l, (ya, yb, yc, yd), gates, x_parts, mods, p, tq, n_lat, n_ctx,
                              not last),)
    return x_parts[0]
```

```python
import functools
import math

import numpy as np
import jax
import jax.numpy as jnp
from jax import lax
from jax.experimental import pallas as pl
from jax.experimental.pallas import tpu as pltpu

F32 = jnp.float32
BF16 = jnp.bfloat16

D_MODEL = 1024
GRID_W = 64
ROPE_BASE = 10000.0
EPS = 1e-6
LOG2E = math.log2(math.e)

A_HEADS = 4
A_HD = 64
A_VD = 128
B_HEADS = 8
B_NOPE = 64
B_ROPE = 32
B_QK = B_NOPE + B_ROPE
B_VD = 64
B_QLORA = 384
B_KVLORA = 256
POOL_WINDOWS = (2, 4, 8, 16)
POOL_GROUP = 128
POOL_HALO = 8
BRANCH_W = 512
N_BRANCH = 4
D_FF = 4 * D_MODEL
N_MOD = 6

LANES = 128

QA_OFF = 0
KA_OFF = QA_OFF + 512
VA_OFF = KA_OFF + 512
CQ_OFF = VA_OFF + 512
CKV_OFF = CQ_OFF + B_QLORA
HEAD_W = CKV_OFF + B_KVLORA
CD_OFF = 0
CD_W = 4 * 512
GATE_OFF = CD_OFF + CD_W
TAIL_W = GATE_OFF + N_BRANCH * D_MODEL
IN_FILL = 256
RAW_QB = CQ_OFF
RAW_KVB = RAW_QB + B_HEADS * LANES
RAW_KR = RAW_KVB + B_HEADS * LANES
RAW_W = RAW_KR + LANES

ATT_SUB = 128
SAFE_LOG2 = 40.0

VMEM_LIMIT = 56 * 1024 * 1024


def _dot(a, b):
    return jnp.dot(a, b, preferred_element_type=F32)


def _dot_nt(a, b):
    return lax.dot_general(a, b, (((1,), (1,)), ((), ())), preferred_element_type=F32)


def _lane_iota(n=LANES):
    return lax.broadcasted_iota(jnp.int32, (1, n), 1)


def _rms(x):
    return x * lax.rsqrt(jnp.mean(x * x, axis=-1, keepdims=True) + EPS)


def _rope(t, tabs, half):
    cos, s_left, s_right = tabs
    n = t.shape[-1]
    return (t * cos + pltpu.roll(t, n - half, 1) * s_left
            + pltpu.roll(t, half, 1) * s_right)


def _layer_spec(l, shape):
    nd = len(shape)
    return pl.BlockSpec((None,) + tuple(shape), lambda *_: (l,) + (0,) * nd,
                        pipeline_mode=pl.Buffered(1))


def _stream_specs(split, tq, d, ctx_tile):
    if split:
        return [pl.BlockSpec((1, tq, d), lambda i, j: (i, jnp.minimum(j, ctx_tile - 1), 0)),
                pl.BlockSpec((1, tq, d), lambda i, j: (i, 0, 0))]
    return [pl.BlockSpec((1, tq, d), lambda i, j: (i, j, 0))]


def _mod_spec(l, b, ctx_tile):
    return pl.BlockSpec((None, None, 1, N_MOD * D_MODEL),
                        lambda i, j: (l, jnp.where(j == ctx_tile, b, i), 0, 0))


def _mod_kernel(c_ref, w_ref, b_ref, o_ref):
    c = c_ref[...]
    h = (c * jax.nn.sigmoid(c)).astype(BF16)
    o_ref[...] = _dot(h, w_ref[...].astype(BF16)) + b_ref[...]


def _modulation(cc, w_mod, b_mod):
    depth, d, n = w_mod.shape
    rows = cc.shape[0]
    tn = 1536
    return pl.pallas_call(
        _mod_kernel,
        grid=(depth, n // tn),
        in_specs=[
            pl.BlockSpec((rows, d), lambda l, j: (0, 0)),
            pl.BlockSpec((None, d, tn), lambda l, j: (l, 0, j)),
            pl.BlockSpec((None, 1, tn), lambda l, j: (l, 0, j)),
        ],
        out_specs=pl.BlockSpec((None, rows, tn), lambda l, j: (l, 0, j)),
        out_shape=jax.ShapeDtypeStruct((depth, rows, n), F32),
        compiler_params=pltpu.CompilerParams(
            dimension_semantics=("parallel", "parallel"), vmem_limit_bytes=VMEM_LIMIT),
        name="modulation",
    )(cc, w_mod, b_mod.reshape(depth, 1, n))


def _inproj_kernel(*refs, nt, ctx_tile, n_tiles, split):
    s = pl.program_id(0)
    n_x = 2 if split else 1
    x_refs, refs = refs[:n_x], refs[n_x:]

    def load_x():
        if split:
            is_ctx = jnp.minimum(s, n_tiles - 1) % nt == ctx_tile
            return jnp.where(is_ctx, x_refs[1][0], x_refs[0][0])
        return x_refs[0][0]

    (mod_ref, g_ref, wh_ref, wkr_ref, wt_ref, wuq_ref, wukv_ref, gcq_ref, gckv_ref,
     gqa_ref, gka_ref, gqb_ref, gkb_ref, ca_ref, la_ref, ra_ref, cb_ref, lb_ref, rb_ref,
     gate_ref, cd_ref, qa_ref, ka_ref, vxa_ref, qb_ref, kb_ref, vxb_ref,
     h0, h1, raw0, raw1) = refs
    d = D_MODEL
    tq = h0.shape[0]

    @pl.when(s == 0)
    def _warm_up():
        for ref in (h0, h1, raw0, raw1):
            ref[...] = jnp.zeros(ref.shape, ref.dtype)

    def stages(h_w, h_r, raw_w, raw_r):
        h = h_r[...]
        cq = _dot(h, wh_ref[:, CQ_OFF:CQ_OFF + B_QLORA])
        ckv = _dot(h, wh_ref[:, CKV_OFF:CKV_OFF + B_KVLORA])
        cqn = (_rms(cq) * gcq_ref[...]).astype(BF16)
        ckvn = (_rms(ckv) * gckv_ref[...]).astype(BF16)

        def passthrough(o_ref, off, width):
            for c0 in range(0, width, IN_FILL):
                o_ref[0, :, c0:c0 + IN_FILL] = _dot(
                    h, wt_ref[:, off + c0:off + c0 + IN_FILL]).astype(o_ref.dtype)

        passthrough(cd_ref, CD_OFF, CD_W)
        raw_w[:, RAW_QB:RAW_QB + B_HEADS * LANES] = _dot(cqn, wuq_ref[...])
        raw_w[:, RAW_KVB:RAW_KVB + B_HEADS * LANES] = _dot(ckvn, wukv_ref[...])
        for c0 in range(0, CQ_OFF, IN_FILL):
            raw_w[:, c0:c0 + IN_FILL] = _dot(h, wh_ref[:, c0:c0 + IN_FILL])
        raw_w[:, RAW_KR:RAW_KR + LANES] = _dot(h, wkr_ref[...])
        passthrough(gate_ref, GATE_OFF, N_BRANCH * d)

        lane = _lane_iota()
        lo = lane < A_HD
        nope = lane < B_NOPE
        rope_a = (ca_ref[...], la_ref[...], ra_ref[...])
        rope_b = (cb_ref[...], lb_ref[...], rb_ref[...])

        gq_b = gqb_ref[...] * (B_QK ** -0.5 * LOG2E)
        for hd in range(B_HEADS):
            q = raw_r[:, RAW_QB + hd * LANES:RAW_QB + (hd + 1) * LANES]
            q2 = q * q
            s_n = jnp.sum(jnp.where(nope, q2, 0.0), axis=-1, keepdims=True)
            s_r = jnp.sum(jnp.where(nope, 0.0, q2), axis=-1, keepdims=True)
            inv = jnp.where(nope, lax.rsqrt(s_n * (1.0 / B_NOPE) + EPS),
                            lax.rsqrt(s_r * (1.0 / B_ROPE) + EPS))
            qb_ref[0, :, hd * LANES:(hd + 1) * LANES] = _rope(
                q * inv * gq_b, rope_b, B_ROPE // 2).astype(BF16)

        kr = raw_r[:, RAW_KR:RAW_KR + LANES]
        krn = kr * lax.rsqrt(jnp.sum(kr * kr, axis=-1, keepdims=True) * (1.0 / B_ROPE) + EPS)
        krn = _rope(krn * jnp.where(nope, 0.0, gkb_ref[...]), rope_b, B_ROPE // 2)
        gk_nope = jnp.where(nope, gkb_ref[...], 0.0)
        for hd in range(B_HEADS):
            kv = raw_r[:, RAW_KVB + hd * LANES:RAW_KVB + (hd + 1) * LANES]
            ss = jnp.sum(jnp.where(nope, kv * kv, 0.0), axis=-1, keepdims=True)
            kn = kv * lax.rsqrt(ss * (1.0 / B_NOPE) + EPS) * gk_nope
            kb_ref[0, :, hd * LANES:(hd + 1) * LANES] = (kn + krn).astype(BF16)
            vxb_ref[0, :, hd * LANES:(hd + 1) * LANES] = jnp.where(nope, 1.0, kv).astype(BF16)

        def norm_halves(t):
            t2 = t * t
            s_lo = jnp.sum(jnp.where(lo, t2, 0.0), axis=-1, keepdims=True)
            s_hi = jnp.sum(jnp.where(lo, 0.0, t2), axis=-1, keepdims=True)
            return t * jnp.where(lo, lax.rsqrt(s_lo * (1.0 / A_HD) + EPS),
                                 lax.rsqrt(s_hi * (1.0 / A_HD) + EPS))

        gq_a = gqa_ref[...] * (A_HD ** -0.5 * LOG2E)
        for off, gain, o_ref in ((QA_OFF, gq_a, qa_ref), (KA_OFF, gka_ref[...], ka_ref)):
            for hd in range(A_HEADS):
                t = raw_r[:, off + hd * LANES:off + (hd + 1) * LANES]
                o_ref[0, :, hd * LANES:(hd + 1) * LANES] = _rope(
                    norm_halves(t) * gain, rope_a, A_HD // 2).astype(BF16)
        ones = jnp.ones((tq, A_VD), BF16)
        for hd in range(A_HEADS):
            vxa_ref[0, :, 2 * hd * A_VD:(2 * hd + 1) * A_VD] = raw_r[
                :, VA_OFF + hd * A_VD:VA_OFF + (hd + 1) * A_VD].astype(BF16)
            vxa_ref[0, :, (2 * hd + 1) * A_VD:(2 * hd + 2) * A_VD] = ones

        y = _rms(load_x()) * g_ref[...]
        h_w[...] = (y * (1.0 + mod_ref[:, d:2 * d]) + mod_ref[:, 0:d]).astype(BF16)

    pl.when(s % 2 == 0)(lambda: stages(h0, h1, raw1, raw0))
    pl.when(s % 2 == 1)(lambda: stages(h1, h0, raw0, raw1))


def _inproj(l, x_parts, mods, p, rope_a, rope_b, tq, n_lat, n_ctx):
    b, _, d = x_parts[0].shape
    t = n_lat + n_ctx
    nt = t // tq
    ctx_tile = n_lat // tq
    n_tiles = b * nt
    split = len(x_parts) == 2
    tile_a = lambda s: jnp.minimum(s, n_tiles - 1)
    tile_b = lambda s: jnp.clip(s - 1, 0, n_tiles - 1)
    tile_c = lambda s: jnp.clip(s - 2, 0, n_tiles - 1)
    tab = pl.BlockSpec((tq, LANES), lambda s: (tile_c(s) % nt, 0))
    vec = lambda n: _layer_spec(l, (1, n))
    if split:
        x_specs = [pl.BlockSpec((1, tq, d), lambda s: (tile_a(s) // nt,
                                                       jnp.minimum(tile_a(s) % nt, ctx_tile - 1), 0)),
                   pl.BlockSpec((1, tq, d), lambda s: (tile_a(s) // nt, 0, 0))]
    else:
        x_specs = [pl.BlockSpec((1, tq, d), lambda s: (tile_a(s) // nt, tile_a(s) % nt, 0))]

    def out(width, dtype, tile):
        return (pl.BlockSpec((1, tq, width), lambda s: (tile(s) // nt, tile(s) % nt, 0)),
                jax.ShapeDtypeStruct((b, t, width), dtype))

    outs = [out(N_BRANCH * d, BF16, tile_b), out(CD_W, BF16, tile_b),
            out(A_HEADS * LANES, BF16, tile_c), out(A_HEADS * LANES, BF16, tile_c),
            out(2 * A_HEADS * A_VD, BF16, tile_c), out(B_HEADS * LANES, BF16, tile_c),
            out(B_HEADS * LANES, BF16, tile_c), out(B_HEADS * LANES, BF16, tile_c)]
    return pl.pallas_call(
        functools.partial(_inproj_kernel, nt=nt, ctx_tile=ctx_tile, n_tiles=n_tiles, split=split),
        grid=(n_tiles + 2,),
        in_specs=x_specs + [
            pl.BlockSpec((None, None, 1, N_MOD * d),
                         lambda s: (l, jnp.where(tile_a(s) % nt == ctx_tile, b, tile_a(s) // nt), 0, 0)),
            vec(d),
            _layer_spec(l, (d, HEAD_W)),
            _layer_spec(l, (d, LANES)),
            _layer_spec(l, (d, TAIL_W)),
            _layer_spec(l, (B_QLORA, B_HEADS * LANES)),
            _layer_spec(l, (B_KVLORA, B_HEADS * LANES)),
            vec(B_QLORA), vec(B_KVLORA), vec(LANES), vec(LANES), vec(LANES), vec(LANES),
            tab, tab, tab, tab, tab, tab,
        ],
        out_specs=[o[0] for o in outs],
        out_shape=[o[1] for o in outs],
        scratch_shapes=[pltpu.VMEM((tq, d), BF16), pltpu.VMEM((tq, d), BF16),
                        pltpu.VMEM((tq, RAW_W), F32), pltpu.VMEM((tq, RAW_W), F32)],
        compiler_params=pltpu.CompilerParams(
            dimension_semantics=("arbitrary",), vmem_limit_bytes=VMEM_LIMIT),
        name="inproj",
    )(*x_parts, mods, p["g_norm1"], p["w_head"], p["w_kr"], p["w_tail"], p["w_uq"], p["w_ukv"],
      p["g_cq"], p["g_ckv"], p["gq_a"], p["gk_a"], p["gq_b"], p["gk_b"], *rope_a, *rope_b)


def _softmax_pv(q, k, vx, shift_max):
    s = _dot_nt(q, k)
    if shift_max:
        s = s - jnp.max(s, axis=-1, keepdims=True)
    return _dot(jnp.exp2(s).astype(BF16), vx)


def _sweep(make_tile, bounded, n_lat, t_all):
    def run(shift_max):
        tile = make_tile(shift_max)
        for s0 in range(0, t_all, ATT_SUB):
            tile(s0, 0 if s0 < n_lat else n_lat)

    pl.when(bounded)(lambda: run(False))
    pl.when(jnp.logical_not(bounded))(lambda: run(True))


def _attn_a_kernel(safe_ref, q_ref, k_ref, vx_ref, lam_ref, gsub_ref, o_ref, *, l, n_lat, lam_init):
    t_all = k_ref.shape[1]
    lo = _lane_iota() < A_HD
    la = lam_ref[...].astype(F32)
    lam = (jnp.exp(jnp.sum(la[0:1] * la[1:2], axis=-1, keepdims=True))
           - jnp.exp(jnp.sum(la[2:3] * la[3:4], axis=-1, keepdims=True)) + lam_init)
    post = gsub_ref[...] * (1.0 - lam_init)

    def make_tile(shift_max):
        def tile(r0, key_lo):
            q = q_ref[0, r0:r0 + ATT_SUB, :]
            k = k_ref[0, key_lo:t_all, :]
            vx = vx_ref[0, key_lo:t_all, :]
            zero = jnp.zeros_like(q)
            r1 = _softmax_pv(jnp.where(lo, q, zero), k, vx, shift_max)
            r2 = _softmax_pv(jnp.where(lo, zero, q), k, vx, shift_max)
            o = (r1[:, 0:A_VD] / r1[:, A_VD:A_VD + 1]
                 - lam * (r2[:, 0:A_VD] / r2[:, A_VD:A_VD + 1]))
            o_ref[0, r0:r0 + ATT_SUB, :] = (_rms(o) * post).astype(o_ref.dtype)
        return tile

    _sweep(make_tile, safe_ref[l] != 0, n_lat, t_all)


def _attn_a(l, qa, ka, vxa, p, n_lat, lam_init):
    b, t, _ = qa.shape
    head = lambda i, h: (i, 0, h)
    return pl.pallas_call(
        functools.partial(_attn_a_kernel, l=l, n_lat=n_lat, lam_init=lam_init),
        grid=(b, A_HEADS),
        in_specs=[
            pl.BlockSpec(memory_space=pltpu.SMEM),
            pl.BlockSpec((1, t, LANES), head),
            pl.BlockSpec((1, t, LANES), head),
            pl.BlockSpec((1, t, 2 * A_VD), head),
            _layer_spec(l, (4, A_HD)),
            _layer_spec(l, (1, LANES)),
        ],
        out_specs=pl.BlockSpec((1, t, LANES), head),
        out_shape=jax.ShapeDtypeStruct((b, t, BRANCH_W), BF16),
        compiler_params=pltpu.CompilerParams(
            dimension_semantics=("parallel", "parallel"), vmem_limit_bytes=VMEM_LIMIT),
        name="attn_a",
    )(p["safe_a"], qa, ka, vxa, p["lam_a"], p["g_sub_a"])


def _attn_b_kernel(safe_ref, q_ref, k_ref, vx_ref, o_ref, *, l, n_lat):
    t_all = k_ref.shape[1]
    nope = _lane_iota() < B_NOPE

    def make_tile(shift_max):
        def tile(r0, key_lo):
            outs = []
            for hh in range(2):
                sl = slice(hh * LANES, (hh + 1) * LANES)
                r = _softmax_pv(q_ref[0, r0:r0 + ATT_SUB, sl], k_ref[0, key_lo:t_all, sl],
                                vx_ref[0, key_lo:t_all, :], shift_max)[:, sl]
                outs.append(r / r[:, 0:1])
            o = jnp.where(nope, pltpu.roll(outs[0], B_VD, 1), outs[1])
            o_ref[0, r0:r0 + ATT_SUB, :] = o.astype(o_ref.dtype)
        return tile

    _sweep(make_tile, safe_ref[l] != 0, n_lat, t_all)


def _attn_b(l, qb, kb, vxb, p, n_lat):
    b, t, _ = qb.shape
    pair = lambda i, h: (i, 0, h)
    spec = pl.BlockSpec((1, t, 2 * LANES), pair)
    return pl.pallas_call(
        functools.partial(_attn_b_kernel, l=l, n_lat=n_lat),
        grid=(b, B_HEADS // 2),
        in_specs=[pl.BlockSpec(memory_space=pltpu.SMEM), spec, spec, spec],
        out_specs=pl.BlockSpec((1, t, LANES), pair),
        out_shape=jax.ShapeDtypeStruct((b, t, BRANCH_W), BF16),
        compiler_params=pltpu.CompilerParams(
            dimension_semantics=("parallel", "parallel"), vmem_limit_bytes=VMEM_LIMIT),
        name="attn_b",
    )(p["safe_b"], qb, kb, vxb)


def _mix_cd_kernel(u_ref, pb_ref, pc_ref, pxx_ref, wp_ref, sp_ref, wc_ref,
                   yc_ref, yd_ref, pad_s, *, segments):
    g = pl.program_id(1)
    zeros_halo = jnp.zeros((POOL_HALO, LANES), F32)
    wc = wc_ref[...]

    for start, length in segments:
        rows = lax.broadcasted_iota(jnp.int32, (length, 1), 0)
        u = u_ref[0, start:start + length, :].astype(F32)
        pad_s[0:POOL_HALO, :] = zeros_halo
        pad_s[POOL_HALO:POOL_HALO + length, :] = u
        pad_s[POOL_HALO + length:2 * POOL_HALO + length, :] = zeros_halo

        for gi, w in enumerate(POOL_WINDOWS):
            @pl.when(g == gi)
            def _pool(w=w):
                acc = pad_s[POOL_HALO - w // 2:POOL_HALO - w // 2 + length, :]
                for j in range(1 - w // 2, w // 2):
                    acc = acc + pad_s[POOL_HALO + j:POOL_HALO + j + length, :]
                cnt = (jnp.minimum(rows + w // 2, length) - jnp.maximum(rows - w // 2, 0)).astype(F32)
                dd = acc / cnt - u
                y = _dot(dd.astype(BF16), wp_ref[...]) * sp_ref[...]
                yc_ref[0, start:start + length, :] = y.astype(yc_ref.dtype)

        uu = (pc_ref[0, start:start + length, :].astype(F32)
              * pxx_ref[0, start:start + length, :].astype(F32))
        pad_s[POOL_HALO:POOL_HALO + length, :] = uu
        y = (pad_s[POOL_HALO - 1:POOL_HALO - 1 + length, :] * wc[0:1]
             + uu * wc[1:2]
             + pad_s[POOL_HALO + 1:POOL_HALO + 1 + length, :] * wc[2:3])
        yd_ref[0, start:start + length, :] = (
            pb_ref[0, start:start + length, :].astype(F32) * y).astype(yd_ref.dtype)


def _mix_cd(l, cd, p, n_lat):
    b, t, _ = cd.shape
    segments = ((0, n_lat), (n_lat, t - n_lat))
    n_g = len(POOL_WINDOWS)

    def col(k):
        return pl.BlockSpec((1, t, LANES), lambda i, g: (i, 0, k * n_g + g))

    out_spec = pl.BlockSpec((1, t, LANES), lambda i, g: (i, 0, g))
    return pl.pallas_call(
        functools.partial(_mix_cd_kernel, segments=segments),
        grid=(b, n_g),
        in_specs=[
            col(0), col(1), col(2), col(3),
            pl.BlockSpec((None, None, POOL_GROUP, POOL_GROUP), lambda i, g: (l, g, 0, 0)),
            pl.BlockSpec((None, 1, LANES), lambda i, g: (l, 0, g)),
            pl.BlockSpec((None, 3, LANES), lambda i, g: (l, 0, g)),
        ],
        out_specs=[out_spec, out_spec],
        out_shape=[jax.ShapeDtypeStruct((b, t, BRANCH_W), BF16)] * 2,
        scratch_shapes=[pltpu.VMEM((max(n_lat, t - n_lat) + 2 * POOL_HALO, LANES), F32)],
        compiler_params=pltpu.CompilerParams(
            dimension_semantics=("parallel", "parallel"), vmem_limit_bytes=VMEM_LIMIT),
        name="mix_cd",
    )(cd, cd, cd, cd, p["w_pool"], p["s_pool"], p["w_conv"])


def _merge_ffn_kernel(*refs, ctx_tile, split):
    ya_ref, yb_ref, yc_ref, yd_ref, gate_ref = refs[:5]
    if split:
        x_ref, xc_ref = refs[5:7]
        refs = refs[7:]
        x = jnp.where(pl.program_id(1) == ctx_tile, xc_ref[0], x_ref[0])
    else:
        x = refs[5][0]
        refs = refs[6:]
    mod_ref, wb_ref, wo_ref, g2_ref, w1_ref, w2_ref, o_ref = refs
    d = D_MODEL
    merged = None
    for n, y_ref in enumerate((ya_ref, yb_ref, yc_ref, yd_ref)):
        proj = _dot(y_ref[0], wb_ref[n])
        gate = jax.nn.sigmoid(gate_ref[0, :, n * d:(n + 1) * d].astype(F32))
        merged = gate * proj if merged is None else merged + gate * proj
    mix = _dot(merged.astype(BF16), wo_ref[...])
    x1 = x + mod_ref[:, 2 * d:3 * d] * mix

    y = _rms(x1) * g2_ref[...]
    h = (y * (1.0 + mod_ref[:, 4 * d:5 * d]) + mod_ref[:, 3 * d:4 * d]).astype(BF16)
    f = None
    for c0 in range(0, D_FF, d):
        a = jnp.maximum(_dot(h, w1_ref[:, c0:c0 + d]), 0.0)
        part = _dot((a * a).astype(BF16), w2_ref[c0:c0 + d, :])
        f = part if f is None else f + part
    o_ref[0] = x1 + mod_ref[:, 5 * d:6 * d] * f


def _merge_ffn(l, ys, gates, x_parts, mods, p, tq, n_lat, n_ctx, with_ctx):
    b, _, d = x_parts[0].shape
    t = n_lat + n_ctx
    ctx_tile = n_lat // tq
    split = len(x_parts) == 2
    row = lambda i, j: (i, j, 0)
    y_spec = pl.BlockSpec((1, tq, BRANCH_W), row)
    nq, out_rows = (t // tq, t) if with_ctx else (ctx_tile, n_lat)
    aliases = {5: 0} if (with_ctx and not split) else {}
    return pl.pallas_call(
        functools.partial(_merge_ffn_kernel, ctx_tile=ctx_tile, split=split),
        grid=(b, nq),
        in_specs=[
            y_spec, y_spec, y_spec, y_spec,
            pl.BlockSpec((1, tq, N_BRANCH * d), row),
        ] + _stream_specs(split, tq, d, ctx_tile) + [
            _mod_spec(l, b, ctx_tile),
            _layer_spec(l, (N_BRANCH, BRANCH_W, d)),
            _layer_spec(l, (d, d)),
            _layer_spec(l, (1, d)),
            _layer_spec(l, (d, D_FF)),
            _layer_spec(l, (D_FF, d)),
        ],
        out_specs=pl.BlockSpec((1, tq, d), row),
        out_shape=jax.ShapeDtypeStruct((b, out_rows, d), F32),
        input_output_aliases=aliases,
        compiler_params=pltpu.CompilerParams(
            dimension_semantics=("parallel", "parallel"), vmem_limit_bytes=VMEM_LIMIT),
        name="merge_ffn",
    )(*ys, gates, *x_parts, mods, p["w_branch"], p["w_o"], p["g_norm2"], p["w_ff1"], p["w_ff2"])


def _rope_tables(n_lat, n_ctx, rot_dim, lane_lo, period, total=LANES):
    rows = n_lat // GRID_W
    row = np.repeat(np.arange(rows, dtype=np.float64), GRID_W)
    col = np.tile(np.arange(GRID_W, dtype=np.float64), rows)
    n_freq = rot_dim // 4
    inv = ROPE_BASE ** (-np.arange(n_freq, dtype=np.float64) / n_freq)
    inv = inv.astype(np.float32).astype(np.float64)
    ang = np.concatenate([row[:, None] * inv, col[:, None] * inv], axis=-1)
    ang = ang.astype(np.float32).astype(np.float64)
    half = rot_dim // 2
    t = n_lat + n_ctx
    cos = np.ones((t, total), np.float32)
    s_left = np.zeros((t, total), np.float32)
    s_right = np.zeros((t, total), np.float32)
    starts = [lane_lo] if period == 0 else list(range(lane_lo, total, period))
    for s0 in starts:
        cos[:n_lat, s0:s0 + half] = np.cos(ang)
        cos[:n_lat, s0 + half:s0 + rot_dim] = np.cos(ang)
        s_left[:n_lat, s0:s0 + half] = -np.sin(ang)
        s_right[:n_lat, s0 + half:s0 + rot_dim] = np.sin(ang)
    return jnp.asarray(cos), jnp.asarray(s_left), jnp.asarray(s_right)


def _prep_w_in_kernel(w_ref, last_ref, head_ref, kr_ref, tail_ref):
    main_w = w_ref.shape[-1]
    head_ref[...] = w_ref[:, 0:HEAD_W].astype(BF16)
    lane = _lane_iota()
    slot = pltpu.roll(w_ref[:, HEAD_W:HEAD_W + LANES], B_NOPE, 1)
    kr_ref[...] = jnp.where((lane >= B_NOPE) & (lane < B_QK), slot, 0.0).astype(BF16)
    wide = jnp.concatenate([w_ref[:, HEAD_W:main_w], last_ref[...]], axis=1)
    tail_ref[...] = pltpu.roll(wide, wide.shape[-1] - B_ROPE, 1)[:, 0:TAIL_W].astype(BF16)


def _prep_w_in(w_in):
    depth, d, n = w_in.shape
    assert n == HEAD_W + B_ROPE + TAIL_W
    rt = 256
    main_w = (n // LANES) * LANES
    last = jnp.pad(w_in[:, :, main_w:], ((0, 0), (0, 0), (0, LANES - (n - main_w))))
    blk = lambda w: pl.BlockSpec((None, rt, w), lambda l, i: (l, i, 0))
    return pl.pallas_call(
        _prep_w_in_kernel,
        grid=(depth, d // rt),
        in_specs=[blk(main_w), blk(LANES)],
        out_specs=[blk(HEAD_W), blk(LANES), blk(TAIL_W)],
        out_shape=[jax.ShapeDtypeStruct((depth, d, w), BF16) for w in (HEAD_W, LANES, TAIL_W)],
        compiler_params=pltpu.CompilerParams(
            dimension_semantics=("parallel", "parallel"), vmem_limit_bytes=VMEM_LIMIT),
        name="prep_w_in",
    )(w_in, last)


def _score_bounds(gq_a, gk_a, gq_b, gk_b):
    amax = lambda g: jnp.max(jnp.abs(g), axis=-1)
    bound_a = A_HD * amax(gq_a) * amax(gk_a) * (A_HD ** -0.5 * LOG2E)
    nq = jnp.sqrt(B_NOPE * amax(gq_b[:, :B_NOPE]) ** 2 + B_ROPE * amax(gq_b[:, B_NOPE:]) ** 2)
    nk = jnp.sqrt(B_NOPE * amax(gk_b[:, :B_NOPE]) ** 2 + B_ROPE * amax(gk_b[:, B_NOPE:]) ** 2)
    bound_b = nq * nk * (B_QK ** -0.5 * LOG2E)
    margin = 1.05
    return ((bound_a * margin < SAFE_LOG2).astype(jnp.int32),
            (bound_b * margin < SAFE_LOG2).astype(jnp.int32))


def _pad_head_slots(v, width):
    lead = v.shape[:-1]
    v = v.reshape(lead + (-1, width))
    v = jnp.pad(v, [(0, 0)] * len(lead) + [(0, 0), (0, LANES - width)])
    return v.reshape(lead + (-1,))


def kernel(x, c, ctx, c_ctx, w_mod, b_mod, g_norm1, g_norm2, w_in, gq_a, gk_a, lam_a, g_sub_a,
           g_cq, w_uq, g_ckv, w_ukv, gq_b, gk_b, w_pool, s_pool, w_conv, w_branch, w_o,
           w_ff1, w_ff2):
    b, n_lat, d = x.shape
    n_ctx = ctx.shape[1]
    depth = w_mod.shape[0]
    tq = n_ctx
    assert d == D_MODEL and n_lat % ATT_SUB == 0 and n_ctx % ATT_SUB == 0
    assert n_lat % tq == 0 and tq % LANES == 0 and n_lat % GRID_W == 0

    rope_a = _rope_tables(n_lat, n_ctx, A_HD, 0, A_HD)
    rope_b = _rope_tables(n_lat, n_ctx, B_ROPE, B_NOPE, 0)

    mod_rows = -(-(b + 1) // 8) * 8
    cc = jnp.concatenate([c, c_ctx[None, :], jnp.zeros((mod_rows - b - 1, d), F32)], axis=0)
    mods = _modulation(cc, w_mod, b_mod).reshape(depth, mod_rows, 1, N_MOD * d)

    vec = lambda a: a[:, None, :]
    safe_a, safe_b = _score_bounds(gq_a, gk_a, gq_b, gk_b)
    w_head, w_kr, w_tail = _prep_w_in(w_in)
    p = {
        "safe_a": safe_a, "safe_b": safe_b,
        "w_head": w_head, "w_kr": w_kr, "w_tail": w_tail,
        "w_uq": _pad_head_slots(w_uq, B_QK).astype(BF16),
        "w_ukv": w_ukv.astype(BF16),
        "w_pool": w_pool.astype(BF16),
        "w_branch": w_branch.astype(BF16),
        "w_o": w_o.astype(BF16),
        "w_ff1": w_ff1.astype(BF16),
        "w_ff2": w_ff2.astype(BF16),
        "g_norm1": vec(g_norm1), "g_norm2": vec(g_norm2),
        "g_cq": vec(g_cq), "g_ckv": vec(g_ckv),
        "gq_a": vec(jnp.tile(gq_a, (1, 2))), "gk_a": vec(jnp.tile(gk_a, (1, 2))),
        "gq_b": vec(jnp.pad(gq_b, ((0, 0), (0, LANES - B_QK)))),
        "gk_b": vec(jnp.pad(gk_b, ((0, 0), (0, LANES - B_QK)))),
        "lam_a": lam_a, "g_sub_a": vec(g_sub_a),
        "s_pool": vec(s_pool), "w_conv": w_conv,
    }

    x_parts = (x, ctx)
    for l in range(depth):
        last = l == depth - 1
        lam_init = 0.8 - 0.6 * math.exp(-0.3 * l)
        gates, cd, qa, ka, vxa, qb, kb, vxb = _inproj(
            l, x_parts, mods, p, rope_a, rope_b, tq, n_lat, n_ctx)
        ya = _attn_a(l, qa, ka, vxa, p, n_lat, lam_init)
        yb = _attn_b(l, qb, kb, vxb, p, n_lat)
        yc, yd = _mix_cd(l, cd, p, n_lat)
        x_parts = (_merge_ffn(l, (ya, yb, yc, yd), gates, x_parts, mods, p, tq, n_lat, n_ctx,
                              not last),)
    return x_parts[0]
```

```python
import functools
import math

import numpy as np
import jax
import jax.numpy as jnp
from jax import lax
from jax.experimental import pallas as pl
from jax.experimental.pallas import tpu as pltpu

F32 = jnp.float32
BF16 = jnp.bfloat16

D_MODEL = 1024
GRID_W = 64
ROPE_BASE = 10000.0
EPS = 1e-6
LOG2E = math.log2(math.e)

A_HEADS = 4
A_HD = 64
A_VD = 128
B_HEADS = 8
B_NOPE = 64
B_ROPE = 32
B_QK = B_NOPE + B_ROPE
B_VD = 64
B_QLORA = 384
B_KVLORA = 256
POOL_WINDOWS = (2, 4, 8, 16)
POOL_GROUP = 128
POOL_HALO = 8
BRANCH_W = 512
N_BRANCH = 4
D_FF = 4 * D_MODEL
N_MOD = 6

LANES = 128

QA_OFF = 0
KA_OFF = QA_OFF + 512
VA_OFF = KA_OFF + 512
CQ_OFF = VA_OFF + 512
CKV_OFF = CQ_OFF + B_QLORA
HEAD_W = CKV_OFF + B_KVLORA
CD_OFF = 0
CD_W = 4 * 512
GATE_OFF = CD_OFF + CD_W
TAIL_W = GATE_OFF + N_BRANCH * D_MODEL
IN_FILL = 256
RAW_QB = CQ_OFF
RAW_KVB = RAW_QB + B_HEADS * LANES
RAW_KR = RAW_KVB + B_HEADS * LANES
RAW_W = RAW_KR + LANES

ATT_SUB = 128
SAFE_LOG2 = 40.0

VMEM_LIMIT = 56 * 1024 * 1024


def _dot(a, b):
    return jnp.dot(a, b, preferred_element_type=F32)


def _dot_nt(a, b):
    return lax.dot_general(a, b, (((1,), (1,)), ((), ())), preferred_element_type=F32)


def _lane_iota(n=LANES):
    return lax.broadcasted_iota(jnp.int32, (1, n), 1)


def _rms(x):
    return x * lax.rsqrt(jnp.mean(x * x, axis=-1, keepdims=True) + EPS)


def _rope(t, tabs, half):
    cos, s_left, s_right = tabs
    n = t.shape[-1]
    return (t * cos + pltpu.roll(t, n - half, 1) * s_left
            + pltpu.roll(t, half, 1) * s_right)


def _layer_spec(l, shape):
    nd = len(shape)
    return pl.BlockSpec((None,) + tuple(shape), lambda *_: (l,) + (0,) * nd,
                        pipeline_mode=pl.Buffered(1))


def _mod_kernel(c_ref, w_ref, b_ref, o_ref):
    c = c_ref[...]
    h = (c * jax.nn.sigmoid(c)).astype(BF16)
    o_ref[...] = _dot(h, w_ref[...].astype(BF16)) + b_ref[...]


def _modulation(cc, w_mod, b_mod):
    depth, d, n = w_mod.shape
    rows = cc.shape[0]
    tn = 1536
    return pl.pallas_call(
        _mod_kernel,
        grid=(depth, n // tn),
        in_specs=[
            pl.BlockSpec((rows, d), lambda l, j: (0, 0)),
            pl.BlockSpec((None, d, tn), lambda l, j: (l, 0, j)),
            pl.BlockSpec((None, 1, tn), lambda l, j: (l, 0, j)),
        ],
        out_specs=pl.BlockSpec((None, rows, tn), lambda l, j: (l, 0, j)),
        out_shape=jax.ShapeDtypeStruct((depth, rows, n), F32),
        compiler_params=pltpu.CompilerParams(
            dimension_semantics=("parallel", "parallel"), vmem_limit_bytes=VMEM_LIMIT),
        name="modulation",
    )(cc, w_mod, b_mod.reshape(depth, 1, n))


def _inproj_kernel(*refs, nt, ctx_tile, n_tiles, split):
    s = pl.program_id(0)
    n_x = 2 if split else 1
    x_refs, refs = refs[:n_x], refs[n_x:]

    def load_x():
        if split:
            is_ctx = jnp.minimum(s, n_tiles - 1) % nt == ctx_tile
            return jnp.where(is_ctx, x_refs[1][0], x_refs[0][0])
        return x_refs[0][0]

    (mod_ref, g_ref, wh_ref, wkr_ref, wt_ref, wuq_ref, wukv_ref, gcq_ref, gckv_ref,
     gqa_ref, gka_ref, gqb_ref, gkb_ref, ca_ref, la_ref, ra_ref, cb_ref, lb_ref, rb_ref,
     gate_ref, cd_ref, qa_ref, ka_ref, vxa_ref, qb_ref, kb_ref, vxb_ref,
     h0, h1, raw0, raw1) = refs
    d = D_MODEL
    tq = h0.shape[0]

    @pl.when(s == 0)
    def _warm_up():
        for ref in (h0, h1, raw0, raw1):
            ref[...] = jnp.zeros(ref.shape, ref.dtype)

    def stages(h_w, h_r, raw_w, raw_r):
        h = h_r[...]
        cq = _dot(h, wh_ref[:, CQ_OFF:CQ_OFF + B_QLORA])
        ckv = _dot(h, wh_ref[:, CKV_OFF:CKV_OFF + B_KVLORA])
        cqn = (_rms(cq) * gcq_ref[...]).astype(BF16)
        ckvn = (_rms(ckv) * gckv_ref[...]).astype(BF16)

        def passthrough(o_ref, off, width):
            for c0 in range(0, width, IN_FILL):
                o_ref[0, :, c0:c0 + IN_FILL] = _dot(
                    h, wt_ref[:, off + c0:off + c0 + IN_FILL]).astype(o_ref.dtype)

        passthrough(cd_ref, CD_OFF, CD_W)
        raw_w[:, RAW_QB:RAW_QB + B_HEADS * LANES] = _dot(cqn, wuq_ref[...])
        raw_w[:, RAW_KVB:RAW_KVB + B_HEADS * LANES] = _dot(ckvn, wukv_ref[...])
        for c0 in range(0, CQ_OFF, IN_FILL):
            raw_w[:, c0:c0 + IN_FILL] = _dot(h, wh_ref[:, c0:c0 + IN_FILL])
        raw_w[:, RAW_KR:RAW_KR + LANES] = _dot(h, wkr_ref[...])
        passthrough(gate_ref, GATE_OFF, N_BRANCH * d)

        lane = _lane_iota()
        lo = lane < A_HD
        nope = lane < B_NOPE
        rope_a = (ca_ref[...], la_ref[...], ra_ref[...])
        rope_b = (cb_ref[...], lb_ref[...], rb_ref[...])

        gq_b = gqb_ref[...] * (B_QK ** -0.5 * LOG2E)
        for hd in range(B_HEADS):
            q = raw_r[:, RAW_QB + hd * LANES:RAW_QB + (hd + 1) * LANES]
            q2 = q * q
            s_n = jnp.sum(jnp.where(nope, q2, 0.0), axis=-1, keepdims=True)
            s_r = jnp.sum(jnp.where(nope, 0.0, q2), axis=-1, keepdims=True)
            inv = jnp.where(nope, lax.rsqrt(s_n * (1.0 / B_NOPE) + EPS),
                            lax.rsqrt(s_r * (1.0 / B_ROPE) + EPS))
            qb_ref[0, :, hd * LANES:(hd + 1) * LANES] = _rope(
                q * inv * gq_b, rope_b, B_ROPE // 2).astype(BF16)

        kr = raw_r[:, RAW_KR:RAW_KR + LANES]
        krn = kr * lax.rsqrt(jnp.sum(kr * kr, axis=-1, keepdims=True) * (1.0 / B_ROPE) + EPS)
        krn = _rope(krn * jnp.where(nope, 0.0, gkb_ref[...]), rope_b, B_ROPE // 2)
        gk_nope = jnp.where(nope, gkb_ref[...], 0.0)
        for hd in range(B_HEADS):
            kv = raw_r[:, RAW_KVB + hd * LANES:RAW_KVB + (hd + 1) * LANES]
            ss = jnp.sum(jnp.where(nope, kv * kv, 0.0), axis=-1, keepdims=True)
            kn = kv * lax.rsqrt(ss * (1.0 / B_NOPE) + EPS) * gk_nope
            kb_ref[0, :, hd * LANES:(hd + 1) * LANES] = (kn + krn).astype(BF16)
            vxb_ref[0, :, hd * LANES:(hd + 1) * LANES] = jnp.where(nope, 1.0, kv).astype(BF16)

        def norm_halves(t):
            t2 = t * t
            s_lo = jnp.sum(jnp.where(lo, t2, 0.0), axis=-1, keepdims=True)
            s_hi = jnp.sum(jnp.where(lo, 0.0, t2), axis=-1, keepdims=True)
            return t * jnp.where(lo, lax.rsqrt(s_lo * (1.0 / A_HD) + EPS),
                                 lax.rsqrt(s_hi * (1.0 / A_HD) + EPS))

        gq_a = gqa_ref[...] * (A_HD ** -0.5 * LOG2E)
        for off, gain, o_ref in ((QA_OFF, gq_a, qa_ref), (KA_OFF, gka_ref[...], ka_ref)):
            for hd in range(A_HEADS):
                t = raw_r[:, off + hd * LANES:off + (hd + 1) * LANES]
                o_ref[0, :, hd * LANES:(hd + 1) * LANES] = _rope(
                    norm_halves(t) * gain, rope_a, A_HD // 2).astype(BF16)
        ones = jnp.ones((tq, A_VD), BF16)
        for hd in range(A_HEADS):
            vxa_ref[0, :, 2 * hd * A_VD:(2 * hd + 1) * A_VD] = raw_r[
                :, VA_OFF + hd * A_VD:VA_OFF + (hd + 1) * A_VD].astype(BF16)
            vxa_ref[0, :, (2 * hd + 1) * A_VD:(2 * hd + 2) * A_VD] = ones

        y = _rms(load_x()) * g_ref[...]
        h_w[...] = (y * (1.0 + mod_ref[:, d:2 * d]) + mod_ref[:, 0:d]).astype(BF16)

    pl.when(s % 2 == 0)(lambda: stages(h0, h1, raw1, raw0))
    pl.when(s % 2 == 1)(lambda: stages(h1, h0, raw0, raw1))


def _inproj(l, x_parts, mods, p, rope_a, rope_b, tq, n_lat, n_ctx):
    b, _, d = x_parts[0].shape
    t = n_lat + n_ctx
    nt = t // tq
    ctx_tile = n_lat // tq
    n_tiles = b * nt
    split = len(x_parts) == 2
    tile_a = lambda s: jnp.minimum(s, n_tiles - 1)
    tile_b = lambda s: jnp.clip(s - 1, 0, n_tiles - 1)
    tile_c = lambda s: jnp.clip(s - 2, 0, n_tiles - 1)
    tab = pl.BlockSpec((tq, LANES), lambda s: (tile_c(s) % nt, 0))
    vec = lambda n: _layer_spec(l, (1, n))
    if split:
        x_specs = [pl.BlockSpec((1, tq, d), lambda s: (tile_a(s) // nt,
                                                       jnp.minimum(tile_a(s) % nt, ctx_tile - 1), 0)),
                   pl.BlockSpec((1, tq, d), lambda s: (tile_a(s) // nt, 0, 0))]
    else:
        x_specs = [pl.BlockSpec((1, tq, d), lambda s: (tile_a(s) // nt, tile_a(s) % nt, 0))]

    def out(width, dtype, tile):
        return (pl.BlockSpec((1, tq, width), lambda s: (tile(s) // nt, tile(s) % nt, 0)),
                jax.ShapeDtypeStruct((b, t, width), dtype))

    outs = [out(N_BRANCH * d, BF16, tile_b), out(CD_W, BF16, tile_b),
            out(A_HEADS * LANES, BF16, tile_c), out(A_HEADS * LANES, BF16, tile_c),
            out(2 * A_HEADS * A_VD, BF16, tile_c), out(B_HEADS * LANES, BF16, tile_c),
            out(B_HEADS * LANES, BF16, tile_c), out(B_HEADS * LANES, BF16, tile_c)]
    return pl.pallas_call(
        functools.partial(_inproj_kernel, nt=nt, ctx_tile=ctx_tile, n_tiles=n_tiles, split=split),
        grid=(n_tiles + 2,),
        in_specs=x_specs + [
            pl.BlockSpec((None, None, 1, N_MOD * d),
                         lambda s: (l, jnp.where(tile_a(s) % nt == ctx_tile, b, tile_a(s) // nt), 0, 0)),
            vec(d),
            _layer_spec(l, (d, HEAD_W)),
            _layer_spec(l, (d, LANES)),
            _layer_spec(l, (d, TAIL_W)),
            _layer_spec(l, (B_QLORA, B_HEADS * LANES)),
            _layer_spec(l, (B_KVLORA, B_HEADS * LANES)),
            vec(B_QLORA), vec(B_KVLORA), vec(LANES), vec(LANES), vec(LANES), vec(LANES),
            tab, tab, tab, tab, tab, tab,
        ],
        out_specs=[o[0] for o in outs],
        out_shape=[o[1] for o in outs],
        scratch_shapes=[pltpu.VMEM((tq, d), BF16), pltpu.VMEM((tq, d), BF16),
                        pltpu.VMEM((tq, RAW_W), F32), pltpu.VMEM((tq, RAW_W), F32)],
        compiler_params=pltpu.CompilerParams(
            dimension_semantics=("arbitrary",), vmem_limit_bytes=VMEM_LIMIT),
        name="inproj",
    )(*x_parts, mods, p["g_norm1"], p["w_head"], p["w_kr"], p["w_tail"], p["w_uq"], p["w_ukv"],
      p["g_cq"], p["g_ckv"], p["gq_a"], p["gk_a"], p["gq_b"], p["gk_b"], *rope_a, *rope_b)


def _softmax_pv(q, k, vx, shift_max):
    s = _dot_nt(q, k)
    if shift_max:
        s = s - jnp.max(s, axis=-1, keepdims=True)
    return _dot(jnp.exp2(s).astype(BF16), vx)


def _sweep(make_tile, bounded, n_lat, t_all):
    def run(shift_max):
        tile = make_tile(shift_max)
        for s0 in range(0, t_all, ATT_SUB):
            tile(s0, 0 if s0 < n_lat else n_lat)

    pl.when(bounded)(lambda: run(False))
    pl.when(jnp.logical_not(bounded))(lambda: run(True))


def _attn_a_kernel(safe_ref, q_ref, k_ref, vx_ref, lam_ref, gsub_ref, o_ref, *, l, n_lat, lam_init):
    t_all = k_ref.shape[1]
    lo = _lane_iota() < A_HD
    la = lam_ref[...].astype(F32)
    lam = (jnp.exp(jnp.sum(la[0:1] * la[1:2], axis=-1, keepdims=True))
           - jnp.exp(jnp.sum(la[2:3] * la[3:4], axis=-1, keepdims=True)) + lam_init)
    post = gsub_ref[...] * (1.0 - lam_init)

    def make_tile(shift_max):
        def tile(r0, key_lo):
            q = q_ref[0, r0:r0 + ATT_SUB, :]
            k = k_ref[0, key_lo:t_all, :]
            vx = vx_ref[0, key_lo:t_all, :]
            zero = jnp.zeros_like(q)
            r1 = _softmax_pv(jnp.where(lo, q, zero), k, vx, shift_max)
            r2 = _softmax_pv(jnp.where(lo, zero, q), k, vx, shift_max)
            o = (r1[:, 0:A_VD] / r1[:, A_VD:A_VD + 1]
                 - lam * (r2[:, 0:A_VD] / r2[:, A_VD:A_VD + 1]))
            o_ref[0, r0:r0 + ATT_SUB, :] = (_rms(o) * post).astype(o_ref.dtype)
        return tile

    _sweep(make_tile, safe_ref[l] != 0, n_lat, t_all)


def _attn_a(l, qa, ka, vxa, p, n_lat, lam_init):
    b, t, _ = qa.shape
    head = lambda i, h: (i, 0, h)
    return pl.pallas_call(
        functools.partial(_attn_a_kernel, l=l, n_lat=n_lat, lam_init=lam_init),
        grid=(b, A_HEADS),
        in_specs=[
            pl.BlockSpec(memory_space=pltpu.SMEM),
            pl.BlockSpec((1, t, LANES), head),
            pl.BlockSpec((1, t, LANES), head),
            pl.BlockSpec((1, t, 2 * A_VD), head),
            _layer_spec(l, (4, A_HD)),
            _layer_spec(l, (1, LANES)),
        ],
        out_specs=pl.BlockSpec((1, t, LANES), head),
        out_shape=jax.ShapeDtypeStruct((b, t, BRANCH_W), BF16),
        compiler_params=pltpu.CompilerParams(
            dimension_semantics=("parallel", "parallel"), vmem_limit_bytes=VMEM_LIMIT),
        name="attn_a",
    )(p["safe_a"], qa, ka, vxa, p["lam_a"], p["g_sub_a"])


def _attn_b_kernel(safe_ref, q_ref, k_ref, vx_ref, o_ref, *, l, n_lat):
    t_all = k_ref.shape[1]
    nope = _lane_iota() < B_NOPE

    def make_tile(shift_max):
        def tile(r0, key_lo):
            outs = []
            for hh in range(2):
                sl = slice(hh * LANES, (hh + 1) * LANES)
                r = _softmax_pv(q_ref[0, r0:r0 + ATT_SUB, sl], k_ref[0, key_lo:t_all, sl],
                                vx_ref[0, key_lo:t_all, :], shift_max)[:, sl]
                outs.append(r / r[:, 0:1])
            o = jnp.where(nope, pltpu.roll(outs[0], B_VD, 1), outs[1])
            o_ref[0, r0:r0 + ATT_SUB, :] = o.astype(o_ref.dtype)
        return tile

    _sweep(make_tile, safe_ref[l] != 0, n_lat, t_all)


def _attn_b(l, qb, kb, vxb, p, n_lat):
    b, t, _ = qb.shape
    pair = lambda i, h: (i, 0, h)
    spec = pl.BlockSpec((1, t, 2 * LANES), pair)
    return pl.pallas_call(
        functools.partial(_attn_b_kernel, l=l, n_lat=n_lat),
        grid=(b, B_HEADS // 2),
        in_specs=[pl.BlockSpec(memory_space=pltpu.SMEM), spec, spec, spec],
        out_specs=pl.BlockSpec((1, t, LANES), pair),
        out_shape=jax.ShapeDtypeStruct((b, t, BRANCH_W), BF16),
        compiler_params=pltpu.CompilerParams(
            dimension_semantics=("parallel", "parallel"), vmem_limit_bytes=VMEM_LIMIT),
        name="attn_b",
    )(p["safe_b"], qb, kb, vxb)


def _mixers(cd_prev, cd_cur, cd_next, wc_ref, dd_w, yd_w, pads, j, nq, ctx_tile, n_lat, n_ctx, one):
    tq = cd_cur.shape[1]
    hb = cd_prev.shape[1]
    seg_first = (j == 0) | (j == ctx_tile)
    seg_last = (j == ctx_tile - 1) | (j == nq - 1)
    keep_l = jnp.where(seg_first, 0.0, 1.0).astype(F32) * one
    keep_r = jnp.where(seg_last, 0.0, 1.0).astype(F32) * one
    in_ctx = j >= ctx_tile
    seg_len = jnp.where(in_ctx, n_ctx, n_lat)
    row0 = jnp.where(in_ctx, j - ctx_tile, j) * tq
    rows = lax.broadcasted_iota(jnp.int32, (tq, 1), 0) + row0

    def padded(part, pad):
        for c0 in range(0, BRANCH_W, LANES):
            src = slice(part * BRANCH_W + c0, part * BRANCH_W + c0 + LANES)
            dst = slice(c0, c0 + LANES)
            pad[0:POOL_HALO, dst] = (cd_prev[0, hb - POOL_HALO:hb, src].astype(F32)
                                     * keep_l[:, dst])
            pad[POOL_HALO:POOL_HALO + tq, dst] = cd_cur[0, :, src].astype(F32) * one[:, dst]
            pad[POOL_HALO + tq:2 * POOL_HALO + tq, dst] = (
                cd_next[0, 0:POOL_HALO, src].astype(F32) * keep_r[:, dst])

    pad_u, pad_c, pad_x = pads
    padded(0, pad_u)
    padded(2, pad_c)
    padded(3, pad_x)

    for gi, w in enumerate(POOL_WINDOWS):
        cols = slice(gi * POOL_GROUP, (gi + 1) * POOL_GROUP)
        acc = pad_u[POOL_HALO - w // 2:POOL_HALO - w // 2 + tq, cols]
        for k in range(1 - w // 2, w // 2):
            acc = acc + pad_u[POOL_HALO + k:POOL_HALO + k + tq, cols]
        cnt = (jnp.minimum(rows + w // 2, seg_len) - jnp.maximum(rows - w // 2, 0)).astype(F32)
        dd_w[:, cols] = (acc / cnt - pad_u[POOL_HALO:POOL_HALO + tq, cols]).astype(dd_w.dtype)

    wc = wc_ref[...]

    for c0 in range(0, BRANCH_W, LANES):
        cols = slice(c0, c0 + LANES)

        def uu(k):
            return (pad_c[POOL_HALO + k:POOL_HALO + k + tq, cols]
                    * pad_x[POOL_HALO + k:POOL_HALO + k + tq, cols])

        y = uu(-1) * wc[0:1, cols] + uu(0) * wc[1:2, cols] + uu(1) * wc[2:3, cols]
        yd_w[:, cols] = (cd_cur[0, :, BRANCH_W + c0:BRANCH_W + c0 + LANES].astype(F32)
                         * y).astype(yd_w.dtype)


def _merge_ffn_kernel(*refs, nq, ctx_tile, n_tiles, n_lat, n_ctx, split):
    s = pl.program_id(0)
    ya_ref, yb_ref, gate_ref, cd_prev, cd_cur, cd_next = refs[:6]
    n_x = 2 if split else 1
    x_refs, refs = refs[6:6 + n_x], refs[6 + n_x:]
    (mod_ref, zero_ref, wp_ref, sp_ref, wc_ref, wb_ref, wo_ref, g2_ref, w1_ref, w2_ref, o_ref,
     dd0, dd1, yd0, yd1, pad_u, pad_c, pad_x) = refs
    d = D_MODEL
    j_mix = jnp.minimum(s, n_tiles - 1) % nq
    j_main = jnp.maximum(s - 1, 0) % nq

    @pl.when(s == 0)
    def _warm_up():
        for ref in (dd0, dd1, yd0, yd1):
            ref[...] = jnp.zeros(ref.shape, ref.dtype)

    def stages(dd_w, yd_w, dd_r, yd_r):
        if split:
            x = jnp.where(j_main == ctx_tile, x_refs[1][0], x_refs[0][0])
        else:
            x = x_refs[0][0]
        yc = jnp.concatenate(
            [_dot(dd_r[:, gi * POOL_GROUP:(gi + 1) * POOL_GROUP], wp_ref[gi])
             for gi in range(len(POOL_WINDOWS))], axis=-1) * sp_ref[...]
        merged = None
        for n, y in enumerate((ya_ref[0], yb_ref[0], yc.astype(BF16), yd_r[...])):
            proj = _dot(y, wb_ref[n])
            gate = jax.nn.sigmoid(gate_ref[0, :, n * d:(n + 1) * d].astype(F32))
            merged = gate * proj if merged is None else merged + gate * proj
        mix = _dot(merged.astype(BF16), wo_ref[...])
        x1 = x + mod_ref[:, 2 * d:3 * d] * mix

        y = _rms(x1) * g2_ref[...]
        h = (y * (1.0 + mod_ref[:, 4 * d:5 * d]) + mod_ref[:, 3 * d:4 * d]).astype(BF16)
        f = None
        one = None
        for c0 in range(0, D_FF, d):
            a = jnp.maximum(_dot(h, w1_ref[:, c0:c0 + d]), 0.0)
            if one is None:
                bits = pltpu.bitcast(a[0:8, 0:BRANCH_W], jnp.int32) & zero_ref[...]
                one = 1.0 + pltpu.bitcast(bits, F32)[0:1, :]
            part = _dot((a * a).astype(BF16), w2_ref[c0:c0 + d, :])
            f = part if f is None else f + part
        o_ref[0] = x1 + mod_ref[:, 5 * d:6 * d] * f

        _mixers(cd_prev, cd_cur, cd_next, wc_ref, dd_w, yd_w, (pad_u, pad_c, pad_x),
                j_mix, nq, ctx_tile, n_lat, n_ctx, one)

    pl.when(s % 2 == 0)(lambda: stages(dd0, yd0, dd1, yd1))
    pl.when(s % 2 == 1)(lambda: stages(dd1, yd1, dd0, yd0))


def _merge_ffn(l, ya, yb, gates, cd, x_parts, mods, p, tq, n_lat, n_ctx, with_ctx):
    b, _, d = x_parts[0].shape
    t = n_lat + n_ctx
    ctx_tile = n_lat // tq
    split = len(x_parts) == 2
    nq, out_rows = (t // tq, t) if with_ctx else (ctx_tile, n_lat)
    n_tiles = b * nq
    hb = 16
    t_mix = lambda s: jnp.minimum(s, n_tiles - 1)
    t_main = lambda s: jnp.maximum(s - 1, 0)
    row_main = lambda s: (t_main(s) // nq, t_main(s) % nq, 0)
    y_spec = pl.BlockSpec((1, tq, BRANCH_W), row_main)
    if split:
        x_specs = [pl.BlockSpec((1, tq, d), lambda s: (t_main(s) // nq,
                                                       jnp.minimum(t_main(s) % nq, ctx_tile - 1), 0)),
                   pl.BlockSpec((1, tq, d), lambda s: (t_main(s) // nq, 0, 0))]
    else:
        x_specs = [pl.BlockSpec((1, tq, d), row_main)]
    cd_specs = [
        pl.BlockSpec((1, hb, CD_W), lambda s: (t_mix(s) // nq,
                                               jnp.maximum((t_mix(s) % nq) * (tq // hb) - 1, 0), 0)),
        pl.BlockSpec((1, tq, CD_W), lambda s: (t_mix(s) // nq, t_mix(s) % nq, 0)),
        pl.BlockSpec((1, hb, CD_W), lambda s: (t_mix(s) // nq,
                                               jnp.minimum((t_mix(s) % nq + 1) * (tq // hb),
                                                           t // hb - 1), 0)),
    ]
    aliases = {6: 0} if (with_ctx and not split) else {}
    return pl.pallas_call(
        functools.partial(_merge_ffn_kernel, nq=nq, ctx_tile=ctx_tile, n_tiles=n_tiles,
                          n_lat=n_lat, n_ctx=n_ctx, split=split),
        grid=(n_tiles + 1,),
        in_specs=[y_spec, y_spec, pl.BlockSpec((1, tq, N_BRANCH * d), row_main)] + cd_specs + x_specs + [
            pl.BlockSpec((None, None, 1, N_MOD * d),
                         lambda s: (l, jnp.where(t_main(s) % nq == ctx_tile, b, t_main(s) // nq), 0, 0)),
            pl.BlockSpec((8, BRANCH_W), lambda s: (0, 0)),
            _layer_spec(l, (len(POOL_WINDOWS), POOL_GROUP, POOL_GROUP)),
            _layer_spec(l, (1, BRANCH_W)),
            _layer_spec(l, (3, BRANCH_W)),
            _layer_spec(l, (N_BRANCH, BRANCH_W, d)),
            _layer_spec(l, (d, d)),
            _layer_spec(l, (1, d)),
            _layer_spec(l, (d, D_FF)),
            _layer_spec(l, (D_FF, d)),
        ],
        out_specs=pl.BlockSpec((1, tq, d), row_main),
        out_shape=jax.ShapeDtypeStruct((b, out_rows, d), F32),
        scratch_shapes=[pltpu.VMEM((tq, BRANCH_W), BF16)] * 4
        + [pltpu.VMEM((tq + 2 * POOL_HALO, BRANCH_W), F32)] * 3,
        input_output_aliases=aliases,
        compiler_params=pltpu.CompilerParams(
            dimension_semantics=("arbitrary",), vmem_limit_bytes=VMEM_LIMIT),
        name="merge_ffn",
    )(ya, yb, gates, cd, cd, cd, *x_parts, mods, jnp.zeros((8, BRANCH_W), jnp.int32),
      p["w_pool"], p["s_pool"], p["w_conv"],
      p["w_branch"], p["w_o"], p["g_norm2"], p["w_ff1"], p["w_ff2"])


def _rope_tables(n_lat, n_ctx, rot_dim, lane_lo, period, total=LANES):
    rows = n_lat // GRID_W
    row = np.repeat(np.arange(rows, dtype=np.float64), GRID_W)
    col = np.tile(np.arange(GRID_W, dtype=np.float64), rows)
    n_freq = rot_dim // 4
    inv = ROPE_BASE ** (-np.arange(n_freq, dtype=np.float64) / n_freq)
    inv = inv.astype(np.float32).astype(np.float64)
    ang = np.concatenate([row[:, None] * inv, col[:, None] * inv], axis=-1)
    ang = ang.astype(np.float32).astype(np.float64)
    half = rot_dim // 2
    t = n_lat + n_ctx
    cos = np.ones((t, total), np.float32)
    s_left = np.zeros((t, total), np.float32)
    s_right = np.zeros((t, total), np.float32)
    starts = [lane_lo] if period == 0 else list(range(lane_lo, total, period))
    for s0 in starts:
        cos[:n_lat, s0:s0 + half] = np.cos(ang)
        cos[:n_lat, s0 + half:s0 + rot_dim] = np.cos(ang)
        s_left[:n_lat, s0:s0 + half] = -np.sin(ang)
        s_right[:n_lat, s0 + half:s0 + rot_dim] = np.sin(ang)
    return jnp.asarray(cos), jnp.asarray(s_left), jnp.asarray(s_right)


def _score_bounds(gq_a, gk_a, gq_b, gk_b):
    amax = lambda g: jnp.max(jnp.abs(g), axis=-1)
    bound_a = A_HD * amax(gq_a) * amax(gk_a) * (A_HD ** -0.5 * LOG2E)
    nq = jnp.sqrt(B_NOPE * amax(gq_b[:, :B_NOPE]) ** 2 + B_ROPE * amax(gq_b[:, B_NOPE:]) ** 2)
    nk = jnp.sqrt(B_NOPE * amax(gk_b[:, :B_NOPE]) ** 2 + B_ROPE * amax(gk_b[:, B_NOPE:]) ** 2)
    bound_b = nq * nk * (B_QK ** -0.5 * LOG2E)
    margin = 1.05
    return ((bound_a * margin < SAFE_LOG2).astype(jnp.int32),
            (bound_b * margin < SAFE_LOG2).astype(jnp.int32))


def _pad_head_slots(v, width):
    lead = v.shape[:-1]
    v = v.reshape(lead + (-1, width))
    v = jnp.pad(v, [(0, 0)] * len(lead) + [(0, 0), (0, LANES - width)])
    return v.reshape(lead + (-1,))


def kernel(x, c, ctx, c_ctx, w_mod, b_mod, g_norm1, g_norm2, w_in, gq_a, gk_a, lam_a, g_sub_a,
           g_cq, w_uq, g_ckv, w_ukv, gq_b, gk_b, w_pool, s_pool, w_conv, w_branch, w_o,
           w_ff1, w_ff2):
    b, n_lat, d = x.shape
    n_ctx = ctx.shape[1]
    depth = w_mod.shape[0]
    tq = n_ctx
    assert d == D_MODEL and n_lat % ATT_SUB == 0 and n_ctx % ATT_SUB == 0
    assert n_lat % tq == 0 and tq % LANES == 0 and n_lat % GRID_W == 0

    rope_a = _rope_tables(n_lat, n_ctx, A_HD, 0, A_HD)
    rope_b = _rope_tables(n_lat, n_ctx, B_ROPE, B_NOPE, 0)

    mod_rows = -(-(b + 1) // 8) * 8
    cc = jnp.concatenate([c, c_ctx[None, :], jnp.zeros((mod_rows - b - 1, d), F32)], axis=0)
    mods = _modulation(cc, w_mod, b_mod).reshape(depth, mod_rows, 1, N_MOD * d)

    vec = lambda a: a[:, None, :]
    safe_a, safe_b = _score_bounds(gq_a, gk_a, gq_b, gk_b)
    p = {
        "safe_a": safe_a, "safe_b": safe_b,
        "w_head": w_in[:, :, :HEAD_W].astype(BF16),
        "w_kr": jnp.pad(w_in[:, :, HEAD_W:HEAD_W + B_ROPE],
                        ((0, 0), (0, 0), (B_NOPE, LANES - B_NOPE - B_ROPE))).astype(BF16),
        "w_tail": w_in[:, :, HEAD_W + B_ROPE:].astype(BF16),
        "w_uq": _pad_head_slots(w_uq, B_QK).astype(BF16),
        "w_ukv": w_ukv.astype(BF16),
        "w_pool": w_pool.astype(BF16),
        "w_branch": w_branch.astype(BF16),
        "w_o": w_o.astype(BF16),
        "w_ff1": w_ff1.astype(BF16),
        "w_ff2": w_ff2.astype(BF16),
        "g_norm1": vec(g_norm1), "g_norm2": vec(g_norm2),
        "g_cq": vec(g_cq), "g_ckv": vec(g_ckv),
        "gq_a": vec(jnp.tile(gq_a, (1, 2))), "gk_a": vec(jnp.tile(gk_a, (1, 2))),
        "gq_b": vec(jnp.pad(gq_b, ((0, 0), (0, LANES - B_QK)))),
        "gk_b": vec(jnp.pad(gk_b, ((0, 0), (0, LANES - B_QK)))),
        "lam_a": lam_a, "g_sub_a": vec(g_sub_a),
        "s_pool": vec(s_pool), "w_conv": w_conv,
    }

    x_parts = (x, ctx)
    for l in range(depth):
        last = l == depth - 1
        lam_init = 0.8 - 0.6 * math.exp(-0.3 * l)
        gates, cd, qa, ka, vxa, qb, kb, vxb = _inproj(
            l, x_parts, mods, p, rope_a, rope_b, tq, n_lat, n_ctx)
        ya = _attn_a(l, qa, ka, vxa, p, n_lat, lam_init)
        yb = _attn_b(l, qb, kb, vxb, p, n_lat)
        x_parts = (_merge_ffn(l, ya, yb, gates, cd, x_parts, mods, p, tq, n_lat, n_ctx,
                              not last),)
    return x_parts[0]
```

```python
import functools
import math

import numpy as np
import jax
import jax.numpy as jnp
from jax import lax
from jax.experimental import pallas as pl
from jax.experimental.pallas import tpu as pltpu

F32 = jnp.float32
BF16 = jnp.bfloat16

D_MODEL = 1024
GRID_W = 64
ROPE_BASE = 10000.0
EPS = 1e-6
LOG2E = math.log2(math.e)

A_HEADS = 4
A_HD = 64
A_VD = 128
B_HEADS = 8
B_NOPE = 64
B_ROPE = 32
B_QK = B_NOPE + B_ROPE
B_VD = 64
B_QLORA = 384
B_KVLORA = 256
POOL_WINDOWS = (2, 4, 8, 16)
POOL_GROUP = 128
POOL_HALO = 8
BRANCH_W = 512
N_BRANCH = 4
D_FF = 4 * D_MODEL
N_MOD = 6

LANES = 128

QA_OFF = 0
KA_OFF = QA_OFF + 512
VA_OFF = KA_OFF + 512
CQ_OFF = VA_OFF + 512
CKV_OFF = CQ_OFF + B_QLORA
HEAD_W = CKV_OFF + B_KVLORA
CD_OFF = 0
CD_W = 4 * 512
GATE_OFF = CD_OFF + CD_W
TAIL_W = GATE_OFF + N_BRANCH * D_MODEL
IN_FILL = 256
RAW_QB = CQ_OFF
RAW_KVB = RAW_QB + B_HEADS * LANES
RAW_KR = RAW_KVB + B_HEADS * LANES
RAW_W = RAW_KR + LANES

ATT_SUB = 128
ATT_SUB_BOUNDED = 256
SAFE_LOG2 = 40.0

VMEM_LIMIT = 56 * 1024 * 1024


def _dot(a, b):
    return jnp.dot(a, b, preferred_element_type=F32)


def _dot_nt(a, b):
    return lax.dot_general(a, b, (((1,), (1,)), ((), ())), preferred_element_type=F32)


def _lane_iota(n=LANES):
    return lax.broadcasted_iota(jnp.int32, (1, n), 1)


def _rms(x):
    return x * lax.rsqrt(jnp.mean(x * x, axis=-1, keepdims=True) + EPS)


def _rope(t, tabs, half):
    cos, s_left, s_right = tabs
    n = t.shape[-1]
    return (t * cos + pltpu.roll(t, n - half, 1) * s_left
            + pltpu.roll(t, half, 1) * s_right)


def _layer_spec(l, shape):
    nd = len(shape)
    return pl.BlockSpec((None,) + tuple(shape), lambda *_: (l,) + (0,) * nd,
                        pipeline_mode=pl.Buffered(1))


def _stream_specs(split, tq, d, ctx_tile):
    if split:
        return [pl.BlockSpec((1, tq, d), lambda i, j: (i, jnp.minimum(j, ctx_tile - 1), 0)),
                pl.BlockSpec((1, tq, d), lambda i, j: (i, 0, 0))]
    return [pl.BlockSpec((1, tq, d), lambda i, j: (i, j, 0))]


def _mod_spec(l, b, ctx_tile):
    return pl.BlockSpec((None, None, 1, N_MOD * D_MODEL),
                        lambda i, j: (l, jnp.where(j == ctx_tile, b, i), 0, 0))


def _mod_kernel(c_ref, w_ref, b_ref, o_ref):
    c = c_ref[...]
    h = (c * jax.nn.sigmoid(c)).astype(BF16)
    o_ref[...] = _dot(h, w_ref[...].astype(BF16)) + b_ref[...]


def _modulation(cc, w_mod, b_mod):
    depth, d, n = w_mod.shape
    rows = cc.shape[0]
    tn = 1536
    return pl.pallas_call(
        _mod_kernel,
        grid=(depth, n // tn),
        in_specs=[
            pl.BlockSpec((rows, d), lambda l, j: (0, 0)),
            pl.BlockSpec((None, d, tn), lambda l, j: (l, 0, j)),
            pl.BlockSpec((None, 1, tn), lambda l, j: (l, 0, j)),
        ],
        out_specs=pl.BlockSpec((None, rows, tn), lambda l, j: (l, 0, j)),
        out_shape=jax.ShapeDtypeStruct((depth, rows, n), F32),
        compiler_params=pltpu.CompilerParams(
            dimension_semantics=("parallel", "parallel"), vmem_limit_bytes=VMEM_LIMIT),
        name="modulation",
    )(cc, w_mod, b_mod.reshape(depth, 1, n))


def _inproj_kernel(*refs, nt, ctx_tile, n_tiles, split):
    s = pl.program_id(0)
    n_x = 2 if split else 1
    x_refs, refs = refs[:n_x], refs[n_x:]

    def load_x():
        if split:
            is_ctx = jnp.minimum(s, n_tiles - 1) % nt == ctx_tile
            return jnp.where(is_ctx, x_refs[1][0], x_refs[0][0])
        return x_refs[0][0]

    (mod_ref, g_ref, wh_ref, wkr_ref, wt_ref, wuq_ref, wukv_ref, gcq_ref, gckv_ref,
     gqa_ref, gka_ref, gqb_ref, gkb_ref, ca_ref, la_ref, ra_ref, cb_ref, lb_ref, rb_ref,
     gate_ref, cd_ref, qa_ref, ka_ref, vxa_ref, qb_ref, kb_ref, vxb_ref,
     h0, h1, raw0, raw1) = refs
    d = D_MODEL
    tq = h0.shape[0]

    @pl.when(s == 0)
    def _warm_up():
        for ref in (h0, h1, raw0, raw1):
            ref[...] = jnp.zeros(ref.shape, ref.dtype)

    def stages(h_w, h_r, raw_w, raw_r):
        h = h_r[...]
        cq = _dot(h, wh_ref[:, CQ_OFF:CQ_OFF + B_QLORA])
        ckv = _dot(h, wh_ref[:, CKV_OFF:CKV_OFF + B_KVLORA])
        cqn = (_rms(cq) * gcq_ref[...]).astype(BF16)
        ckvn = (_rms(ckv) * gckv_ref[...]).astype(BF16)

        def passthrough(o_ref, off, width):
            for c0 in range(0, width, IN_FILL):
                o_ref[0, :, c0:c0 + IN_FILL] = _dot(
                    h, wt_ref[:, off + c0:off + c0 + IN_FILL]).astype(o_ref.dtype)

        passthrough(cd_ref, CD_OFF, CD_W)
        raw_w[:, RAW_QB:RAW_QB + B_HEADS * LANES] = _dot(cqn, wuq_ref[...])
        raw_w[:, RAW_KVB:RAW_KVB + B_HEADS * LANES] = _dot(ckvn, wukv_ref[...])
        for c0 in range(0, CQ_OFF, IN_FILL):
            raw_w[:, c0:c0 + IN_FILL] = _dot(h, wh_ref[:, c0:c0 + IN_FILL])
        raw_w[:, RAW_KR:RAW_KR + LANES] = _dot(h, wkr_ref[...])
        passthrough(gate_ref, GATE_OFF, N_BRANCH * d)

        lane = _lane_iota()
        lo = lane < A_HD
        nope = lane < B_NOPE
        rope_a = (ca_ref[...], la_ref[...], ra_ref[...])
        rope_b = (cb_ref[...], lb_ref[...], rb_ref[...])

        gq_b = gqb_ref[...] * (B_QK ** -0.5 * LOG2E)
        for hd in range(B_HEADS):
            q = raw_r[:, RAW_QB + hd * LANES:RAW_QB + (hd + 1) * LANES]
            q2 = q * q
            s_n = jnp.sum(jnp.where(nope, q2, 0.0), axis=-1, keepdims=True)
            s_r = jnp.sum(jnp.where(nope, 0.0, q2), axis=-1, keepdims=True)
            inv = jnp.where(nope, lax.rsqrt(s_n * (1.0 / B_NOPE) + EPS),
                            lax.rsqrt(s_r * (1.0 / B_ROPE) + EPS))
            qb_ref[0, :, hd * LANES:(hd + 1) * LANES] = _rope(
                q * inv * gq_b, rope_b, B_ROPE // 2).astype(BF16)

        kr = raw_r[:, RAW_KR:RAW_KR + LANES]
        krn = kr * lax.rsqrt(jnp.sum(kr * kr, axis=-1, keepdims=True) * (1.0 / B_ROPE) + EPS)
        krn = _rope(krn * jnp.where(nope, 0.0, gkb_ref[...]), rope_b, B_ROPE // 2)
        gk_nope = jnp.where(nope, gkb_ref[...], 0.0)
        for hd in range(B_HEADS):
            kv = raw_r[:, RAW_KVB + hd * LANES:RAW_KVB + (hd + 1) * LANES]
            ss = jnp.sum(jnp.where(nope, kv * kv, 0.0), axis=-1, keepdims=True)
            kn = kv * lax.rsqrt(ss * (1.0 / B_NOPE) + EPS) * gk_nope
            kb_ref[0, :, hd * LANES:(hd + 1) * LANES] = (kn + krn).astype(BF16)
            vxb_ref[0, :, hd * LANES:(hd + 1) * LANES] = jnp.where(nope, 1.0, kv).astype(BF16)

        def norm_halves(t):
            t2 = t * t
            s_lo = jnp.sum(jnp.where(lo, t2, 0.0), axis=-1, keepdims=True)
            s_hi = jnp.sum(jnp.where(lo, 0.0, t2), axis=-1, keepdims=True)
            return t * jnp.where(lo, lax.rsqrt(s_lo * (1.0 / A_HD) + EPS),
                                 lax.rsqrt(s_hi * (1.0 / A_HD) + EPS))

        gq_a = gqa_ref[...] * (A_HD ** -0.5 * LOG2E)
        for off, gain, o_ref in ((QA_OFF, gq_a, qa_ref), (KA_OFF, gka_ref[...], ka_ref)):
            for hd in range(A_HEADS):
                t = raw_r[:, off + hd * LANES:off + (hd + 1) * LANES]
                o_ref[0, :, hd * LANES:(hd + 1) * LANES] = _rope(
                    norm_halves(t) * gain, rope_a, A_HD // 2).astype(BF16)
        ones = jnp.ones((tq, A_VD), BF16)
        for hd in range(A_HEADS):
            vxa_ref[0, :, 2 * hd * A_VD:(2 * hd + 1) * A_VD] = raw_r[
                :, VA_OFF + hd * A_VD:VA_OFF + (hd + 1) * A_VD].astype(BF16)
            vxa_ref[0, :, (2 * hd + 1) * A_VD:(2 * hd + 2) * A_VD] = ones

        y = _rms(load_x()) * g_ref[...]
        h_w[...] = (y * (1.0 + mod_ref[:, d:2 * d]) + mod_ref[:, 0:d]).astype(BF16)

    pl.when(s % 2 == 0)(lambda: stages(h0, h1, raw1, raw0))
    pl.when(s % 2 == 1)(lambda: stages(h1, h0, raw0, raw1))


def _inproj(l, x_parts, mods, p, rope_a, rope_b, tq, n_lat, n_ctx):
    b, _, d = x_parts[0].shape
    t = n_lat + n_ctx
    nt = t // tq
    ctx_tile = n_lat // tq
    n_tiles = b * nt
    split = len(x_parts) == 2
    tile_a = lambda s: jnp.minimum(s, n_tiles - 1)
    tile_b = lambda s: jnp.clip(s - 1, 0, n_tiles - 1)
    tile_c = lambda s: jnp.clip(s - 2, 0, n_tiles - 1)
    tab = pl.BlockSpec((tq, LANES), lambda s: (tile_c(s) % nt, 0))
    vec = lambda n: _layer_spec(l, (1, n))
    if split:
        x_specs = [pl.BlockSpec((1, tq, d), lambda s: (tile_a(s) // nt,
                                                       jnp.minimum(tile_a(s) % nt, ctx_tile - 1), 0)),
                   pl.BlockSpec((1, tq, d), lambda s: (tile_a(s) // nt, 0, 0))]
    else:
        x_specs = [pl.BlockSpec((1, tq, d), lambda s: (tile_a(s) // nt, tile_a(s) % nt, 0))]

    def out(width, dtype, tile):
        return (pl.BlockSpec((1, tq, width), lambda s: (tile(s) // nt, tile(s) % nt, 0)),
                jax.ShapeDtypeStruct((b, t, width), dtype))

    outs = [out(N_BRANCH * d, BF16, tile_b), out(CD_W, BF16, tile_b),
            out(A_HEADS * LANES, BF16, tile_c), out(A_HEADS * LANES, BF16, tile_c),
            out(2 * A_HEADS * A_VD, BF16, tile_c), out(B_HEADS * LANES, BF16, tile_c),
            out(B_HEADS * LANES, BF16, tile_c), out(B_HEADS * LANES, BF16, tile_c)]
    return pl.pallas_call(
        functools.partial(_inproj_kernel, nt=nt, ctx_tile=ctx_tile, n_tiles=n_tiles, split=split),
        grid=(n_tiles + 2,),
        in_specs=x_specs + [
            pl.BlockSpec((None, None, 1, N_MOD * d),
                         lambda s: (l, jnp.where(tile_a(s) % nt == ctx_tile, b, tile_a(s) // nt), 0, 0)),
            vec(d),
            _layer_spec(l, (d, HEAD_W)),
            _layer_spec(l, (d, LANES)),
            _layer_spec(l, (d, TAIL_W)),
            _layer_spec(l, (B_QLORA, B_HEADS * LANES)),
            _layer_spec(l, (B_KVLORA, B_HEADS * LANES)),
            vec(B_QLORA), vec(B_KVLORA), vec(LANES), vec(LANES), vec(LANES), vec(LANES),
            tab, tab, tab, tab, tab, tab,
        ],
        out_specs=[o[0] for o in outs],
        out_shape=[o[1] for o in outs],
        scratch_shapes=[pltpu.VMEM((tq, d), BF16), pltpu.VMEM((tq, d), BF16),
                        pltpu.VMEM((tq, RAW_W), F32), pltpu.VMEM((tq, RAW_W), F32)],
        compiler_params=pltpu.CompilerParams(
            dimension_semantics=("arbitrary",), vmem_limit_bytes=VMEM_LIMIT),
        name="inproj",
    )(*x_parts, mods, p["g_norm1"], p["w_head"], p["w_kr"], p["w_tail"], p["w_uq"], p["w_ukv"],
      p["g_cq"], p["g_ckv"], p["gq_a"], p["gk_a"], p["gq_b"], p["gk_b"], *rope_a, *rope_b)


def _softmax_pv(q, k, vx, shift_max):
    s = _dot_nt(q, k)
    if shift_max:
        s = s - jnp.max(s, axis=-1, keepdims=True)
    return _dot(jnp.exp2(s).astype(BF16), vx)


def _sweep(make_tile, bounded, n_lat, t_all):
    def run(shift_max, sub):
        tile = make_tile(shift_max, sub)
        for s0 in range(0, t_all, sub):
            tile(s0, 0 if s0 < n_lat else n_lat)

    pl.when(bounded)(lambda: run(False, ATT_SUB_BOUNDED))
    pl.when(jnp.logical_not(bounded))(lambda: run(True, ATT_SUB))


def _attn_a_kernel(safe_ref, q_ref, k_ref, vx_ref, lam_ref, gsub_ref, o_ref, *, l, n_lat, lam_init):
    t_all = k_ref.shape[1]
    lo = _lane_iota() < A_HD
    la = lam_ref[...].astype(F32)
    lam = (jnp.exp(jnp.sum(la[0:1] * la[1:2], axis=-1, keepdims=True))
           - jnp.exp(jnp.sum(la[2:3] * la[3:4], axis=-1, keepdims=True)) + lam_init)
    post = gsub_ref[...] * (1.0 - lam_init)

    def make_tile(shift_max, sub):
        def tile(r0, key_lo):
            q = q_ref[0, r0:r0 + sub, :]
            k = k_ref[0, key_lo:t_all, :]
            vx = vx_ref[0, key_lo:t_all, :]
            zero = jnp.zeros_like(q)
            r1 = _softmax_pv(jnp.where(lo, q, zero), k, vx, shift_max)
            r2 = _softmax_pv(jnp.where(lo, zero, q), k, vx, shift_max)
            o = (r1[:, 0:A_VD] / r1[:, A_VD:A_VD + 1]
                 - lam * (r2[:, 0:A_VD] / r2[:, A_VD:A_VD + 1]))
            o_ref[0, r0:r0 + sub, :] = (_rms(o) * post).astype(o_ref.dtype)
        return tile

    _sweep(make_tile, safe_ref[l] != 0, n_lat, t_all)


def _attn_a(l, qa, ka, vxa, p, n_lat, lam_init):
    b, t, _ = qa.shape
    head = lambda i, h: (i, 0, h)
    return pl.pallas_call(
        functools.partial(_attn_a_kernel, l=l, n_lat=n_lat, lam_init=lam_init),
        grid=(b, A_HEADS),
        in_specs=[
            pl.BlockSpec(memory_space=pltpu.SMEM),
            pl.BlockSpec((1, t, LANES), head),
            pl.BlockSpec((1, t, LANES), head),
            pl.BlockSpec((1, t, 2 * A_VD), head),
            _layer_spec(l, (4, A_HD)),
            _layer_spec(l, (1, LANES)),
        ],
        out_specs=pl.BlockSpec((1, t, LANES), head),
        out_shape=jax.ShapeDtypeStruct((b, t, BRANCH_W), BF16),
        compiler_params=pltpu.CompilerParams(
            dimension_semantics=("parallel", "parallel"), vmem_limit_bytes=VMEM_LIMIT),
        name="attn_a",
    )(p["safe_a"], qa, ka, vxa, p["lam_a"], p["g_sub_a"])


def _attn_b_kernel(safe_ref, q_ref, k_ref, vx_ref, o_ref, *, l, n_lat):
    t_all = k_ref.shape[1]
    nope = _lane_iota() < B_NOPE

    def make_tile(shift_max, sub):
        def tile(r0, key_lo):
            outs = []
            for hh in range(2):
                sl = slice(hh * LANES, (hh + 1) * LANES)
                r = _softmax_pv(q_ref[0, r0:r0 + sub, sl], k_ref[0, key_lo:t_all, sl],
                                vx_ref[0, key_lo:t_all, :], shift_max)[:, sl]
                outs.append(r / r[:, 0:1])
            o = jnp.where(nope, pltpu.roll(outs[0], B_VD, 1), outs[1])
            o_ref[0, r0:r0 + sub, :] = o.astype(o_ref.dtype)
        return tile

    _sweep(make_tile, safe_ref[l] != 0, n_lat, t_all)


def _attn_b(l, qb, kb, vxb, p, n_lat):
    b, t, _ = qb.shape
    pair = lambda i, h: (i, 0, h)
    spec = pl.BlockSpec((1, t, 2 * LANES), pair)
    return pl.pallas_call(
        functools.partial(_attn_b_kernel, l=l, n_lat=n_lat),
        grid=(b, B_HEADS // 2),
        in_specs=[pl.BlockSpec(memory_space=pltpu.SMEM), spec, spec, spec],
        out_specs=pl.BlockSpec((1, t, LANES), pair),
        out_shape=jax.ShapeDtypeStruct((b, t, BRANCH_W), BF16),
        compiler_params=pltpu.CompilerParams(
            dimension_semantics=("parallel", "parallel"), vmem_limit_bytes=VMEM_LIMIT),
        name="attn_b",
    )(p["safe_b"], qb, kb, vxb)


def _mix_cd_kernel(u_ref, pb_ref, pc_ref, pxx_ref, wp_ref, sp_ref, wc_ref,
                   yc_ref, yd_ref, pad_s, *, segments):
    g = pl.program_id(1)
    zeros_halo = jnp.zeros((POOL_HALO, LANES), F32)
    wc = wc_ref[...]

    for start, length in segments:
        rows = lax.broadcasted_iota(jnp.int32, (length, 1), 0)
        u = u_ref[0, start:start + length, :].astype(F32)
        pad_s[0:POOL_HALO, :] = zeros_halo
        pad_s[POOL_HALO:POOL_HALO + length, :] = u
        pad_s[POOL_HALO + length:2 * POOL_HALO + length, :] = zeros_halo

        for gi, w in enumerate(POOL_WINDOWS):
            @pl.when(g == gi)
            def _pool(w=w):
                acc = pad_s[POOL_HALO - w // 2:POOL_HALO - w // 2 + length, :]
                for j in range(1 - w // 2, w // 2):
                    acc = acc + pad_s[POOL_HALO + j:POOL_HALO + j + length, :]
                cnt = (jnp.minimum(rows + w // 2, length) - jnp.maximum(rows - w // 2, 0)).astype(F32)
                dd = acc / cnt - u
                y = _dot(dd.astype(BF16), wp_ref[...]) * sp_ref[...]
                yc_ref[0, start:start + length, :] = y.astype(yc_ref.dtype)

        uu = (pc_ref[0, start:start + length, :].astype(F32)
              * pxx_ref[0, start:start + length, :].astype(F32))
        pad_s[POOL_HALO:POOL_HALO + length, :] = uu
        y = (pad_s[POOL_HALO - 1:POOL_HALO - 1 + length, :] * wc[0:1]
             + uu * wc[1:2]
             + pad_s[POOL_HALO + 1:POOL_HALO + 1 + length, :] * wc[2:3])
        yd_ref[0, start:start + length, :] = (
            pb_ref[0, start:start + length, :].astype(F32) * y).astype(yd_ref.dtype)


def _mix_cd(l, cd, p, n_lat):
    b, t, _ = cd.shape
    segments = ((0, n_lat), (n_lat, t - n_lat))
    n_g = len(POOL_WINDOWS)

    def col(k):
        return pl.BlockSpec((1, t, LANES), lambda i, g: (i, 0, k * n_g + g))

    out_spec = pl.BlockSpec((1, t, LANES), lambda i, g: (i, 0, g))
    return pl.pallas_call(
        functools.partial(_mix_cd_kernel, segments=segments),
        grid=(b, n_g),
        in_specs=[
            col(0), col(1), col(2), col(3),
            pl.BlockSpec((None, None, POOL_GROUP, POOL_GROUP), lambda i, g: (l, g, 0, 0)),
            pl.BlockSpec((None, 1, LANES), lambda i, g: (l, 0, g)),
            pl.BlockSpec((None, 3, LANES), lambda i, g: (l, 0, g)),
        ],
        out_specs=[out_spec, out_spec],
        out_shape=[jax.ShapeDtypeStruct((b, t, BRANCH_W), BF16)] * 2,
        scratch_shapes=[pltpu.VMEM((max(n_lat, t - n_lat) + 2 * POOL_HALO, LANES), F32)],
        compiler_params=pltpu.CompilerParams(
            dimension_semantics=("parallel", "parallel"), vmem_limit_bytes=VMEM_LIMIT),
        name="mix_cd",
    )(cd, cd, cd, cd, p["w_pool"], p["s_pool"], p["w_conv"])


def _merge_ffn_kernel(*refs, ctx_tile, split):
    ya_ref, yb_ref, yc_ref, yd_ref, gate_ref = refs[:5]
    if split:
        x_ref, xc_ref = refs[5:7]
        refs = refs[7:]
        x = jnp.where(pl.program_id(1) == ctx_tile, xc_ref[0], x_ref[0])
    else:
        x = refs[5][0]
        refs = refs[6:]
    mod_ref, wb_ref, wo_ref, g2_ref, w1_ref, w2_ref, o_ref = refs
    d = D_MODEL
    merged = None
    for n, y_ref in enumerate((ya_ref, yb_ref, yc_ref, yd_ref)):
        proj = _dot(y_ref[0], wb_ref[n])
        gate = jax.nn.sigmoid(gate_ref[0, :, n * d:(n + 1) * d].astype(F32))
        merged = gate * proj if merged is None else merged + gate * proj
    mix = _dot(merged.astype(BF16), wo_ref[...])
    x1 = x + mod_ref[:, 2 * d:3 * d] * mix

    y = _rms(x1) * g2_ref[...]
    h = (y * (1.0 + mod_ref[:, 4 * d:5 * d]) + mod_ref[:, 3 * d:4 * d]).astype(BF16)
    f = None
    for c0 in range(0, D_FF, d):
        a = jnp.maximum(_dot(h, w1_ref[:, c0:c0 + d]), 0.0)
        part = _dot((a * a).astype(BF16), w2_ref[c0:c0 + d, :])
        f = part if f is None else f + part
    o_ref[0] = x1 + mod_ref[:, 5 * d:6 * d] * f


def _merge_ffn(l, ys, gates, x_parts, mods, p, tq, n_lat, n_ctx, with_ctx):
    b, _, d = x_parts[0].shape
    t = n_lat + n_ctx
    ctx_tile = n_lat // tq
    split = len(x_parts) == 2
    row = lambda i, j: (i, j, 0)
    y_spec = pl.BlockSpec((1, tq, BRANCH_W), row)
    nq, out_rows = (t // tq, t) if with_ctx else (ctx_tile, n_lat)
    aliases = {5: 0} if (with_ctx and not split) else {}
    return pl.pallas_call(
        functools.partial(_merge_ffn_kernel, ctx_tile=ctx_tile, split=split),
        grid=(b, nq),
        in_specs=[
            y_spec, y_spec, y_spec, y_spec,
            pl.BlockSpec((1, tq, N_BRANCH * d), row),
        ] + _stream_specs(split, tq, d, ctx_tile) + [
            _mod_spec(l, b, ctx_tile),
            _layer_spec(l, (N_BRANCH, BRANCH_W, d)),
            _layer_spec(l, (d, d)),
            _layer_spec(l, (1, d)),
            _layer_spec(l, (d, D_FF)),
            _layer_spec(l, (D_FF, d)),
        ],
        out_specs=pl.BlockSpec((1, tq, d), row),
        out_shape=jax.ShapeDtypeStruct((b, out_rows, d), F32),
        input_output_aliases=aliases,
        compiler_params=pltpu.CompilerParams(
            dimension_semantics=("parallel", "parallel"), vmem_limit_bytes=VMEM_LIMIT),
        name="merge_ffn",
    )(*ys, gates, *x_parts, mods, p["w_branch"], p["w_o"], p["g_norm2"], p["w_ff1"], p["w_ff2"])


def _rope_tables(n_lat, n_ctx, rot_dim, lane_lo, period, total=LANES):
    rows = n_lat // GRID_W
    row = np.repeat(np.arange(rows, dtype=np.float64), GRID_W)
    col = np.tile(np.arange(GRID_W, dtype=np.float64), rows)
    n_freq = rot_dim // 4
    inv = ROPE_BASE ** (-np.arange(n_freq, dtype=np.float64) / n_freq)
    inv = inv.astype(np.float32).astype(np.float64)
    ang = np.concatenate([row[:, None] * inv, col[:, None] * inv], axis=-1)
    ang = ang.astype(np.float32).astype(np.float64)
    half = rot_dim // 2
    t = n_lat + n_ctx
    cos = np.ones((t, total), np.float32)
    s_left = np.zeros((t, total), np.float32)
    s_right = np.zeros((t, total), np.float32)
    starts = [lane_lo] if period == 0 else list(range(lane_lo, total, period))
    for s0 in starts:
        cos[:n_lat, s0:s0 + half] = np.cos(ang)
        cos[:n_lat, s0 + half:s0 + rot_dim] = np.cos(ang)
        s_left[:n_lat, s0:s0 + half] = -np.sin(ang)
        s_right[:n_lat, s0 + half:s0 + rot_dim] = np.sin(ang)
    return jnp.asarray(cos), jnp.asarray(s_left), jnp.asarray(s_right)


def _score_bounds(gq_a, gk_a, gq_b, gk_b):
    amax = lambda g: jnp.max(jnp.abs(g), axis=-1)
    bound_a = A_HD * amax(gq_a) * amax(gk_a) * (A_HD ** -0.5 * LOG2E)
    nq = jnp.sqrt(B_NOPE * amax(gq_b[:, :B_NOPE]) ** 2 + B_ROPE * amax(gq_b[:, B_NOPE:]) ** 2)
    nk = jnp.sqrt(B_NOPE * amax(gk_b[:, :B_NOPE]) ** 2 + B_ROPE * amax(gk_b[:, B_NOPE:]) ** 2)
    bound_b = nq * nk * (B_QK ** -0.5 * LOG2E)
    margin = 1.05
    return ((bound_a * margin < SAFE_LOG2).astype(jnp.int32),
            (bound_b * margin < SAFE_LOG2).astype(jnp.int32))


def _pad_head_slots(v, width):
    lead = v.shape[:-1]
    v = v.reshape(lead + (-1, width))
    v = jnp.pad(v, [(0, 0)] * len(lead) + [(0, 0), (0, LANES - width)])
    return v.reshape(lead + (-1,))


def kernel(x, c, ctx, c_ctx, w_mod, b_mod, g_norm1, g_norm2, w_in, gq_a, gk_a, lam_a, g_sub_a,
           g_cq, w_uq, g_ckv, w_ukv, gq_b, gk_b, w_pool, s_pool, w_conv, w_branch, w_o,
           w_ff1, w_ff2):
    b, n_lat, d = x.shape
    n_ctx = ctx.shape[1]
    depth = w_mod.shape[0]
    tq = n_ctx
    assert d == D_MODEL and n_lat % ATT_SUB_BOUNDED == 0 and n_ctx % ATT_SUB_BOUNDED == 0
    assert n_lat % tq == 0 and tq % LANES == 0 and n_lat % GRID_W == 0

    rope_a = _rope_tables(n_lat, n_ctx, A_HD, 0, A_HD)
    rope_b = _rope_tables(n_lat, n_ctx, B_ROPE, B_NOPE, 0)

    mod_rows = -(-(b + 1) // 8) * 8
    cc = jnp.concatenate([c, c_ctx[None, :], jnp.zeros((mod_rows - b - 1, d), F32)], axis=0)
    mods = _modulation(cc, w_mod, b_mod).reshape(depth, mod_rows, 1, N_MOD * d)

    vec = lambda a: a[:, None, :]
    safe_a, safe_b = _score_bounds(gq_a, gk_a, gq_b, gk_b)
    p = {
        "safe_a": safe_a, "safe_b": safe_b,
        "w_head": w_in[:, :, :HEAD_W].astype(BF16),
        "w_kr": jnp.pad(w_in[:, :, HEAD_W:HEAD_W + B_ROPE],
                        ((0, 0), (0, 0), (B_NOPE, LANES - B_NOPE - B_ROPE))).astype(BF16),
        "w_tail": w_in[:, :, HEAD_W + B_ROPE:].astype(BF16),
        "w_uq": _pad_head_slots(w_uq, B_QK).astype(BF16),
        "w_ukv": w_ukv.astype(BF16),
        "w_pool": w_pool.astype(BF16),
        "w_branch": w_branch.astype(BF16),
        "w_o": w_o.astype(BF16),
        "w_ff1": w_ff1.astype(BF16),
        "w_ff2": w_ff2.astype(BF16),
        "g_norm1": vec(g_norm1), "g_norm2": vec(g_norm2),
        "g_cq": vec(g_cq), "g_ckv": vec(g_ckv),
        "gq_a": vec(jnp.tile(gq_a, (1, 2))), "gk_a": vec(jnp.tile(gk_a, (1, 2))),
        "gq_b": vec(jnp.pad(gq_b, ((0, 0), (0, LANES - B_QK)))),
        "gk_b": vec(jnp.pad(gk_b, ((0, 0), (0, LANES - B_QK)))),
        "lam_a": lam_a, "g_sub_a": vec(g_sub_a),
        "s_pool": vec(s_pool), "w_conv": w_conv,
    }

    x_parts = (x, ctx)
    for l in range(depth):
        last = l == depth - 1
        lam_init = 0.8 - 0.6 * math.exp(-0.3 * l)
        gates, cd, qa, ka, vxa, qb, kb, vxb = _inproj(
            l, x_parts, mods, p, rope_a, rope_b, tq, n_lat, n_ctx)
        ya = _attn_a(l, qa, ka, vxa, p, n_lat, lam_init)
        yb = _attn_b(l, qb, kb, vxb, p, n_lat)
        yc, yd = _mix_cd(l, cd, p, n_lat)
        x_parts = (_merge_ffn(l, (ya, yb, yc, yd), gates, x_parts, mods, p, tq, n_lat, n_ctx,
                              not last),)
    return x_parts[0]
```

```python
import functools
import math

import numpy as np
import jax
import jax.numpy as jnp
from jax import lax
from jax.experimental import pallas as pl
from jax.experimental.pallas import tpu as pltpu

F32 = jnp.float32
BF16 = jnp.bfloat16

D_MODEL = 1024
GRID_W = 64
ROPE_BASE = 10000.0
EPS = 1e-6
LOG2E = math.log2(math.e)

A_HEADS = 4
A_HD = 64
A_VD = 128
B_HEADS = 8
B_NOPE = 64
B_ROPE = 32
B_QK = B_NOPE + B_ROPE
B_VD = 64
B_QLORA = 384
B_KVLORA = 256
POOL_WINDOWS = (2, 4, 8, 16)
POOL_GROUP = 128
POOL_HALO = 8
BRANCH_W = 512
N_BRANCH = 4
D_FF = 4 * D_MODEL
N_MOD = 6

LANES = 128

QA_OFF = 0
KA_OFF = QA_OFF + 512
VA_OFF = KA_OFF + 512
CKV_OFF = VA_OFF + 512
CQ_OFF = CKV_OFF + B_KVLORA
KR_OFF = CQ_OFF + B_QLORA
HEAD_W = KR_OFF + LANES
W_IN_HEAD = 3 * 512 + B_QLORA + B_KVLORA
CD_OFF = 0
CD_W = 4 * 512
GATE_OFF = CD_OFF + CD_W
TAIL_W = GATE_OFF + N_BRANCH * D_MODEL
IN_FILL = 256
RAW_QB = CKV_OFF
RAW_KVB = RAW_QB + B_HEADS * LANES
RAW_KR = RAW_KVB + B_HEADS * LANES
RAW_W = RAW_KR + LANES

ATT_SUB = 128
ATT_SUB_BOUNDED = 256
SAFE_LOG2 = 40.0

VMEM_LIMIT = 56 * 1024 * 1024


def _dot(a, b):
    return jnp.dot(a, b, preferred_element_type=F32)


def _dot_nt(a, b):
    return lax.dot_general(a, b, (((1,), (1,)), ((), ())), preferred_element_type=F32)


def _lane_iota(n=LANES):
    return lax.broadcasted_iota(jnp.int32, (1, n), 1)


def _rms(x):
    return x * lax.rsqrt(jnp.mean(x * x, axis=-1, keepdims=True) + EPS)


def _rope(t, tabs, half):
    cos, s_left, s_right = tabs
    n = t.shape[-1]
    return (t * cos + pltpu.roll(t, n - half, 1) * s_left
            + pltpu.roll(t, half, 1) * s_right)


def _layer_spec(l, shape):
    nd = len(shape)
    return pl.BlockSpec((None,) + tuple(shape), lambda *_: (l,) + (0,) * nd,
                        pipeline_mode=pl.Buffered(1))


def _stream_specs(split, tq, d, ctx_tile):
    if split:
        return [pl.BlockSpec((1, tq, d), lambda i, j: (i, jnp.minimum(j, ctx_tile - 1), 0)),
                pl.BlockSpec((1, tq, d), lambda i, j: (i, 0, 0))]
    return [pl.BlockSpec((1, tq, d), lambda i, j: (i, j, 0))]


def _mod_spec(l, b, ctx_tile):
    return pl.BlockSpec((None, None, 1, N_MOD * D_MODEL),
                        lambda i, j: (l, jnp.where(j == ctx_tile, b, i), 0, 0))


def _mod_kernel(c_ref, w_ref, b_ref, o_ref):
    c = c_ref[...]
    h = (c * jax.nn.sigmoid(c)).astype(BF16)
    o_ref[...] = _dot(h, w_ref[...].astype(BF16)) + b_ref[...]


def _modulation(cc, w_mod, b_mod):
    depth, d, n = w_mod.shape
    rows = cc.shape[0]
    tn = 1536
    return pl.pallas_call(
        _mod_kernel,
        grid=(depth, n // tn),
        in_specs=[
            pl.BlockSpec((rows, d), lambda l, j: (0, 0)),
            pl.BlockSpec((None, d, tn), lambda l, j: (l, 0, j)),
            pl.BlockSpec((None, 1, tn), lambda l, j: (l, 0, j)),
        ],
        out_specs=pl.BlockSpec((None, rows, tn), lambda l, j: (l, 0, j)),
        out_shape=jax.ShapeDtypeStruct((depth, rows, n), F32),
        compiler_params=pltpu.CompilerParams(
            dimension_semantics=("parallel", "parallel"), vmem_limit_bytes=VMEM_LIMIT),
        name="modulation",
    )(cc, w_mod, b_mod.reshape(depth, 1, n))


def _inproj_kernel(*refs, nt, ctx_tile, n_tiles, split):
    s = pl.program_id(0)
    n_x = 2 if split else 1
    x_refs, refs = refs[:n_x], refs[n_x:]

    def load_x():
        if split:
            is_ctx = jnp.minimum(s, n_tiles - 1) % nt == ctx_tile
            return jnp.where(is_ctx, x_refs[1][0], x_refs[0][0])
        return x_refs[0][0]

    (mod_ref, g_ref, wh_ref, wt_ref, wuq_ref, wukv_ref, gcq_ref, gckv_ref,
     gqa_ref, gka_ref, gqb_ref, gkb_ref, ca_ref, la_ref, ra_ref, cb_ref, lb_ref, rb_ref,
     gate_ref, cd_ref, qa_ref, ka_ref, vxa_ref, qb_ref, kb_ref, vxb_ref,
     h0, h1, raw0, raw1) = refs
    d = D_MODEL
    tq = h0.shape[0]

    @pl.when(s == 0)
    def _warm_up():
        for ref in (h0, h1, raw0, raw1):
            ref[...] = jnp.zeros(ref.shape, ref.dtype)

    def stages(h_w, h_r, raw_w, raw_r):
        h = h_r[...]
        cq_kr = _dot(h, wh_ref[:, CQ_OFF:HEAD_W])
        ckv = _dot(h, wh_ref[:, CKV_OFF:CKV_OFF + B_KVLORA])
        cqn = (_rms(cq_kr[:, 0:B_QLORA]) * gcq_ref[...]).astype(BF16)
        ckvn = (_rms(ckv) * gckv_ref[...]).astype(BF16)

        def passthrough(o_ref, off, width):
            for c0 in range(0, width, IN_FILL):
                o_ref[0, :, c0:c0 + IN_FILL] = _dot(
                    h, wt_ref[:, off + c0:off + c0 + IN_FILL]).astype(o_ref.dtype)

        passthrough(cd_ref, CD_OFF, CD_W)
        raw_w[:, RAW_QB:RAW_QB + B_HEADS * LANES] = _dot(cqn, wuq_ref[...])
        raw_w[:, RAW_KVB:RAW_KVB + B_HEADS * LANES] = _dot(ckvn, wukv_ref[...])
        for c0 in range(0, CKV_OFF, IN_FILL):
            raw_w[:, c0:c0 + IN_FILL] = _dot(h, wh_ref[:, c0:c0 + IN_FILL])
        raw_w[:, RAW_KR:RAW_KR + LANES] = cq_kr[:, B_QLORA:B_QLORA + LANES]
        passthrough(gate_ref, GATE_OFF, N_BRANCH * d)

        lane = _lane_iota()
        lo = lane < A_HD
        nope = lane < B_NOPE
        rope_a = (ca_ref[...], la_ref[...], ra_ref[...])
        rope_b = (cb_ref[...], lb_ref[...], rb_ref[...])

        gq_b = gqb_ref[...] * (B_QK ** -0.5 * LOG2E)
        for hd in range(B_HEADS):
            q = raw_r[:, RAW_QB + hd * LANES:RAW_QB + (hd + 1) * LANES]
            q2 = q * q
            s_n = jnp.sum(jnp.where(nope, q2, 0.0), axis=-1, keepdims=True)
            s_r = jnp.sum(jnp.where(nope, 0.0, q2), axis=-1, keepdims=True)
            inv = jnp.where(nope, lax.rsqrt(s_n * (1.0 / B_NOPE) + EPS),
                            lax.rsqrt(s_r * (1.0 / B_ROPE) + EPS))
            qb_ref[0, :, hd * LANES:(hd + 1) * LANES] = _rope(
                q * inv * gq_b, rope_b, B_ROPE // 2).astype(BF16)

        kr = raw_r[:, RAW_KR:RAW_KR + LANES]
        krn = kr * lax.rsqrt(jnp.sum(kr * kr, axis=-1, keepdims=True) * (1.0 / B_ROPE) + EPS)
        krn = _rope(krn * jnp.where(nope, 0.0, gkb_ref[...]), rope_b, B_ROPE // 2)
        gk_nope = jnp.where(nope, gkb_ref[...], 0.0)
        for hd in range(B_HEADS):
            kv = raw_r[:, RAW_KVB + hd * LANES:RAW_KVB + (hd + 1) * LANES]
            ss = jnp.sum(jnp.where(nope, kv * kv, 0.0), axis=-1, keepdims=True)
            kn = kv * lax.rsqrt(ss * (1.0 / B_NOPE) + EPS) * gk_nope
            kb_ref[0, :, hd * LANES:(hd + 1) * LANES] = (kn + krn).astype(BF16)
            vxb_ref[0, :, hd * LANES:(hd + 1) * LANES] = jnp.where(nope, 1.0, kv).astype(BF16)

        def norm_halves(t):
            t2 = t * t
            s_lo = jnp.sum(jnp.where(lo, t2, 0.0), axis=-1, keepdims=True)
            s_hi = jnp.sum(jnp.where(lo, 0.0, t2), axis=-1, keepdims=True)
            return t * jnp.where(lo, lax.rsqrt(s_lo * (1.0 / A_HD) + EPS),
                                 lax.rsqrt(s_hi * (1.0 / A_HD) + EPS))

        gq_a = gqa_ref[...] * (A_HD ** -0.5 * LOG2E)
        for off, gain, o_ref in ((QA_OFF, gq_a, qa_ref), (KA_OFF, gka_ref[...], ka_ref)):
            for hd in range(A_HEADS):
                t = raw_r[:, off + hd * LANES:off + (hd + 1) * LANES]
                o_ref[0, :, hd * LANES:(hd + 1) * LANES] = _rope(
                    norm_halves(t) * gain, rope_a, A_HD // 2).astype(BF16)
        ones = jnp.ones((tq, A_VD), BF16)
        for hd in range(A_HEADS):
            vxa_ref[0, :, 2 * hd * A_VD:(2 * hd + 1) * A_VD] = raw_r[
                :, VA_OFF + hd * A_VD:VA_OFF + (hd + 1) * A_VD].astype(BF16)
            vxa_ref[0, :, (2 * hd + 1) * A_VD:(2 * hd + 2) * A_VD] = ones

        y = _rms(load_x()) * g_ref[...]
        h_w[...] = (y * (1.0 + mod_ref[:, d:2 * d]) + mod_ref[:, 0:d]).astype(BF16)

    pl.when(s % 2 == 0)(lambda: stages(h0, h1, raw1, raw0))
    pl.when(s % 2 == 1)(lambda: stages(h1, h0, raw0, raw1))


def _inproj(l, x_parts, mods, p, rope_a, rope_b, tq, n_lat, n_ctx):
    b, _, d = x_parts[0].shape
    t = n_lat + n_ctx
    nt = t // tq
    ctx_tile = n_lat // tq
    n_tiles = b * nt
    split = len(x_parts) == 2
    tile_a = lambda s: jnp.minimum(s, n_tiles - 1)
    tile_b = lambda s: jnp.clip(s - 1, 0, n_tiles - 1)
    tile_c = lambda s: jnp.clip(s - 2, 0, n_tiles - 1)
    tab = pl.BlockSpec((tq, LANES), lambda s: (tile_c(s) % nt, 0))
    vec = lambda n: _layer_spec(l, (1, n))
    if split:
        x_specs = [pl.BlockSpec((1, tq, d), lambda s: (tile_a(s) // nt,
                                                       jnp.minimum(tile_a(s) % nt, ctx_tile - 1), 0)),
                   pl.BlockSpec((1, tq, d), lambda s: (tile_a(s) // nt, 0, 0))]
    else:
        x_specs = [pl.BlockSpec((1, tq, d), lambda s: (tile_a(s) // nt, tile_a(s) % nt, 0))]

    def out(width, dtype, tile):
        return (pl.BlockSpec((1, tq, width), lambda s: (tile(s) // nt, tile(s) % nt, 0)),
                jax.ShapeDtypeStruct((b, t, width), dtype))

    outs = [out(N_BRANCH * d, BF16, tile_b), out(CD_W, BF16, tile_b),
            out(A_HEADS * LANES, BF16, tile_c), out(A_HEADS * LANES, BF16, tile_c),
            out(2 * A_HEADS * A_VD, BF16, tile_c), out(B_HEADS * LANES, BF16, tile_c),
            out(B_HEADS * LANES, BF16, tile_c), out(B_HEADS * LANES, BF16, tile_c)]
    return pl.pallas_call(
        functools.partial(_inproj_kernel, nt=nt, ctx_tile=ctx_tile, n_tiles=n_tiles, split=split),
        grid=(n_tiles + 2,),
        in_specs=x_specs + [
            pl.BlockSpec((None, None, 1, N_MOD * d),
                         lambda s: (l, jnp.where(tile_a(s) % nt == ctx_tile, b, tile_a(s) // nt), 0, 0)),
            vec(d),
            _layer_spec(l, (d, HEAD_W)),
            _layer_spec(l, (d, TAIL_W)),
            _layer_spec(l, (B_QLORA, B_HEADS * LANES)),
            _layer_spec(l, (B_KVLORA, B_HEADS * LANES)),
            vec(B_QLORA), vec(B_KVLORA), vec(LANES), vec(LANES), vec(LANES), vec(LANES),
            tab, tab, tab, tab, tab, tab,
        ],
        out_specs=[o[0] for o in outs],
        out_shape=[o[1] for o in outs],
        scratch_shapes=[pltpu.VMEM((tq, d), BF16), pltpu.VMEM((tq, d), BF16),
                        pltpu.VMEM((tq, RAW_W), F32), pltpu.VMEM((tq, RAW_W), F32)],
        compiler_params=pltpu.CompilerParams(
            dimension_semantics=("arbitrary",), vmem_limit_bytes=VMEM_LIMIT),
        name="inproj",
    )(*x_parts, mods, p["g_norm1"], p["w_head"], p["w_tail"], p["w_uq"], p["w_ukv"],
      p["g_cq"], p["g_ckv"], p["gq_a"], p["gk_a"], p["gq_b"], p["gk_b"], *rope_a, *rope_b)


def _softmax_pv(q, k, vx, shift_max):
    s = _dot_nt(q, k)
    if shift_max:
        s = s - jnp.max(s, axis=-1, keepdims=True)
    return _dot(jnp.exp2(s).astype(BF16), vx)


def _sweep(make_tile, bounded, n_lat, t_all):
    def run(shift_max, sub):
        tile = make_tile(shift_max, sub)
        for s0 in range(0, t_all, sub):
            tile(s0, 0 if s0 < n_lat else n_lat)

    pl.when(bounded)(lambda: run(False, ATT_SUB_BOUNDED))
    pl.when(jnp.logical_not(bounded))(lambda: run(True, ATT_SUB))


def _attn_a_kernel(safe_ref, q_ref, k_ref, vx_ref, lam_ref, gsub_ref, o_ref, *, l, n_lat, lam_init):
    t_all = k_ref.shape[1]
    lo = _lane_iota() < A_HD
    la = lam_ref[...].astype(F32)
    lam = (jnp.exp(jnp.sum(la[0:1] * la[1:2], axis=-1, keepdims=True))
           - jnp.exp(jnp.sum(la[2:3] * la[3:4], axis=-1, keepdims=True)) + lam_init)
    post = gsub_ref[...] * (1.0 - lam_init)

    def make_tile(shift_max, sub):
        def tile(r0, key_lo):
            q = q_ref[0, r0:r0 + sub, :]
            k = k_ref[0, key_lo:t_all, :]
            vx = vx_ref[0, key_lo:t_all, :]
            zero = jnp.zeros_like(q)
            r1 = _softmax_pv(jnp.where(lo, q, zero), k, vx, shift_max)
            r2 = _softmax_pv(jnp.where(lo, zero, q), k, vx, shift_max)
            o = (r1[:, 0:A_VD] / r1[:, A_VD:A_VD + 1]
                 - lam * (r2[:, 0:A_VD] / r2[:, A_VD:A_VD + 1]))
            o_ref[0, r0:r0 + sub, :] = (_rms(o) * post).astype(o_ref.dtype)
        return tile

    _sweep(make_tile, safe_ref[l] != 0, n_lat, t_all)


def _attn_a(l, qa, ka, vxa, p, n_lat, lam_init):
    b, t, _ = qa.shape
    head = lambda i, h: (i, 0, h)
    return pl.pallas_call(
        functools.partial(_attn_a_kernel, l=l, n_lat=n_lat, lam_init=lam_init),
        grid=(b, A_HEADS),
        in_specs=[
            pl.BlockSpec(memory_space=pltpu.SMEM),
            pl.BlockSpec((1, t, LANES), head),
            pl.BlockSpec((1, t, LANES), head),
            pl.BlockSpec((1, t, 2 * A_VD), head),
            _layer_spec(l, (4, A_HD)),
            _layer_spec(l, (1, LANES)),
        ],
        out_specs=pl.BlockSpec((1, t, LANES), head),
        out_shape=jax.ShapeDtypeStruct((b, t, BRANCH_W), BF16),
        compiler_params=pltpu.CompilerParams(
            dimension_semantics=("parallel", "parallel"), vmem_limit_bytes=VMEM_LIMIT),
        name="attn_a",
    )(p["safe_a"], qa, ka, vxa, p["lam_a"], p["g_sub_a"])


def _attn_b_kernel(safe_ref, q_ref, k_ref, vx_ref, o_ref, *, l, n_lat):
    t_all = k_ref.shape[1]
    nope = _lane_iota() < B_NOPE

    def make_tile(shift_max, sub):
        def tile(r0, key_lo):
            outs = []
            for hh in range(2):
                sl = slice(hh * LANES, (hh + 1) * LANES)
                r = _softmax_pv(q_ref[0, r0:r0 + sub, sl], k_ref[0, key_lo:t_all, sl],
                                vx_ref[0, key_lo:t_all, :], shift_max)[:, sl]
                outs.append(r / r[:, 0:1])
            o = jnp.where(nope, pltpu.roll(outs[0], B_VD, 1), outs[1])
            o_ref[0, r0:r0 + sub, :] = o.astype(o_ref.dtype)
        return tile

    _sweep(make_tile, safe_ref[l] != 0, n_lat, t_all)


def _attn_b(l, qb, kb, vxb, p, n_lat):
    b, t, _ = qb.shape
    pair = lambda i, h: (i, 0, h)
    spec = pl.BlockSpec((1, t, 2 * LANES), pair)
    return pl.pallas_call(
        functools.partial(_attn_b_kernel, l=l, n_lat=n_lat),
        grid=(b, B_HEADS // 2),
        in_specs=[pl.BlockSpec(memory_space=pltpu.SMEM), spec, spec, spec],
        out_specs=pl.BlockSpec((1, t, LANES), pair),
        out_shape=jax.ShapeDtypeStruct((b, t, BRANCH_W), BF16),
        compiler_params=pltpu.CompilerParams(
            dimension_semantics=("parallel", "parallel"), vmem_limit_bytes=VMEM_LIMIT),
        name="attn_b",
    )(p["safe_b"], qb, kb, vxb)


def _mix_cd_kernel(u_ref, pb_ref, pc_ref, pxx_ref, wp_ref, sp_ref, wc_ref,
                   yc_ref, yd_ref, pad_s, *, segments):
    g = pl.program_id(1)
    zeros_halo = jnp.zeros((POOL_HALO, LANES), F32)
    wc = wc_ref[...]

    for start, length in segments:
        rows = lax.broadcasted_iota(jnp.int32, (length, 1), 0)
        u = u_ref[0, start:start + length, :].astype(F32)
        pad_s[0:POOL_HALO, :] = zeros_halo
        pad_s[POOL_HALO:POOL_HALO + length, :] = u
        pad_s[POOL_HALO + length:2 * POOL_HALO + length, :] = zeros_halo

        for gi, w in enumerate(POOL_WINDOWS):
            @pl.when(g == gi)
            def _pool(w=w):
                acc = pad_s[POOL_HALO - w // 2:POOL_HALO - w // 2 + length, :]
                for j in range(1 - w // 2, w // 2):
                    acc = acc + pad_s[POOL_HALO + j:POOL_HALO + j + length, :]
                cnt = (jnp.minimum(rows + w // 2, length) - jnp.maximum(rows - w // 2, 0)).astype(F32)
                dd = acc / cnt - u
                y = _dot(dd.astype(BF16), wp_ref[...]) * sp_ref[...]
                yc_ref[0, start:start + length, :] = y.astype(yc_ref.dtype)

        uu = (pc_ref[0, start:start + length, :].astype(F32)
              * pxx_ref[0, start:start + length, :].astype(F32))
        pad_s[POOL_HALO:POOL_HALO + length, :] = uu
        y = (pad_s[POOL_HALO - 1:POOL_HALO - 1 + length, :] * wc[0:1]
             + uu * wc[1:2]
             + pad_s[POOL_HALO + 1:POOL_HALO + 1 + length, :] * wc[2:3])
        yd_ref[0, start:start + length, :] = (
            pb_ref[0, start:start + length, :].astype(F32) * y).astype(yd_ref.dtype)


def _mix_cd(l, cd, p, n_lat):
    b, t, _ = cd.shape
    segments = ((0, n_lat), (n_lat, t - n_lat))
    n_g = len(POOL_WINDOWS)

    def col(k):
        return pl.BlockSpec((1, t, LANES), lambda i, g: (i, 0, k * n_g + g))

    out_spec = pl.BlockSpec((1, t, LANES), lambda i, g: (i, 0, g))
    return pl.pallas_call(
        functools.partial(_mix_cd_kernel, segments=segments),
        grid=(b, n_g),
        in_specs=[
            col(0), col(1), col(2), col(3),
            pl.BlockSpec((None, None, POOL_GROUP, POOL_GROUP), lambda i, g: (l, g, 0, 0)),
            pl.BlockSpec((None, 1, LANES), lambda i, g: (l, 0, g)),
            pl.BlockSpec((None, 3, LANES), lambda i, g: (l, 0, g)),
        ],
        out_specs=[out_spec, out_spec],
        out_shape=[jax.ShapeDtypeStruct((b, t, BRANCH_W), BF16)] * 2,
        scratch_shapes=[pltpu.VMEM((max(n_lat, t - n_lat) + 2 * POOL_HALO, LANES), F32)],
        compiler_params=pltpu.CompilerParams(
            dimension_semantics=("parallel", "parallel"), vmem_limit_bytes=VMEM_LIMIT),
        name="mix_cd",
    )(cd, cd, cd, cd, p["w_pool"], p["s_pool"], p["w_conv"])


def _merge_ffn_kernel(*refs, ctx_tile, split):
    ya_ref, yb_ref, yc_ref, yd_ref, gate_ref = refs[:5]
    if split:
        x_ref, xc_ref = refs[5:7]
        refs = refs[7:]
        x = jnp.where(pl.program_id(1) == ctx_tile, xc_ref[0], x_ref[0])
    else:
        x = refs[5][0]
        refs = refs[6:]
    mod_ref, wb_ref, wo_ref, g2_ref, w1_ref, w2_ref, o_ref = refs
    d = D_MODEL
    merged = None
    for n, y_ref in enumerate((ya_ref, yb_ref, yc_ref, yd_ref)):
        proj = _dot(y_ref[0], wb_ref[n])
        gate = jax.nn.sigmoid(gate_ref[0, :, n * d:(n + 1) * d].astype(F32))
        merged = gate * proj if merged is None else merged + gate * proj
    mix = _dot(merged.astype(BF16), wo_ref[...])
    x1 = x + mod_ref[:, 2 * d:3 * d] * mix

    y = _rms(x1) * g2_ref[...]
    h = (y * (1.0 + mod_ref[:, 4 * d:5 * d]) + mod_ref[:, 3 * d:4 * d]).astype(BF16)
    f = None
    for c0 in range(0, D_FF, d):
        a = jnp.maximum(_dot(h, w1_ref[:, c0:c0 + d]), 0.0)
        part = _dot((a * a).astype(BF16), w2_ref[c0:c0 + d, :])
        f = part if f is None else f + part
    o_ref[0] = x1 + mod_ref[:, 5 * d:6 * d] * f


def _merge_ffn(l, ys, gates, x_parts, mods, p, tq, n_lat, n_ctx, with_ctx):
    b, _, d = x_parts[0].shape
    t = n_lat + n_ctx
    ctx_tile = n_lat // tq
    split = len(x_parts) == 2
    row = lambda i, j: (i, j, 0)
    y_spec = pl.BlockSpec((1, tq, BRANCH_W), row)
    nq, out_rows = (t // tq, t) if with_ctx else (ctx_tile, n_lat)
    aliases = {5: 0} if (with_ctx and not split) else {}
    return pl.pallas_call(
        functools.partial(_merge_ffn_kernel, ctx_tile=ctx_tile, split=split),
        grid=(b, nq),
        in_specs=[
            y_spec, y_spec, y_spec, y_spec,
            pl.BlockSpec((1, tq, N_BRANCH * d), row),
        ] + _stream_specs(split, tq, d, ctx_tile) + [
            _mod_spec(l, b, ctx_tile),
            _layer_spec(l, (N_BRANCH, BRANCH_W, d)),
            _layer_spec(l, (d, d)),
            _layer_spec(l, (1, d)),
            _layer_spec(l, (d, D_FF)),
            _layer_spec(l, (D_FF, d)),
        ],
        out_specs=pl.BlockSpec((1, tq, d), row),
        out_shape=jax.ShapeDtypeStruct((b, out_rows, d), F32),
        input_output_aliases=aliases,
        compiler_params=pltpu.CompilerParams(
            dimension_semantics=("parallel", "parallel"), vmem_limit_bytes=VMEM_LIMIT),
        name="merge_ffn",
    )(*ys, gates, *x_parts, mods, p["w_branch"], p["w_o"], p["g_norm2"], p["w_ff1"], p["w_ff2"])


def _rope_tables(n_lat, n_ctx, rot_dim, lane_lo, period, total=LANES):
    rows = n_lat // GRID_W
    row = np.repeat(np.arange(rows, dtype=np.float64), GRID_W)
    col = np.tile(np.arange(GRID_W, dtype=np.float64), rows)
    n_freq = rot_dim // 4
    inv = ROPE_BASE ** (-np.arange(n_freq, dtype=np.float64) / n_freq)
    inv = inv.astype(np.float32).astype(np.float64)
    ang = np.concatenate([row[:, None] * inv, col[:, None] * inv], axis=-1)
    ang = ang.astype(np.float32).astype(np.float64)
    half = rot_dim // 2
    t = n_lat + n_ctx
    cos = np.ones((t, total), np.float32)
    s_left = np.zeros((t, total), np.float32)
    s_right = np.zeros((t, total), np.float32)
    starts = [lane_lo] if period == 0 else list(range(lane_lo, total, period))
    for s0 in starts:
        cos[:n_lat, s0:s0 + half] = np.cos(ang)
        cos[:n_lat, s0 + half:s0 + rot_dim] = np.cos(ang)
        s_left[:n_lat, s0:s0 + half] = -np.sin(ang)
        s_right[:n_lat, s0 + half:s0 + rot_dim] = np.sin(ang)
    return jnp.asarray(cos), jnp.asarray(s_left), jnp.asarray(s_right)


def _score_bounds(gq_a, gk_a, gq_b, gk_b):
    amax = lambda g: jnp.max(jnp.abs(g), axis=-1)
    bound_a = A_HD * amax(gq_a) * amax(gk_a) * (A_HD ** -0.5 * LOG2E)
    nq = jnp.sqrt(B_NOPE * amax(gq_b[:, :B_NOPE]) ** 2 + B_ROPE * amax(gq_b[:, B_NOPE:]) ** 2)
    nk = jnp.sqrt(B_NOPE * amax(gk_b[:, :B_NOPE]) ** 2 + B_ROPE * amax(gk_b[:, B_NOPE:]) ** 2)
    bound_b = nq * nk * (B_QK ** -0.5 * LOG2E)
    margin = 1.05
    return ((bound_a * margin < SAFE_LOG2).astype(jnp.int32),
            (bound_b * margin < SAFE_LOG2).astype(jnp.int32))


def _head_weights(w_in):
    cq0 = VA_OFF + 512
    wb = w_in[:, :, :W_IN_HEAD + B_ROPE].astype(BF16)
    kr_slot = jnp.pad(wb[:, :, W_IN_HEAD:], ((0, 0), (0, 0), (B_NOPE, LANES - B_QK)))
    return jnp.concatenate([wb[:, :, :cq0], wb[:, :, cq0 + B_QLORA:W_IN_HEAD],
                            wb[:, :, cq0:cq0 + B_QLORA], kr_slot], axis=-1)


def _pad_head_slots(v, width):
    lead = v.shape[:-1]
    v = v.reshape(lead + (-1, width))
    v = jnp.pad(v, [(0, 0)] * len(lead) + [(0, 0), (0, LANES - width)])
    return v.reshape(lead + (-1,))


def kernel(x, c, ctx, c_ctx, w_mod, b_mod, g_norm1, g_norm2, w_in, gq_a, gk_a, lam_a, g_sub_a,
           g_cq, w_uq, g_ckv, w_ukv, gq_b, gk_b, w_pool, s_pool, w_conv, w_branch, w_o,
           w_ff1, w_ff2):
    b, n_lat, d = x.shape
    n_ctx = ctx.shape[1]
    depth = w_mod.shape[0]
    tq = n_ctx
    assert d == D_MODEL and n_lat % ATT_SUB_BOUNDED == 0 and n_ctx % ATT_SUB_BOUNDED == 0
    assert n_lat % tq == 0 and tq % LANES == 0 and n_lat % GRID_W == 0

    rope_a = _rope_tables(n_lat, n_ctx, A_HD, 0, A_HD)
    rope_b = _rope_tables(n_lat, n_ctx, B_ROPE, B_NOPE, 0)

    mod_rows = -(-(b + 1) // 8) * 8
    cc = jnp.concatenate([c, c_ctx[None, :], jnp.zeros((mod_rows - b - 1, d), F32)], axis=0)
    mods = _modulation(cc, w_mod, b_mod).reshape(depth, mod_rows, 1, N_MOD * d)

    vec = lambda a: a[:, None, :]
    safe_a, safe_b = _score_bounds(gq_a, gk_a, gq_b, gk_b)
    p = {
        "safe_a": safe_a, "safe_b": safe_b,
        "w_head": _head_weights(w_in),
        "w_tail": w_in[:, :, W_IN_HEAD + B_ROPE:].astype(BF16),
        "w_uq": _pad_head_slots(w_uq, B_QK).astype(BF16),
        "w_ukv": w_ukv.astype(BF16),
        "w_pool": w_pool.astype(BF16),
        "w_branch": w_branch.astype(BF16),
        "w_o": w_o.astype(BF16),
        "w_ff1": w_ff1.astype(BF16),
        "w_ff2": w_ff2.astype(BF16),
        "g_norm1": vec(g_norm1), "g_norm2": vec(g_norm2),
        "g_cq": vec(g_cq), "g_ckv": vec(g_ckv),
        "gq_a": vec(jnp.tile(gq_a, (1, 2))), "gk_a": vec(jnp.tile(gk_a, (1, 2))),
        "gq_b": vec(jnp.pad(gq_b, ((0, 0), (0, LANES - B_QK)))),
        "gk_b": vec(jnp.pad(gk_b, ((0, 0), (0, LANES - B_QK)))),
        "lam_a": lam_a, "g_sub_a": vec(g_sub_a),
        "s_pool": vec(s_pool), "w_conv": w_conv,
    }

    x_parts = (x, ctx)
    for l in range(depth):
        last = l == depth - 1
        lam_init = 0.8 - 0.6 * math.exp(-0.3 * l)
        gates, cd, qa, ka, vxa, qb, kb, vxb = _inproj(
            l, x_parts, mods, p, rope_a, rope_b, tq, n_lat, n_ctx)
        ya = _attn_a(l, qa, ka, vxa, p, n_lat, lam_init)
        yb = _attn_b(l, qb, kb, vxb, p, n_lat)
        yc, yd = _mix_cd(l, cd, p, n_lat)
        x_parts = (_merge_ffn(l, (ya, yb, yc, yd), gates, x_parts, mods, p, tq, n_lat, n_ctx,
                              not last),)
    return x_parts[0]
```

```python
import functools
import math

import numpy as np
import jax
import jax.numpy as jnp
from jax import lax
from jax.experimental import pallas as pl
from jax.experimental.pallas import tpu as pltpu

F32 = jnp.float32
BF16 = jnp.bfloat16

D_MODEL = 1024
GRID_W = 64
ROPE_BASE = 10000.0
EPS = 1e-6
LOG2E = math.log2(math.e)

A_HEADS = 4
A_HD = 64
A_VD = 128
B_HEADS = 8
B_NOPE = 64
B_ROPE = 32
B_QK = B_NOPE + B_ROPE
B_VD = 64
B_QLORA = 384
B_KVLORA = 256
POOL_WINDOWS = (2, 4, 8, 16)
POOL_GROUP = 128
POOL_HALO = 8
BRANCH_W = 512
N_BRANCH = 4
D_FF = 4 * D_MODEL
N_MOD = 6

LANES = 128

QA_OFF = 0
KA_OFF = QA_OFF + 512
VA_OFF = KA_OFF + 512
CKV_OFF = VA_OFF + 512
CQ_OFF = CKV_OFF + B_KVLORA
KR_OFF = CQ_OFF + B_QLORA
HEAD_W = KR_OFF + LANES
W_IN_HEAD = 3 * 512 + B_QLORA + B_KVLORA
CD_OFF = 0
CD_W = 4 * 512
GATE_OFF = CD_OFF + CD_W
TAIL_W = GATE_OFF + N_BRANCH * D_MODEL
IN_FILL = 256
RAW_QB = CKV_OFF
RAW_KVB = RAW_QB + B_HEADS * LANES
RAW_KR = RAW_KVB + B_HEADS * LANES
RAW_W = RAW_KR + LANES

ATT_SUB = 128
ATT_SUB_BOUNDED = 256
SAFE_LOG2 = 40.0

VMEM_LIMIT = 56 * 1024 * 1024


def _dot(a, b):
    return jnp.dot(a, b, preferred_element_type=F32)


def _dot_nt(a, b):
    return lax.dot_general(a, b, (((1,), (1,)), ((), ())), preferred_element_type=F32)


def _lane_iota(n=LANES):
    return lax.broadcasted_iota(jnp.int32, (1, n), 1)


def _rms(x):
    return x * lax.rsqrt(jnp.mean(x * x, axis=-1, keepdims=True) + EPS)


def _rope(t, tabs, half):
    cos, s_left, s_right = tabs
    n = t.shape[-1]
    return (t * cos + pltpu.roll(t, n - half, 1) * s_left
            + pltpu.roll(t, half, 1) * s_right)


def _layer_spec(l, shape):
    nd = len(shape)
    return pl.BlockSpec((None,) + tuple(shape), lambda *_: (l,) + (0,) * nd,
                        pipeline_mode=pl.Buffered(1))


def _stream_specs(split, tq, d, ctx_tile):
    if split:
        return [pl.BlockSpec((1, tq, d), lambda i, j: (i, jnp.minimum(j, ctx_tile - 1), 0)),
                pl.BlockSpec((1, tq, d), lambda i, j: (i, 0, 0))]
    return [pl.BlockSpec((1, tq, d), lambda i, j: (i, j, 0))]


def _mod_spec(l, b, ctx_tile):
    return pl.BlockSpec((None, None, 1, N_MOD * D_MODEL),
                        lambda i, j: (l, jnp.where(j == ctx_tile, b, i), 0, 0))


def _mod_kernel(c_ref, w_ref, b_ref, o_ref):
    c = c_ref[...]
    h = (c * jax.nn.sigmoid(c)).astype(BF16)
    o_ref[...] = _dot(h, w_ref[...].astype(BF16)) + b_ref[...]


def _modulation(cc, w_mod, b_mod):
    depth, d, n = w_mod.shape
    rows = cc.shape[0]
    tn = 1536
    return pl.pallas_call(
        _mod_kernel,
        grid=(depth, n // tn),
        in_specs=[
            pl.BlockSpec((rows, d), lambda l, j: (0, 0)),
            pl.BlockSpec((None, d, tn), lambda l, j: (l, 0, j)),
            pl.BlockSpec((None, 1, tn), lambda l, j: (l, 0, j)),
        ],
        out_specs=pl.BlockSpec((None, rows, tn), lambda l, j: (l, 0, j)),
        out_shape=jax.ShapeDtypeStruct((depth, rows, n), F32),
        compiler_params=pltpu.CompilerParams(
            dimension_semantics=("parallel", "parallel"), vmem_limit_bytes=VMEM_LIMIT),
        name="modulation",
    )(cc, w_mod, b_mod.reshape(depth, 1, n))


def _inproj_kernel(*refs, nt, ctx_tile, n_tiles, split):
    s = pl.program_id(0)
    n_x = 2 if split else 1
    x_refs, refs = refs[:n_x], refs[n_x:]

    def load_x():
        if split:
            is_ctx = jnp.minimum(s, n_tiles - 1) % nt == ctx_tile
            return jnp.where(is_ctx, x_refs[1][0], x_refs[0][0])
        return x_refs[0][0]

    (mod_ref, g_ref, wh_ref, wt_ref, wuq_ref, wukv_ref, gcq_ref, gckv_ref,
     gqa_ref, gka_ref, gqb_ref, gkb_ref, ca_ref, la_ref, ra_ref, cb_ref, lb_ref, rb_ref,
     gate_ref, cd_ref, qa_ref, ka_ref, vxa_ref, qb_ref, kb_ref, vxb_ref,
     h0, h1, raw0, raw1) = refs
    d = D_MODEL
    tq = h0.shape[0]

    @pl.when(s == 0)
    def _warm_up():
        for ref in (h0, h1, raw0, raw1):
            ref[...] = jnp.zeros(ref.shape, ref.dtype)

    def stages(h_w, h_r, raw_w, raw_r):
        h = h_r[...]
        cq_kr = _dot(h, wh_ref[:, CQ_OFF:HEAD_W])
        ckv = _dot(h, wh_ref[:, CKV_OFF:CKV_OFF + B_KVLORA])
        cqn = (_rms(cq_kr[:, 0:B_QLORA]) * gcq_ref[...]).astype(BF16)
        ckvn = (_rms(ckv) * gckv_ref[...]).astype(BF16)

        def passthrough(o_ref, off, width):
            for c0 in range(0, width, IN_FILL):
                o_ref[0, :, c0:c0 + IN_FILL] = _dot(
                    h, wt_ref[:, off + c0:off + c0 + IN_FILL]).astype(o_ref.dtype)

        passthrough(cd_ref, CD_OFF, CD_W)
        raw_w[:, RAW_QB:RAW_QB + B_HEADS * LANES] = _dot(cqn, wuq_ref[...])
        raw_w[:, RAW_KVB:RAW_KVB + B_HEADS * LANES] = _dot(ckvn, wukv_ref[...])
        for c0 in range(0, CKV_OFF, IN_FILL):
            raw_w[:, c0:c0 + IN_FILL] = _dot(h, wh_ref[:, c0:c0 + IN_FILL])
        raw_w[:, RAW_KR:RAW_KR + LANES] = cq_kr[:, B_QLORA:B_QLORA + LANES]
        passthrough(gate_ref, GATE_OFF, N_BRANCH * d)

        lane = _lane_iota()
        lo = lane < A_HD
        nope = lane < B_NOPE
        rope_a = (ca_ref[...], la_ref[...], ra_ref[...])
        rope_b = (cb_ref[...], lb_ref[...], rb_ref[...])

        gq_b = gqb_ref[...] * (B_QK ** -0.5 * LOG2E)
        for hd in range(B_HEADS):
            q = raw_r[:, RAW_QB + hd * LANES:RAW_QB + (hd + 1) * LANES]
            q2 = q * q
            s_n = jnp.sum(jnp.where(nope, q2, 0.0), axis=-1, keepdims=True)
            s_r = jnp.sum(jnp.where(nope, 0.0, q2), axis=-1, keepdims=True)
            inv = jnp.where(nope, lax.rsqrt(s_n * (1.0 / B_NOPE) + EPS),
                            lax.rsqrt(s_r * (1.0 / B_ROPE) + EPS))
            qb_ref[0, :, hd * LANES:(hd + 1) * LANES] = _rope(
                q * inv * gq_b, rope_b, B_ROPE // 2).astype(BF16)

        kr = raw_r[:, RAW_KR:RAW_KR + LANES]
        krn = kr * lax.rsqrt(jnp.sum(kr * kr, axis=-1, keepdims=True) * (1.0 / B_ROPE) + EPS)
        krn = _rope(krn * jnp.where(nope, 0.0, gkb_ref[...]), rope_b, B_ROPE // 2)
        gk_nope = jnp.where(nope, gkb_ref[...], 0.0)
        for hd in range(B_HEADS):
            kv = raw_r[:, RAW_KVB + hd * LANES:RAW_KVB + (hd + 1) * LANES]
            ss = jnp.sum(jnp.where(nope, kv * kv, 0.0), axis=-1, keepdims=True)
            kn = kv * lax.rsqrt(ss * (1.0 / B_NOPE) + EPS) * gk_nope
            kb_ref[0, :, hd * LANES:(hd + 1) * LANES] = (kn + krn).astype(BF16)
            vxb_ref[0, :, hd * LANES:(hd + 1) * LANES] = jnp.where(nope, 1.0, kv).astype(BF16)

        def norm_halves(t):
            t2 = t * t
            s_lo = jnp.sum(jnp.where(lo, t2, 0.0), axis=-1, keepdims=True)
            s_hi = jnp.sum(jnp.where(lo, 0.0, t2), axis=-1, keepdims=True)
            return t * jnp.where(lo, lax.rsqrt(s_lo * (1.0 / A_HD) + EPS),
                                 lax.rsqrt(s_hi * (1.0 / A_HD) + EPS))

        gq_a = gqa_ref[...] * (A_HD ** -0.5 * LOG2E)
        for off, gain, o_ref in ((QA_OFF, gq_a, qa_ref), (KA_OFF, gka_ref[...], ka_ref)):
            for hd in range(A_HEADS):
                t = raw_r[:, off + hd * LANES:off + (hd + 1) * LANES]
                o_ref[0, :, hd * LANES:(hd + 1) * LANES] = _rope(
                    norm_halves(t) * gain, rope_a, A_HD // 2).astype(BF16)
        ones = jnp.ones((tq, A_VD), BF16)
        for hd in range(A_HEADS):
            vxa_ref[0, :, 2 * hd * A_VD:(2 * hd + 1) * A_VD] = raw_r[
                :, VA_OFF + hd * A_VD:VA_OFF + (hd + 1) * A_VD].astype(BF16)
            vxa_ref[0, :, (2 * hd + 1) * A_VD:(2 * hd + 2) * A_VD] = ones

        y = _rms(load_x()) * g_ref[...]
        h_w[...] = (y * (1.0 + mod_ref[:, d:2 * d]) + mod_ref[:, 0:d]).astype(BF16)

    pl.when(s % 2 == 0)(lambda: stages(h0, h1, raw1, raw0))
    pl.when(s % 2 == 1)(lambda: stages(h1, h0, raw0, raw1))


def _inproj(l, x_parts, mods, p, rope_a, rope_b, tq, n_lat, n_ctx):
    b, _, d = x_parts[0].shape
    t = n_lat + n_ctx
    nt = t // tq
    ctx_tile = n_lat // tq
    n_tiles = b * nt
    split = len(x_parts) == 2
    tile_a = lambda s: jnp.minimum(s, n_tiles - 1)
    tile_b = lambda s: jnp.clip(s - 1, 0, n_tiles - 1)
    tile_c = lambda s: jnp.clip(s - 2, 0, n_tiles - 1)
    tab = pl.BlockSpec((tq, LANES), lambda s: (tile_c(s) % nt, 0))
    vec = lambda n: _layer_spec(l, (1, n))
    if split:
        x_specs = [pl.BlockSpec((1, tq, d), lambda s: (tile_a(s) // nt,
                                                       jnp.minimum(tile_a(s) % nt, ctx_tile - 1), 0)),
                   pl.BlockSpec((1, tq, d), lambda s: (tile_a(s) // nt, 0, 0))]
    else:
        x_specs = [pl.BlockSpec((1, tq, d), lambda s: (tile_a(s) // nt, tile_a(s) % nt, 0))]

    def out(width, dtype, tile):
        return (pl.BlockSpec((1, tq, width), lambda s: (tile(s) // nt, tile(s) % nt, 0)),
                jax.ShapeDtypeStruct((b, t, width), dtype))

    outs = [out(N_BRANCH * d, BF16, tile_b), out(CD_W, BF16, tile_b),
            out(A_HEADS * LANES, BF16, tile_c), out(A_HEADS * LANES, BF16, tile_c),
            out(2 * A_HEADS * A_VD, BF16, tile_c), out(B_HEADS * LANES, BF16, tile_c),
            out(B_HEADS * LANES, BF16, tile_c), out(B_HEADS * LANES, BF16, tile_c)]
    return pl.pallas_call(
        functools.partial(_inproj_kernel, nt=nt, ctx_tile=ctx_tile, n_tiles=n_tiles, split=split),
        grid=(n_tiles + 2,),
        in_specs=x_specs + [
            pl.BlockSpec((None, None, 1, N_MOD * d),
                         lambda s: (l, jnp.where(tile_a(s) % nt == ctx_tile, b, tile_a(s) // nt), 0, 0)),
            vec(d),
            _layer_spec(l, (d, HEAD_W)),
            _layer_spec(l, (d, TAIL_W)),
            _layer_spec(l, (B_QLORA, B_HEADS * LANES)),
            _layer_spec(l, (B_KVLORA, B_HEADS * LANES)),
            vec(B_QLORA), vec(B_KVLORA), vec(LANES), vec(LANES), vec(LANES), vec(LANES),
            tab, tab, tab, tab, tab, tab,
        ],
        out_specs=[o[0] for o in outs],
        out_shape=[o[1] for o in outs],
        scratch_shapes=[pltpu.VMEM((tq, d), BF16), pltpu.VMEM((tq, d), BF16),
                        pltpu.VMEM((tq, RAW_W), F32), pltpu.VMEM((tq, RAW_W), F32)],
        compiler_params=pltpu.CompilerParams(
            dimension_semantics=("arbitrary",), vmem_limit_bytes=VMEM_LIMIT),
        name="inproj",
    )(*x_parts, mods, p["g_norm1"], p["w_head"], p["w_tail"], p["w_uq"], p["w_ukv"],
      p["g_cq"], p["g_ckv"], p["gq_a"], p["gk_a"], p["gq_b"], p["gk_b"], *rope_a, *rope_b)


def _softmax_pv(q, k, vx, shift_max):
    s = _dot_nt(q, k)
    if shift_max:
        s = s - jnp.max(s, axis=-1, keepdims=True)
    return _dot(jnp.exp2(s).astype(BF16), vx)


def _sweep(make_tile, bounded, n_lat, t_all):
    def run(shift_max, sub):
        tile = make_tile(shift_max, sub)
        for s0 in range(0, t_all, sub):
            tile(s0, 0 if s0 < n_lat else n_lat)

    pl.when(bounded)(lambda: run(False, ATT_SUB_BOUNDED))
    pl.when(jnp.logical_not(bounded))(lambda: run(True, ATT_SUB))


def _attn_a_kernel(safe_ref, q_ref, k_ref, vx_ref, lam_ref, gsub_ref, o_ref, *, l, n_lat, lam_init):
    t_all = k_ref.shape[1]
    lo = _lane_iota() < A_HD
    la = lam_ref[...].astype(F32)
    lam = (jnp.exp(jnp.sum(la[0:1] * la[1:2], axis=-1, keepdims=True))
           - jnp.exp(jnp.sum(la[2:3] * la[3:4], axis=-1, keepdims=True)) + lam_init)
    post = gsub_ref[...] * (1.0 - lam_init)

    def make_tile(shift_max, sub):
        def tile(r0, key_lo):
            q = q_ref[0, r0:r0 + sub, :]
            k = k_ref[0, key_lo:t_all, :]
            vx = vx_ref[0, key_lo:t_all, :]
            zero = jnp.zeros_like(q)
            r1 = _softmax_pv(jnp.where(lo, q, zero), k, vx, shift_max)
            r2 = _softmax_pv(jnp.where(lo, zero, q), k, vx, shift_max)
            o = (r1[:, 0:A_VD] / r1[:, A_VD:A_VD + 1]
                 - lam * (r2[:, 0:A_VD] / r2[:, A_VD:A_VD + 1]))
            o_ref[0, r0:r0 + sub, :] = (_rms(o) * post).astype(o_ref.dtype)
        return tile

    _sweep(make_tile, safe_ref[l] != 0, n_lat, t_all)


def _attn_a(l, qa, ka, vxa, p, n_lat, lam_init):
    b, t, _ = qa.shape
    head = lambda i, h: (i, 0, h)
    return pl.pallas_call(
        functools.partial(_attn_a_kernel, l=l, n_lat=n_lat, lam_init=lam_init),
        grid=(b, A_HEADS),
        in_specs=[
            pl.BlockSpec(memory_space=pltpu.SMEM),
            pl.BlockSpec((1, t, LANES), head),
            pl.BlockSpec((1, t, LANES), head),
            pl.BlockSpec((1, t, 2 * A_VD), head),
            _layer_spec(l, (4, A_HD)),
            _layer_spec(l, (1, LANES)),
        ],
        out_specs=pl.BlockSpec((1, t, LANES), head),
        out_shape=jax.ShapeDtypeStruct((b, t, BRANCH_W), BF16),
        compiler_params=pltpu.CompilerParams(
            dimension_semantics=("parallel", "parallel"), vmem_limit_bytes=VMEM_LIMIT),
        name="attn_a",
    )(p["safe_a"], qa, ka, vxa, p["lam_a"], p["g_sub_a"])


def _attn_b_kernel(safe_ref, q_ref, k_ref, vx_ref, o_ref, *, l, n_lat):
    t_all = k_ref.shape[1]
    nope = _lane_iota() < B_NOPE

    def make_tile(shift_max, sub):
        def tile(r0, key_lo):
            outs = []
            for hh in range(2):
                sl = slice(hh * LANES, (hh + 1) * LANES)
                r = _softmax_pv(q_ref[0, r0:r0 + sub, sl], k_ref[0, key_lo:t_all, sl],
                                vx_ref[0, key_lo:t_all, :], shift_max)[:, sl]
                outs.append(r / r[:, 0:1])
            o = jnp.where(nope, pltpu.roll(outs[0], B_VD, 1), outs[1])
            o_ref[0, r0:r0 + sub, :] = o.astype(o_ref.dtype)
        return tile

    _sweep(make_tile, safe_ref[l] != 0, n_lat, t_all)


def _attn_b(l, qb, kb, vxb, p, n_lat):
    b, t, _ = qb.shape
    pair = lambda i, h: (i, 0, h)
    spec = pl.BlockSpec((1, t, 2 * LANES), pair)
    return pl.pallas_call(
        functools.partial(_attn_b_kernel, l=l, n_lat=n_lat),
        grid=(b, B_HEADS // 2),
        in_specs=[pl.BlockSpec(memory_space=pltpu.SMEM), spec, spec, spec],
        out_specs=pl.BlockSpec((1, t, LANES), pair),
        out_shape=jax.ShapeDtypeStruct((b, t, BRANCH_W), BF16),
        compiler_params=pltpu.CompilerParams(
            dimension_semantics=("parallel", "parallel"), vmem_limit_bytes=VMEM_LIMIT),
        name="attn_b",
    )(p["safe_b"], qb, kb, vxb)


def _mix_cd_kernel(u_ref, pb_ref, pc_ref, pxx_ref, wp_ref, sp_ref, wc_ref,
                   yc_ref, yd_ref, pad_s, *, segments):
    g = pl.program_id(1)
    zeros_halo = jnp.zeros((POOL_HALO, LANES), F32)
    wc = wc_ref[...]

    edge = lax.broadcasted_iota(jnp.int32, (POOL_HALO, 1), 0)
    for start, length in segments:
        u = u_ref[0, start:start + length, :].astype(F32)
        pad_s[0:POOL_HALO, :] = zeros_halo
        pad_s[POOL_HALO:POOL_HALO + length, :] = u
        pad_s[POOL_HALO + length:2 * POOL_HALO + length, :] = zeros_halo

        for gi, w in enumerate(POOL_WINDOWS):
            @pl.when(g == gi)
            def _pool(w=w):
                acc = pad_s[POOL_HALO - w // 2:POOL_HALO - w // 2 + length, :]
                for j in range(1 - w // 2, w // 2):
                    acc = acc + pad_s[POOL_HALO + j:POOL_HALO + j + length, :]
                def clipped(r):
                    cnt = jnp.minimum(r + w // 2, length) - jnp.maximum(r - w // 2, 0)
                    return cnt.astype(F32)

                last = length - POOL_HALO
                mean = jnp.concatenate([acc[0:POOL_HALO] / clipped(edge),
                                        acc[POOL_HALO:last] * (1.0 / w),
                                        acc[last:length] / clipped(edge + last)], axis=0)
                dd = mean - u
                y = _dot(dd.astype(BF16), wp_ref[...]) * sp_ref[...]
                yc_ref[0, start:start + length, :] = y.astype(yc_ref.dtype)

        uu = (pc_ref[0, start:start + length, :].astype(F32)
              * pxx_ref[0, start:start + length, :].astype(F32))
        pad_s[POOL_HALO:POOL_HALO + length, :] = uu
        y = (pad_s[POOL_HALO - 1:POOL_HALO - 1 + length, :] * wc[0:1]
             + uu * wc[1:2]
             + pad_s[POOL_HALO + 1:POOL_HALO + 1 + length, :] * wc[2:3])
        yd_ref[0, start:start + length, :] = (
            pb_ref[0, start:start + length, :].astype(F32) * y).astype(yd_ref.dtype)


def _mix_cd(l, cd, p, n_lat):
    b, t, _ = cd.shape
    segments = ((0, n_lat), (n_lat, t - n_lat))
    n_g = len(POOL_WINDOWS)

    def col(k):
        return pl.BlockSpec((1, t, LANES), lambda i, g: (i, 0, k * n_g + g))

    out_spec = pl.BlockSpec((1, t, LANES), lambda i, g: (i, 0, g))
    return pl.pallas_call(
        functools.partial(_mix_cd_kernel, segments=segments),
        grid=(b, n_g),
        in_specs=[
            col(0), col(1), col(2), col(3),
            pl.BlockSpec((None, None, POOL_GROUP, POOL_GROUP), lambda i, g: (l, g, 0, 0)),
            pl.BlockSpec((None, 1, LANES), lambda i, g: (l, 0, g)),
            pl.BlockSpec((None, 3, LANES), lambda i, g: (l, 0, g)),
        ],
        out_specs=[out_spec, out_spec],
        out_shape=[jax.ShapeDtypeStruct((b, t, BRANCH_W), BF16)] * 2,
        scratch_shapes=[pltpu.VMEM((max(n_lat, t - n_lat) + 2 * POOL_HALO, LANES), F32)],
        compiler_params=pltpu.CompilerParams(
            dimension_semantics=("parallel", "parallel"), vmem_limit_bytes=VMEM_LIMIT),
        name="mix_cd",
    )(cd, cd, cd, cd, p["w_pool"], p["s_pool"], p["w_conv"])


def _merge_ffn_kernel(*refs, ctx_tile, split):
    ya_ref, yb_ref, yc_ref, yd_ref, gate_ref = refs[:5]
    if split:
        x_ref, xc_ref = refs[5:7]
        refs = refs[7:]
        x = jnp.where(pl.program_id(1) == ctx_tile, xc_ref[0], x_ref[0])
    else:
        x = refs[5][0]
        refs = refs[6:]
    mod_ref, wb_ref, wo_ref, g2_ref, w1_ref, w2_ref, o_ref = refs
    d = D_MODEL
    merged = None
    for n, y_ref in enumerate((ya_ref, yb_ref, yc_ref, yd_ref)):
        proj = _dot(y_ref[0], wb_ref[n])
        gate = jax.nn.sigmoid(gate_ref[0, :, n * d:(n + 1) * d].astype(F32))
        merged = gate * proj if merged is None else merged + gate * proj
    mix = _dot(merged.astype(BF16), wo_ref[...])
    x1 = x + mod_ref[:, 2 * d:3 * d] * mix

    y = _rms(x1) * g2_ref[...]
    h = (y * (1.0 + mod_ref[:, 4 * d:5 * d]) + mod_ref[:, 3 * d:4 * d]).astype(BF16)
    f = None
    for c0 in range(0, D_FF, d):
        a = jnp.maximum(_dot(h, w1_ref[:, c0:c0 + d]), 0.0)
        part = _dot((a * a).astype(BF16), w2_ref[c0:c0 + d, :])
        f = part if f is None else f + part
    o_ref[0] = x1 + mod_ref[:, 5 * d:6 * d] * f


def _merge_ffn(l, ys, gates, x_parts, mods, p, tq, n_lat, n_ctx, with_ctx):
    b, _, d = x_parts[0].shape
    t = n_lat + n_ctx
    ctx_tile = n_lat // tq
    split = len(x_parts) == 2
    row = lambda i, j: (i, j, 0)
    y_spec = pl.BlockSpec((1, tq, BRANCH_W), row)
    nq, out_rows = (t // tq, t) if with_ctx else (ctx_tile, n_lat)
    aliases = {5: 0} if (with_ctx and not split) else {}
    return pl.pallas_call(
        functools.partial(_merge_ffn_kernel, ctx_tile=ctx_tile, split=split),
        grid=(b, nq),
        in_specs=[
            y_spec, y_spec, y_spec, y_spec,
            pl.BlockSpec((1, tq, N_BRANCH * d), row),
        ] + _stream_specs(split, tq, d, ctx_tile) + [
            _mod_spec(l, b, ctx_tile),
            _layer_spec(l, (N_BRANCH, BRANCH_W, d)),
            _layer_spec(l, (d, d)),
            _layer_spec(l, (1, d)),
            _layer_spec(l, (d, D_FF)),
            _layer_spec(l, (D_FF, d)),
        ],
        out_specs=pl.BlockSpec((1, tq, d), row),
        out_shape=jax.ShapeDtypeStruct((b, out_rows, d), F32),
        input_output_aliases=aliases,
        compiler_params=pltpu.CompilerParams(
            dimension_semantics=("parallel", "parallel"), vmem_limit_bytes=VMEM_LIMIT),
        name="merge_ffn",
    )(*ys, gates, *x_parts, mods, p["w_branch"], p["w_o"], p["g_norm2"], p["w_ff1"], p["w_ff2"])


def _rope_tables(n_lat, n_ctx, rot_dim, lane_lo, period, total=LANES):
    rows = n_lat // GRID_W
    row = np.repeat(np.arange(rows, dtype=np.float64), GRID_W)
    col = np.tile(np.arange(GRID_W, dtype=np.float64), rows)
    n_freq = rot_dim // 4
    inv = ROPE_BASE ** (-np.arange(n_freq, dtype=np.float64) / n_freq)
    inv = inv.astype(np.float32).astype(np.float64)
    ang = np.concatenate([row[:, None] * inv, col[:, None] * inv], axis=-1)
    ang = ang.astype(np.float32).astype(np.float64)
    half = rot_dim // 2
    t = n_lat + n_ctx
    cos = np.ones((t, total), np.float32)
    s_left = np.zeros((t, total), np.float32)
    s_right = np.zeros((t, total), np.float32)
    starts = [lane_lo] if period == 0 else list(range(lane_lo, total, period))
    for s0 in starts:
        cos[:n_lat, s0:s0 + half] = np.cos(ang)
        cos[:n_lat, s0 + half:s0 + rot_dim] = np.cos(ang)
        s_left[:n_lat, s0:s0 + half] = -np.sin(ang)
        s_right[:n_lat, s0 + half:s0 + rot_dim] = np.sin(ang)
    return jnp.asarray(cos), jnp.asarray(s_left), jnp.asarray(s_right)


def _score_bounds(gq_a, gk_a, gq_b, gk_b):
    amax = lambda g: jnp.max(jnp.abs(g), axis=-1)
    bound_a = A_HD * amax(gq_a) * amax(gk_a) * (A_HD ** -0.5 * LOG2E)
    nq = jnp.sqrt(B_NOPE * amax(gq_b[:, :B_NOPE]) ** 2 + B_ROPE * amax(gq_b[:, B_NOPE:]) ** 2)
    nk = jnp.sqrt(B_NOPE * amax(gk_b[:, :B_NOPE]) ** 2 + B_ROPE * amax(gk_b[:, B_NOPE:]) ** 2)
    bound_b = nq * nk * (B_QK ** -0.5 * LOG2E)
    margin = 1.05
    return ((bound_a * margin < SAFE_LOG2).astype(jnp.int32),
            (bound_b * margin < SAFE_LOG2).astype(jnp.int32))


def _head_weights(w_in):
    cq0 = VA_OFF + 512
    wb = w_in[:, :, :W_IN_HEAD + B_ROPE].astype(BF16)
    kr_slot = jnp.pad(wb[:, :, W_IN_HEAD:], ((0, 0), (0, 0), (B_NOPE, LANES - B_QK)))
    return jnp.concatenate([wb[:, :, :cq0], wb[:, :, cq0 + B_QLORA:W_IN_HEAD],
                            wb[:, :, cq0:cq0 + B_QLORA], kr_slot], axis=-1)


def _pad_head_slots(v, width):
    lead = v.shape[:-1]
    v = v.reshape(lead + (-1, width))
    v = jnp.pad(v, [(0, 0)] * len(lead) + [(0, 0), (0, LANES - width)])
    return v.reshape(lead + (-1,))


def kernel(x, c, ctx, c_ctx, w_mod, b_mod, g_norm1, g_norm2, w_in, gq_a, gk_a, lam_a, g_sub_a,
           g_cq, w_uq, g_ckv, w_ukv, gq_b, gk_b, w_pool, s_pool, w_conv, w_branch, w_o,
           w_ff1, w_ff2):
    b, n_lat, d = x.shape
    n_ctx = ctx.shape[1]
    depth = w_mod.shape[0]
    tq = n_ctx
    assert d == D_MODEL and n_lat % ATT_SUB_BOUNDED == 0 and n_ctx % ATT_SUB_BOUNDED == 0
    assert n_lat % tq == 0 and tq % LANES == 0 and n_lat % GRID_W == 0

    rope_a = _rope_tables(n_lat, n_ctx, A_HD, 0, A_HD)
    rope_b = _rope_tables(n_lat, n_ctx, B_ROPE, B_NOPE, 0)

    mod_rows = -(-(b + 1) // 8) * 8
    cc = jnp.concatenate([c, c_ctx[None, :], jnp.zeros((mod_rows - b - 1, d), F32)], axis=0)
    mods = _modulation(cc, w_mod, b_mod).reshape(depth, mod_rows, 1, N_MOD * d)

    vec = lambda a: a[:, None, :]
    safe_a, safe_b = _score_bounds(gq_a, gk_a, gq_b, gk_b)
    p = {
        "safe_a": safe_a, "safe_b": safe_b,
        "w_head": _head_weights(w_in),
        "w_tail": w_in[:, :, W_IN_HEAD + B_ROPE:].astype(BF16),
        "w_uq": _pad_head_slots(w_uq, B_QK).astype(BF16),
        "w_ukv": w_ukv.astype(BF16),
        "w_pool": w_pool.astype(BF16),
        "w_branch": w_branch.astype(BF16),
        "w_o": w_o.astype(BF16),
        "w_ff1": w_ff1.astype(BF16),
        "w_ff2": w_ff2.astype(BF16),
        "g_norm1": vec(g_norm1), "g_norm2": vec(g_norm2),
        "g_cq": vec(g_cq), "g_ckv": vec(g_ckv),
        "gq_a": vec(jnp.tile(gq_a, (1, 2))), "gk_a": vec(jnp.tile(gk_a, (1, 2))),
        "gq_b": vec(jnp.pad(gq_b, ((0, 0), (0, LANES - B_QK)))),
        "gk_b": vec(jnp.pad(gk_b, ((0, 0), (0, LANES - B_QK)))),
        "lam_a": lam_a, "g_sub_a": vec(g_sub_a),
        "s_pool": vec(s_pool), "w_conv": w_conv,
    }

    x_parts = (x, ctx)
    for l in range(depth):
        last = l == depth - 1
        lam_init = 0.8 - 0.6 * math.exp(-0.3 * l)
        gates, cd, qa, ka, vxa, qb, kb, vxb = _inproj(
            l, x_parts, mods, p, rope_a, rope_b, tq, n_lat, n_ctx)
        ya = _attn_a(l, qa, ka, vxa, p, n_lat, lam_init)
        yb = _attn_b(l, qb, kb, vxb, p, n_lat)
        yc, yd = _mix_cd(l, cd, p, n_lat)
        x_parts = (_merge_ffn(l, (ya, yb, yc, yd), gates, x_parts, mods, p, tq, n_lat, n_ctx,
                              not last),)
    return x_parts[0]
```

```python
import functools
import math

import numpy as np
import jax
import jax.numpy as jnp
from jax import lax
from jax.experimental import pallas as pl
from jax.experimental.pallas import tpu as pltpu

F32 = jnp.float32
BF16 = jnp.bfloat16

D_MODEL = 1024
GRID_W = 64
ROPE_BASE = 10000.0
EPS = 1e-6
LOG2E = math.log2(math.e)

A_HEADS = 4
A_HD = 64
A_VD = 128
B_HEADS = 8
B_NOPE = 64
B_ROPE = 32
B_QK = B_NOPE + B_ROPE
B_VD = 64
B_QLORA = 384
B_KVLORA = 256
POOL_WINDOWS = (2, 4, 8, 16)
POOL_GROUP = 128
POOL_HALO = 8
BRANCH_W = 512
N_BRANCH = 4
D_FF = 4 * D_MODEL
N_MOD = 6

LANES = 128

QA_OFF = 0
KA_OFF = QA_OFF + 512
VA_OFF = KA_OFF + 512
CKV_OFF = VA_OFF + 512
CQ_OFF = CKV_OFF + B_KVLORA
KR_OFF = CQ_OFF + B_QLORA
HEAD_W = KR_OFF + LANES
W_IN_HEAD = 3 * 512 + B_QLORA + B_KVLORA
CD_OFF = 0
CD_W = 4 * 512
GATE_OFF = CD_OFF + CD_W
TAIL_W = GATE_OFF + N_BRANCH * D_MODEL
IN_FILL = 256
RAW_QB = CKV_OFF
RAW_KVB = RAW_QB + B_HEADS * LANES
RAW_KR = RAW_KVB + B_HEADS * LANES
RAW_W = RAW_KR + LANES

ATT_SUB = 128
ATT_SUB_BOUNDED = 256
SAFE_LOG2 = 40.0

VMEM_LIMIT = 56 * 1024 * 1024


def _dot(a, b):
    return jnp.dot(a, b, preferred_element_type=F32)


def _dot_nt(a, b):
    return lax.dot_general(a, b, (((1,), (1,)), ((), ())), preferred_element_type=F32)


def _lane_iota(n=LANES):
    return lax.broadcasted_iota(jnp.int32, (1, n), 1)


def _rms(x):
    return x * lax.rsqrt(jnp.mean(x * x, axis=-1, keepdims=True) + EPS)


def _rope(t, tabs, half):
    cos, s_left, s_right = tabs
    n = t.shape[-1]
    return (t * cos + pltpu.roll(t, n - half, 1) * s_left
            + pltpu.roll(t, half, 1) * s_right)


def _layer_spec(l, shape):
    nd = len(shape)
    return pl.BlockSpec((None,) + tuple(shape), lambda *_: (l,) + (0,) * nd,
                        pipeline_mode=pl.Buffered(1))


def _stream_specs(split, tq, d, ctx_tile):
    if split:
        return [pl.BlockSpec((1, tq, d), lambda i, j: (i, jnp.minimum(j, ctx_tile - 1), 0)),
                pl.BlockSpec((1, tq, d), lambda i, j: (i, 0, 0))]
    return [pl.BlockSpec((1, tq, d), lambda i, j: (i, j, 0))]


def _mod_spec(l, b, ctx_tile):
    return pl.BlockSpec((None, None, 1, N_MOD * D_MODEL),
                        lambda i, j: (l, jnp.where(j == ctx_tile, b, i), 0, 0))


def _mod_kernel(c_ref, w_ref, b_ref, o_ref):
    c = c_ref[...]
    h = (c * jax.nn.sigmoid(c)).astype(BF16)
    o_ref[...] = _dot(h, w_ref[...].astype(BF16)) + b_ref[...]


def _modulation(cc, w_mod, b_mod):
    depth, d, n = w_mod.shape
    rows = cc.shape[0]
    tn = 1536
    return pl.pallas_call(
        _mod_kernel,
        grid=(depth, n // tn),
        in_specs=[
            pl.BlockSpec((rows, d), lambda l, j: (0, 0)),
            pl.BlockSpec((None, d, tn), lambda l, j: (l, 0, j)),
            pl.BlockSpec((None, 1, tn), lambda l, j: (l, 0, j)),
        ],
        out_specs=pl.BlockSpec((None, rows, tn), lambda l, j: (l, 0, j)),
        out_shape=jax.ShapeDtypeStruct((depth, rows, n), F32),
        compiler_params=pltpu.CompilerParams(
            dimension_semantics=("parallel", "parallel"), vmem_limit_bytes=VMEM_LIMIT),
        name="modulation",
    )(cc, w_mod, b_mod.reshape(depth, 1, n))


def _inproj_kernel(*refs, nt, ctx_tile, n_tiles, split):
    s = pl.program_id(0)
    n_x = 2 if split else 1
    x_refs, refs = refs[:n_x], refs[n_x:]

    def load_x():
        if split:
            is_ctx = jnp.minimum(s, n_tiles - 1) % nt == ctx_tile
            return jnp.where(is_ctx, x_refs[1][0], x_refs[0][0])
        return x_refs[0][0]

    (mod_ref, g_ref, wh_ref, wt_ref, wuq_ref, wukv_ref, gcq_ref, gckv_ref,
     gqa_ref, gka_ref, gqb_ref, gkb_ref, ca_ref, la_ref, ra_ref, cb_ref, lb_ref, rb_ref,
     gate_ref, cd_ref, qa_ref, ka_ref, vxa_ref, qb_ref, kb_ref, vxb_ref,
     h0, h1, raw0, raw1) = refs
    d = D_MODEL
    tq = h0.shape[0]

    @pl.when(s == 0)
    def _warm_up():
        for ref in (h0, h1, raw0, raw1):
            ref[...] = jnp.zeros(ref.shape, ref.dtype)

    def stages(h_w, h_r, raw_w, raw_r):
        h = h_r[...]
        cq_kr = _dot(h, wh_ref[:, CQ_OFF:HEAD_W])
        ckv = _dot(h, wh_ref[:, CKV_OFF:CKV_OFF + B_KVLORA])
        cqn = (_rms(cq_kr[:, 0:B_QLORA]) * gcq_ref[...]).astype(BF16)
        ckvn = (_rms(ckv) * gckv_ref[...]).astype(BF16)

        def passthrough(o_ref, off, width):
            for c0 in range(0, width, IN_FILL):
                o_ref[0, :, c0:c0 + IN_FILL] = _dot(
                    h, wt_ref[:, off + c0:off + c0 + IN_FILL]).astype(o_ref.dtype)

        for c0 in range(0, CD_W, IN_FILL):
            res = _dot(h, wt_ref[:, CD_OFF + c0:CD_OFF + c0 + IN_FILL]).astype(cd_ref.dtype)
            for k in range(IN_FILL // LANES):
                cd_ref[0, c0 // LANES + k] = res[:, k * LANES:(k + 1) * LANES]
        raw_w[:, RAW_QB:RAW_QB + B_HEADS * LANES] = _dot(cqn, wuq_ref[...])
        raw_w[:, RAW_KVB:RAW_KVB + B_HEADS * LANES] = _dot(ckvn, wukv_ref[...])
        for c0 in range(0, CKV_OFF, IN_FILL):
            raw_w[:, c0:c0 + IN_FILL] = _dot(h, wh_ref[:, c0:c0 + IN_FILL])
        raw_w[:, RAW_KR:RAW_KR + LANES] = cq_kr[:, B_QLORA:B_QLORA + LANES]
        passthrough(gate_ref, GATE_OFF, N_BRANCH * d)

        lane = _lane_iota()
        lo = lane < A_HD
        nope = lane < B_NOPE
        rope_a = (ca_ref[...], la_ref[...], ra_ref[...])
        rope_b = (cb_ref[...], lb_ref[...], rb_ref[...])

        gq_b = gqb_ref[...] * (B_QK ** -0.5 * LOG2E)
        for hd in range(B_HEADS):
            q = raw_r[:, RAW_QB + hd * LANES:RAW_QB + (hd + 1) * LANES]
            q2 = q * q
            s_n = jnp.sum(jnp.where(nope, q2, 0.0), axis=-1, keepdims=True)
            s_r = jnp.sum(jnp.where(nope, 0.0, q2), axis=-1, keepdims=True)
            inv = jnp.where(nope, lax.rsqrt(s_n * (1.0 / B_NOPE) + EPS),
                            lax.rsqrt(s_r * (1.0 / B_ROPE) + EPS))
            qb_ref[0, :, hd * LANES:(hd + 1) * LANES] = _rope(
                q * inv * gq_b, rope_b, B_ROPE // 2).astype(BF16)

        kr = raw_r[:, RAW_KR:RAW_KR + LANES]
        krn = kr * lax.rsqrt(jnp.sum(kr * kr, axis=-1, keepdims=True) * (1.0 / B_ROPE) + EPS)
        krn = _rope(krn * jnp.where(nope, 0.0, gkb_ref[...]), rope_b, B_ROPE // 2)
        gk_nope = jnp.where(nope, gkb_ref[...], 0.0)
        for hd in range(B_HEADS):
            kv = raw_r[:, RAW_KVB + hd * LANES:RAW_KVB + (hd + 1) * LANES]
            ss = jnp.sum(jnp.where(nope, kv * kv, 0.0), axis=-1, keepdims=True)
            kn = kv * lax.rsqrt(ss * (1.0 / B_NOPE) + EPS) * gk_nope
            kb_ref[0, :, hd * LANES:(hd + 1) * LANES] = (kn + krn).astype(BF16)
            vxb_ref[0, :, hd * LANES:(hd + 1) * LANES] = jnp.where(nope, 1.0, kv).astype(BF16)

        def norm_halves(t):
            t2 = t * t
            s_lo = jnp.sum(jnp.where(lo, t2, 0.0), axis=-1, keepdims=True)
            s_hi = jnp.sum(jnp.where(lo, 0.0, t2), axis=-1, keepdims=True)
            return t * jnp.where(lo, lax.rsqrt(s_lo * (1.0 / A_HD) + EPS),
                                 lax.rsqrt(s_hi * (1.0 / A_HD) + EPS))

        gq_a = gqa_ref[...] * (A_HD ** -0.5 * LOG2E)
        for off, gain, o_ref in ((QA_OFF, gq_a, qa_ref), (KA_OFF, gka_ref[...], ka_ref)):
            for hd in range(A_HEADS):
                t = raw_r[:, off + hd * LANES:off + (hd + 1) * LANES]
                o_ref[0, :, hd * LANES:(hd + 1) * LANES] = _rope(
                    norm_halves(t) * gain, rope_a, A_HD // 2).astype(BF16)
        ones = jnp.ones((tq, A_VD), BF16)
        for hd in range(A_HEADS):
            vxa_ref[0, :, 2 * hd * A_VD:(2 * hd + 1) * A_VD] = raw_r[
                :, VA_OFF + hd * A_VD:VA_OFF + (hd + 1) * A_VD].astype(BF16)
            vxa_ref[0, :, (2 * hd + 1) * A_VD:(2 * hd + 2) * A_VD] = ones

        y = _rms(load_x()) * g_ref[...]
        h_w[...] = (y * (1.0 + mod_ref[:, d:2 * d]) + mod_ref[:, 0:d]).astype(BF16)

    pl.when(s % 2 == 0)(lambda: stages(h0, h1, raw1, raw0))
    pl.when(s % 2 == 1)(lambda: stages(h1, h0, raw0, raw1))


def _inproj(l, x_parts, mods, p, rope_a, rope_b, tq, n_lat, n_ctx):
    b, _, d = x_parts[0].shape
    t = n_lat + n_ctx
    nt = t // tq
    ctx_tile = n_lat // tq
    n_tiles = b * nt
    split = len(x_parts) == 2
    tile_a = lambda s: jnp.minimum(s, n_tiles - 1)
    tile_b = lambda s: jnp.clip(s - 1, 0, n_tiles - 1)
    tile_c = lambda s: jnp.clip(s - 2, 0, n_tiles - 1)
    tab = pl.BlockSpec((tq, LANES), lambda s: (tile_c(s) % nt, 0))
    vec = lambda n: _layer_spec(l, (1, n))
    if split:
        x_specs = [pl.BlockSpec((1, tq, d), lambda s: (tile_a(s) // nt,
                                                       jnp.minimum(tile_a(s) % nt, ctx_tile - 1), 0)),
                   pl.BlockSpec((1, tq, d), lambda s: (tile_a(s) // nt, 0, 0))]
    else:
        x_specs = [pl.BlockSpec((1, tq, d), lambda s: (tile_a(s) // nt, tile_a(s) % nt, 0))]

    def out(width, dtype, tile):
        return (pl.BlockSpec((1, tq, width), lambda s: (tile(s) // nt, tile(s) % nt, 0)),
                jax.ShapeDtypeStruct((b, t, width), dtype))

    cd_out = (pl.BlockSpec((1, CD_W // LANES, tq, LANES),
                           lambda s: (tile_b(s) // nt, 0, tile_b(s) % nt, 0)),
              jax.ShapeDtypeStruct((b, CD_W // LANES, t, LANES), BF16))
    outs = [out(N_BRANCH * d, BF16, tile_b), cd_out,
            out(A_HEADS * LANES, BF16, tile_c), out(A_HEADS * LANES, BF16, tile_c),
            out(2 * A_HEADS * A_VD, BF16, tile_c), out(B_HEADS * LANES, BF16, tile_c),
            out(B_HEADS * LANES, BF16, tile_c), out(B_HEADS * LANES, BF16, tile_c)]
    return pl.pallas_call(
        functools.partial(_inproj_kernel, nt=nt, ctx_tile=ctx_tile, n_tiles=n_tiles, split=split),
        grid=(n_tiles + 2,),
        in_specs=x_specs + [
            pl.BlockSpec((None, None, 1, N_MOD * d),
                         lambda s: (l, jnp.where(tile_a(s) % nt == ctx_tile, b, tile_a(s) // nt), 0, 0)),
            vec(d),
            _layer_spec(l, (d, HEAD_W)),
            _layer_spec(l, (d, TAIL_W)),
            _layer_spec(l, (B_QLORA, B_HEADS * LANES)),
            _layer_spec(l, (B_KVLORA, B_HEADS * LANES)),
            vec(B_QLORA), vec(B_KVLORA), vec(LANES), vec(LANES), vec(LANES), vec(LANES),
            tab, tab, tab, tab, tab, tab,
        ],
        out_specs=[o[0] for o in outs],
        out_shape=[o[1] for o in outs],
        scratch_shapes=[pltpu.VMEM((tq, d), BF16), pltpu.VMEM((tq, d), BF16),
                        pltpu.VMEM((tq, RAW_W), F32), pltpu.VMEM((tq, RAW_W), F32)],
        compiler_params=pltpu.CompilerParams(
            dimension_semantics=("arbitrary",), vmem_limit_bytes=VMEM_LIMIT),
        name="inproj",
    )(*x_parts, mods, p["g_norm1"], p["w_head"], p["w_tail"], p["w_uq"], p["w_ukv"],
      p["g_cq"], p["g_ckv"], p["gq_a"], p["gk_a"], p["gq_b"], p["gk_b"], *rope_a, *rope_b)


def _softmax_pv(q, k, vx, shift_max):
    s = _dot_nt(q, k)
    if shift_max:
        s = s - jnp.max(s, axis=-1, keepdims=True)
    return _dot(jnp.exp2(s).astype(BF16), vx)


def _sweep(make_tile, bounded, n_lat, t_all):
    def run(shift_max, sub):
        tile = make_tile(shift_max, sub)
        for s0 in range(0, t_all, sub):
            tile(s0, 0 if s0 < n_lat else n_lat)

    pl.when(bounded)(lambda: run(False, ATT_SUB_BOUNDED))
    pl.when(jnp.logical_not(bounded))(lambda: run(True, ATT_SUB))


def _attn_a_kernel(safe_ref, q_ref, k_ref, vx_ref, lam_ref, gsub_ref, o_ref, *, l, n_lat, lam_init):
    t_all = k_ref.shape[1]
    lo = _lane_iota() < A_HD
    la = lam_ref[...].astype(F32)
    lam = (jnp.exp(jnp.sum(la[0:1] * la[1:2], axis=-1, keepdims=True))
           - jnp.exp(jnp.sum(la[2:3] * la[3:4], axis=-1, keepdims=True)) + lam_init)
    post = gsub_ref[...] * (1.0 - lam_init)

    def make_tile(shift_max, sub):
        def tile(r0, key_lo):
            q = q_ref[0, r0:r0 + sub, :]
            k = k_ref[0, key_lo:t_all, :]
            vx = vx_ref[0, key_lo:t_all, :]
            zero = jnp.zeros_like(q)
            r1 = _softmax_pv(jnp.where(lo, q, zero), k, vx, shift_max)
            r2 = _softmax_pv(jnp.where(lo, zero, q), k, vx, shift_max)
            o = (r1[:, 0:A_VD] / r1[:, A_VD:A_VD + 1]
                 - lam * (r2[:, 0:A_VD] / r2[:, A_VD:A_VD + 1]))
            o_ref[0, r0:r0 + sub, :] = (_rms(o) * post).astype(o_ref.dtype)
        return tile

    _sweep(make_tile, safe_ref[l] != 0, n_lat, t_all)


def _attn_a(l, qa, ka, vxa, p, n_lat, lam_init):
    b, t, _ = qa.shape
    head = lambda i, h: (i, 0, h)
    return pl.pallas_call(
        functools.partial(_attn_a_kernel, l=l, n_lat=n_lat, lam_init=lam_init),
        grid=(b, A_HEADS),
        in_specs=[
            pl.BlockSpec(memory_space=pltpu.SMEM),
            pl.BlockSpec((1, t, LANES), head),
            pl.BlockSpec((1, t, LANES), head),
            pl.BlockSpec((1, t, 2 * A_VD), head),
            _layer_spec(l, (4, A_HD)),
            _layer_spec(l, (1, LANES)),
        ],
        out_specs=pl.BlockSpec((1, t, LANES), head),
        out_shape=jax.ShapeDtypeStruct((b, t, BRANCH_W), BF16),
        compiler_params=pltpu.CompilerParams(
            dimension_semantics=("parallel", "parallel"), vmem_limit_bytes=VMEM_LIMIT),
        name="attn_a",
    )(p["safe_a"], qa, ka, vxa, p["lam_a"], p["g_sub_a"])


def _attn_b_kernel(safe_ref, q_ref, k_ref, vx_ref, o_ref, *, l, n_lat):
    t_all = k_ref.shape[1]
    nope = _lane_iota() < B_NOPE

    def make_tile(shift_max, sub):
        def tile(r0, key_lo):
            outs = []
            for hh in range(2):
                sl = slice(hh * LANES, (hh + 1) * LANES)
                r = _softmax_pv(q_ref[0, r0:r0 + sub, sl], k_ref[0, key_lo:t_all, sl],
                                vx_ref[0, key_lo:t_all, :], shift_max)[:, sl]
                outs.append(r / r[:, 0:1])
            o = jnp.where(nope, pltpu.roll(outs[0], B_VD, 1), outs[1])
            o_ref[0, r0:r0 + sub, :] = o.astype(o_ref.dtype)
        return tile

    _sweep(make_tile, safe_ref[l] != 0, n_lat, t_all)


def _attn_b(l, qb, kb, vxb, p, n_lat):
    b, t, _ = qb.shape
    pair = lambda i, h: (i, 0, h)
    spec = pl.BlockSpec((1, t, 2 * LANES), pair)
    return pl.pallas_call(
        functools.partial(_attn_b_kernel, l=l, n_lat=n_lat),
        grid=(b, B_HEADS // 2),
        in_specs=[pl.BlockSpec(memory_space=pltpu.SMEM), spec, spec, spec],
        out_specs=pl.BlockSpec((1, t, LANES), pair),
        out_shape=jax.ShapeDtypeStruct((b, t, BRANCH_W), BF16),
        compiler_params=pltpu.CompilerParams(
            dimension_semantics=("parallel", "parallel"), vmem_limit_bytes=VMEM_LIMIT),
        name="attn_b",
    )(p["safe_b"], qb, kb, vxb)


def _mix_cd_kernel(u_ref, pb_ref, pc_ref, pxx_ref, wp_ref, sp_ref, wc_ref,
                   yc_ref, yd_ref, pad_s, *, segments):
    g = pl.program_id(1)
    zeros_halo = jnp.zeros((POOL_HALO, LANES), F32)
    wc = wc_ref[...]

    edge = lax.broadcasted_iota(jnp.int32, (POOL_HALO, 1), 0)
    for start, length in segments:
        u = u_ref[0, start:start + length, :].astype(F32)
        pad_s[0:POOL_HALO, :] = zeros_halo
        pad_s[POOL_HALO:POOL_HALO + length, :] = u
        pad_s[POOL_HALO + length:2 * POOL_HALO + length, :] = zeros_halo

        for gi, w in enumerate(POOL_WINDOWS):
            @pl.when(g == gi)
            def _pool(w=w):
                acc = pad_s[POOL_HALO - w // 2:POOL_HALO - w // 2 + length, :]
                for j in range(1 - w // 2, w // 2):
                    acc = acc + pad_s[POOL_HALO + j:POOL_HALO + j + length, :]
                def clipped(r):
                    cnt = jnp.minimum(r + w // 2, length) - jnp.maximum(r - w // 2, 0)
                    return cnt.astype(F32)

                last = length - POOL_HALO
                mean = jnp.concatenate([acc[0:POOL_HALO] / clipped(edge),
                                        acc[POOL_HALO:last] * (1.0 / w),
                                        acc[last:length] / clipped(edge + last)], axis=0)
                dd = mean - u
                y = _dot(dd.astype(BF16), wp_ref[...]) * sp_ref[...]
                yc_ref[0, start:start + length, :] = y.astype(yc_ref.dtype)

        uu = (pc_ref[0, start:start + length, :].astype(F32)
              * pxx_ref[0, start:start + length, :].astype(F32))
        pad_s[POOL_HALO:POOL_HALO + length, :] = uu
        y = (pad_s[POOL_HALO - 1:POOL_HALO - 1 + length, :] * wc[0:1]
             + uu * wc[1:2]
             + pad_s[POOL_HALO + 1:POOL_HALO + 1 + length, :] * wc[2:3])
        yd_ref[0, start:start + length, :] = (
            pb_ref[0, start:start + length, :].astype(F32) * y).astype(yd_ref.dtype)


def _mix_cd(l, cd, p, n_lat):
    b, _, t, _ = cd.shape
    segments = ((0, n_lat), (n_lat, t - n_lat))
    n_g = len(POOL_WINDOWS)

    def col(k):
        return pl.BlockSpec((None, 1, t, LANES), lambda i, g: (i, k * n_g + g, 0, 0))

    out_spec = pl.BlockSpec((1, t, LANES), lambda i, g: (i, 0, g))
    return pl.pallas_call(
        functools.partial(_mix_cd_kernel, segments=segments),
        grid=(b, n_g),
        in_specs=[
            col(0), col(1), col(2), col(3),
            pl.BlockSpec((None, None, POOL_GROUP, POOL_GROUP), lambda i, g: (l, g, 0, 0)),
            pl.BlockSpec((None, 1, LANES), lambda i, g: (l, 0, g)),
            pl.BlockSpec((None, 3, LANES), lambda i, g: (l, 0, g)),
        ],
        out_specs=[out_spec, out_spec],
        out_shape=[jax.ShapeDtypeStruct((b, t, BRANCH_W), BF16)] * 2,
        scratch_shapes=[pltpu.VMEM((max(n_lat, t - n_lat) + 2 * POOL_HALO, LANES), F32)],
        compiler_params=pltpu.CompilerParams(
            dimension_semantics=("parallel", "parallel"), vmem_limit_bytes=VMEM_LIMIT),
        name="mix_cd",
    )(cd, cd, cd, cd, p["w_pool"], p["s_pool"], p["w_conv"])


def _merge_ffn_kernel(*refs, ctx_tile, split):
    ya_ref, yb_ref, yc_ref, yd_ref, gate_ref = refs[:5]
    if split:
        x_ref, xc_ref = refs[5:7]
        refs = refs[7:]
        x = jnp.where(pl.program_id(1) == ctx_tile, xc_ref[0], x_ref[0])
    else:
        x = refs[5][0]
        refs = refs[6:]
    mod_ref, wb_ref, wo_ref, g2_ref, w1_ref, w2_ref, o_ref = refs
    d = D_MODEL
    merged = None
    for n, y_ref in enumerate((ya_ref, yb_ref, yc_ref, yd_ref)):
        proj = _dot(y_ref[0], wb_ref[n])
        gate = jax.nn.sigmoid(gate_ref[0, :, n * d:(n + 1) * d].astype(F32))
        merged = gate * proj if merged is None else merged + gate * proj
    mix = _dot(merged.astype(BF16), wo_ref[...])
    x1 = x + mod_ref[:, 2 * d:3 * d] * mix

    y = _rms(x1) * g2_ref[...]
    h = (y * (1.0 + mod_ref[:, 4 * d:5 * d]) + mod_ref[:, 3 * d:4 * d]).astype(BF16)
    f = None
    for c0 in range(0, D_FF, d):
        a = jnp.maximum(_dot(h, w1_ref[:, c0:c0 + d]), 0.0)
        part = _dot((a * a).astype(BF16), w2_ref[c0:c0 + d, :])
        f = part if f is None else f + part
    o_ref[0] = x1 + mod_ref[:, 5 * d:6 * d] * f


def _merge_ffn(l, ys, gates, x_parts, mods, p, tq, n_lat, n_ctx, with_ctx):
    b, _, d = x_parts[0].shape
    t = n_lat + n_ctx
    ctx_tile = n_lat // tq
    split = len(x_parts) == 2
    row = lambda i, j: (i, j, 0)
    y_spec = pl.BlockSpec((1, tq, BRANCH_W), row)
    nq, out_rows = (t // tq, t) if with_ctx else (ctx_tile, n_lat)
    aliases = {5: 0} if (with_ctx and not split) else {}
    return pl.pallas_call(
        functools.partial(_merge_ffn_kernel, ctx_tile=ctx_tile, split=split),
        grid=(b, nq),
        in_specs=[
            y_spec, y_spec, y_spec, y_spec,
            pl.BlockSpec((1, tq, N_BRANCH * d), row),
        ] + _stream_specs(split, tq, d, ctx_tile) + [
            _mod_spec(l, b, ctx_tile),
            _layer_spec(l, (N_BRANCH, BRANCH_W, d)),
            _layer_spec(l, (d, d)),
            _layer_spec(l, (1, d)),
            _layer_spec(l, (d, D_FF)),
            _layer_spec(l, (D_FF, d)),
        ],
        out_specs=pl.BlockSpec((1, tq, d), row),
        out_shape=jax.ShapeDtypeStruct((b, out_rows, d), F32),
        input_output_aliases=aliases,
        compiler_params=pltpu.CompilerParams(
            dimension_semantics=("parallel", "parallel"), vmem_limit_bytes=VMEM_LIMIT),
        name="merge_ffn",
    )(*ys, gates, *x_parts, mods, p["w_branch"], p["w_o"], p["g_norm2"], p["w_ff1"], p["w_ff2"])


def _rope_tables(n_lat, n_ctx, rot_dim, lane_lo, period, total=LANES):
    rows = n_lat // GRID_W
    row = np.repeat(np.arange(rows, dtype=np.float64), GRID_W)
    col = np.tile(np.arange(GRID_W, dtype=np.float64), rows)
    n_freq = rot_dim // 4
    inv = ROPE_BASE ** (-np.arange(n_freq, dtype=np.float64) / n_freq)
    inv = inv.astype(np.float32).astype(np.float64)
    ang = np.concatenate([row[:, None] * inv, col[:, None] * inv], axis=-1)
    ang = ang.astype(np.float32).astype(np.float64)
    half = rot_dim // 2
    t = n_lat + n_ctx
    cos = np.ones((t, total), np.float32)
    s_left = np.zeros((t, total), np.float32)
    s_right = np.zeros((t, total), np.float32)
    starts = [lane_lo] if period == 0 else list(range(lane_lo, total, period))
    for s0 in starts:
        cos[:n_lat, s0:s0 + half] = np.cos(ang)
        cos[:n_lat, s0 + half:s0 + rot_dim] = np.cos(ang)
        s_left[:n_lat, s0:s0 + half] = -np.sin(ang)
        s_right[:n_lat, s0 + half:s0 + rot_dim] = np.sin(ang)
    return jnp.asarray(cos), jnp.asarray(s_left), jnp.asarray(s_right)


def _score_bounds(gq_a, gk_a, gq_b, gk_b):
    amax = lambda g: jnp.max(jnp.abs(g), axis=-1)
    bound_a = A_HD * amax(gq_a) * amax(gk_a) * (A_HD ** -0.5 * LOG2E)
    nq = jnp.sqrt(B_NOPE * amax(gq_b[:, :B_NOPE]) ** 2 + B_ROPE * amax(gq_b[:, B_NOPE:]) ** 2)
    nk = jnp.sqrt(B_NOPE * amax(gk_b[:, :B_NOPE]) ** 2 + B_ROPE * amax(gk_b[:, B_NOPE:]) ** 2)
    bound_b = nq * nk * (B_QK ** -0.5 * LOG2E)
    margin = 1.05
    return ((bound_a * margin < SAFE_LOG2).astype(jnp.int32),
            (bound_b * margin < SAFE_LOG2).astype(jnp.int32))


def _head_weights(w_in):
    cq0 = VA_OFF + 512
    wb = w_in[:, :, :W_IN_HEAD + B_ROPE].astype(BF16)
    kr_slot = jnp.pad(wb[:, :, W_IN_HEAD:], ((0, 0), (0, 0), (B_NOPE, LANES - B_QK)))
    return jnp.concatenate([wb[:, :, :cq0], wb[:, :, cq0 + B_QLORA:W_IN_HEAD],
                            wb[:, :, cq0:cq0 + B_QLORA], kr_slot], axis=-1)


def _pad_head_slots(v, width):
    lead = v.shape[:-1]
    v = v.reshape(lead + (-1, width))
    v = jnp.pad(v, [(0, 0)] * len(lead) + [(0, 0), (0, LANES - width)])
    return v.reshape(lead + (-1,))


def kernel(x, c, ctx, c_ctx, w_mod, b_mod, g_norm1, g_norm2, w_in, gq_a, gk_a, lam_a, g_sub_a,
           g_cq, w_uq, g_ckv, w_ukv, gq_b, gk_b, w_pool, s_pool, w_conv, w_branch, w_o,
           w_ff1, w_ff2):
    b, n_lat, d = x.shape
    n_ctx = ctx.shape[1]
    depth = w_mod.shape[0]
    tq = n_ctx
    assert d == D_MODEL and n_lat % ATT_SUB_BOUNDED == 0 and n_ctx % ATT_SUB_BOUNDED == 0
    assert n_lat % tq == 0 and tq % LANES == 0 and n_lat % GRID_W == 0

    rope_a = _rope_tables(n_lat, n_ctx, A_HD, 0, A_HD)
    rope_b = _rope_tables(n_lat, n_ctx, B_ROPE, B_NOPE, 0)

    mod_rows = -(-(b + 1) // 8) * 8
    cc = jnp.concatenate([c, c_ctx[None, :], jnp.zeros((mod_rows - b - 1, d), F32)], axis=0)
    mods = _modulation(cc, w_mod, b_mod).reshape(depth, mod_rows, 1, N_MOD * d)

    vec = lambda a: a[:, None, :]
    safe_a, safe_b = _score_bounds(gq_a, gk_a, gq_b, gk_b)
    p = {
        "safe_a": safe_a, "safe_b": safe_b,
        "w_head": _head_weights(w_in),
        "w_tail": w_in[:, :, W_IN_HEAD + B_ROPE:].astype(BF16),
        "w_uq": _pad_head_slots(w_uq, B_QK).astype(BF16),
        "w_ukv": w_ukv.astype(BF16),
        "w_pool": w_pool.astype(BF16),
        "w_branch": w_branch.astype(BF16),
        "w_o": w_o.astype(BF16),
        "w_ff1": w_ff1.astype(BF16),
        "w_ff2": w_ff2.astype(BF16),
        "g_norm1": vec(g_norm1), "g_norm2": vec(g_norm2),
        "g_cq": vec(g_cq), "g_ckv": vec(g_ckv),
        "gq_a": vec(jnp.tile(gq_a, (1, 2))), "gk_a": vec(jnp.tile(gk_a, (1, 2))),
        "gq_b": vec(jnp.pad(gq_b, ((0, 0), (0, LANES - B_QK)))),
        "gk_b": vec(jnp.pad(gk_b, ((0, 0), (0, LANES - B_QK)))),
        "lam_a": lam_a, "g_sub_a": vec(g_sub_a),
        "s_pool": vec(s_pool), "w_conv": w_conv,
    }

    x_parts = (x, ctx)
    for l in range(depth):
        last = l == depth - 1
        lam_init = 0.8 - 0.6 * math.exp(-0.3 * l)
        gates, cd, qa, ka, vxa, qb, kb, vxb = _inproj(
            l, x_parts, mods, p, rope_a, rope_b, tq, n_lat, n_ctx)
        ya = _attn_a(l, qa, ka, vxa, p, n_lat, lam_init)
        yb = _attn_b(l, qb, kb, vxb, p, n_lat)
        yc, yd = _mix_cd(l, cd, p, n_lat)
        x_parts = (_merge_ffn(l, (ya, yb, yc, yd), gates, x_parts, mods, p, tq, n_lat, n_ctx,
                              not last),)
    return x_parts[0]
```

```python
import functools
import math

import numpy as np
import jax
import jax.numpy as jnp
from jax import lax
from jax.experimental import pallas as pl
from jax.experimental.pallas import tpu as pltpu

F32 = jnp.float32
BF16 = jnp.bfloat16

D_MODEL = 1024
GRID_W = 64
ROPE_BASE = 10000.0
EPS = 1e-6
LOG2E = math.log2(math.e)

A_HEADS = 4
A_HD = 64
A_VD = 128
B_HEADS = 8
B_NOPE = 64
B_ROPE = 32
B_QK = B_NOPE + B_ROPE
B_VD = 64
B_QLORA = 384
B_KVLORA = 256
POOL_WINDOWS = (2, 4, 8, 16)
POOL_GROUP = 128
POOL_HALO = 8
BRANCH_W = 512
N_BRANCH = 4
D_FF = 4 * D_MODEL
N_MOD = 6

LANES = 128

QA_OFF = 0
KA_OFF = QA_OFF + 512
VA_OFF = KA_OFF + 512
CKV_OFF = VA_OFF + 512
CQ_OFF = CKV_OFF + B_KVLORA
KR_OFF = CQ_OFF + B_QLORA
HEAD_W = KR_OFF + LANES
W_IN_HEAD = 3 * 512 + B_QLORA + B_KVLORA
CD_OFF = 0
CD_W = 4 * 512
GATE_OFF = CD_OFF + CD_W
TAIL_W = GATE_OFF + N_BRANCH * D_MODEL
IN_FILL = 256
RAW_QB = CKV_OFF
RAW_KVB = RAW_QB + B_HEADS * LANES
RAW_KR = RAW_KVB + B_HEADS * LANES
RAW_W = RAW_KR + LANES

ATT_SUB = 128
ATT_SUB_BOUNDED = 256
SAFE_LOG2 = 40.0

VMEM_LIMIT = 56 * 1024 * 1024


def _dot(a, b):
    return jnp.dot(a, b, preferred_element_type=F32)


def _dot_nt(a, b):
    return lax.dot_general(a, b, (((1,), (1,)), ((), ())), preferred_element_type=F32)


def _lane_iota(n=LANES):
    return lax.broadcasted_iota(jnp.int32, (1, n), 1)


def _rms(x):
    return x * lax.rsqrt(jnp.mean(x * x, axis=-1, keepdims=True) + EPS)


def _rope(t, tabs, half):
    cos, s_left, s_right = tabs
    n = t.shape[-1]
    return (t * cos + pltpu.roll(t, n - half, 1) * s_left
            + pltpu.roll(t, half, 1) * s_right)


def _layer_spec(l, shape):
    nd = len(shape)
    return pl.BlockSpec((None,) + tuple(shape), lambda *_: (l,) + (0,) * nd,
                        pipeline_mode=pl.Buffered(1))


def _stream_specs(split, tq, d, ctx_tile):
    if split:
        return [pl.BlockSpec((1, tq, d), lambda i, j: (i, jnp.minimum(j, ctx_tile - 1), 0)),
                pl.BlockSpec((1, tq, d), lambda i, j: (i, 0, 0))]
    return [pl.BlockSpec((1, tq, d), lambda i, j: (i, j, 0))]


def _mod_spec(l, b, ctx_tile):
    return pl.BlockSpec((None, None, 1, N_MOD * D_MODEL),
                        lambda i, j: (l, jnp.where(j == ctx_tile, b, i), 0, 0))


def _mod_kernel(c_ref, w_ref, b_ref, o_ref):
    c = c_ref[...]
    h = (c * jax.nn.sigmoid(c)).astype(BF16)
    o_ref[...] = _dot(h, w_ref[...].astype(BF16)) + b_ref[...]


def _modulation(cc, w_mod, b_mod):
    depth, d, n = w_mod.shape
    rows = cc.shape[0]
    tn = 1536
    return pl.pallas_call(
        _mod_kernel,
        grid=(depth, n // tn),
        in_specs=[
            pl.BlockSpec((rows, d), lambda l, j: (0, 0)),
            pl.BlockSpec((None, d, tn), lambda l, j: (l, 0, j)),
            pl.BlockSpec((None, 1, tn), lambda l, j: (l, 0, j)),
        ],
        out_specs=pl.BlockSpec((None, rows, tn), lambda l, j: (l, 0, j)),
        out_shape=jax.ShapeDtypeStruct((depth, rows, n), F32),
        compiler_params=pltpu.CompilerParams(
            dimension_semantics=("parallel", "parallel"), vmem_limit_bytes=VMEM_LIMIT),
        name="modulation",
    )(cc, w_mod, b_mod.reshape(depth, 1, n))


def _inproj_kernel(*refs, nt, ctx_tile, n_tiles, split):
    s = pl.program_id(0)
    n_x = 2 if split else 1
    x_refs, refs = refs[:n_x], refs[n_x:]

    def load_x():
        if split:
            is_ctx = jnp.minimum(s, n_tiles - 1) % nt == ctx_tile
            return jnp.where(is_ctx, x_refs[1][0], x_refs[0][0])
        return x_refs[0][0]

    (mod_ref, g_ref, wh_ref, wt_ref, wuq_ref, wukv_ref, gcq_ref, gckv_ref,
     gqa_ref, gka_ref, gqb_ref, gkb_ref, ca_ref, la_ref, ra_ref, cb_ref, lb_ref, rb_ref,
     gate_ref, cd_ref, qa_ref, ka_ref, vxa_ref, qb_ref, kb_ref, vxb_ref,
     h0, h1, raw0, raw1) = refs
    d = D_MODEL
    tq = h0.shape[0]

    @pl.when(s == 0)
    def _warm_up():
        for ref in (h0, h1, raw0, raw1):
            ref[...] = jnp.zeros(ref.shape, ref.dtype)

    def stages(h_w, h_r, raw_w, raw_r):
        h = h_r[...]
        cq_kr = _dot(h, wh_ref[:, CQ_OFF:HEAD_W])
        ckv = _dot(h, wh_ref[:, CKV_OFF:CKV_OFF + B_KVLORA])
        cqn = (_rms(cq_kr[:, 0:B_QLORA]) * gcq_ref[...]).astype(BF16)
        ckvn = (_rms(ckv) * gckv_ref[...]).astype(BF16)

        def passthrough(o_ref, off, width):
            for c0 in range(0, width, IN_FILL):
                o_ref[0, :, c0:c0 + IN_FILL] = _dot(
                    h, wt_ref[:, off + c0:off + c0 + IN_FILL]).astype(o_ref.dtype)

        passthrough(cd_ref, CD_OFF, CD_W)
        raw_w[:, RAW_QB:RAW_QB + B_HEADS * LANES] = _dot(cqn, wuq_ref[...])
        raw_w[:, RAW_KVB:RAW_KVB + B_HEADS * LANES] = _dot(ckvn, wukv_ref[...])
        for c0 in range(0, CKV_OFF, IN_FILL):
            raw_w[:, c0:c0 + IN_FILL] = _dot(h, wh_ref[:, c0:c0 + IN_FILL])
        raw_w[:, RAW_KR:RAW_KR + LANES] = cq_kr[:, B_QLORA:B_QLORA + LANES]
        passthrough(gate_ref, GATE_OFF, N_BRANCH * d)

        lane = _lane_iota()
        lo = lane < A_HD
        nope = lane < B_NOPE
        rope_a = (ca_ref[...], la_ref[...], ra_ref[...])
        rope_b = (cb_ref[...], lb_ref[...], rb_ref[...])

        gq_b = gqb_ref[...] * (B_QK ** -0.5 * LOG2E)
        for hd in range(B_HEADS):
            q = raw_r[:, RAW_QB + hd * LANES:RAW_QB + (hd + 1) * LANES]
            q2 = q * q
            s_n = jnp.sum(jnp.where(nope, q2, 0.0), axis=-1, keepdims=True)
            s_r = jnp.sum(jnp.where(nope, 0.0, q2), axis=-1, keepdims=True)
            inv = jnp.where(nope, lax.rsqrt(s_n * (1.0 / B_NOPE) + EPS),
                            lax.rsqrt(s_r * (1.0 / B_ROPE) + EPS))
            qb_ref[0, :, hd * LANES:(hd + 1) * LANES] = _rope(
                q * inv * gq_b, rope_b, B_ROPE // 2).astype(BF16)

        kr = raw_r[:, RAW_KR:RAW_KR + LANES]
        krn = kr * lax.rsqrt(jnp.sum(kr * kr, axis=-1, keepdims=True) * (1.0 / B_ROPE) + EPS)
        krn = _rope(krn * jnp.where(nope, 0.0, gkb_ref[...]), rope_b, B_ROPE // 2)
        gk_nope = jnp.where(nope, gkb_ref[...], 0.0)
        for hd in range(B_HEADS):
            kv = raw_r[:, RAW_KVB + hd * LANES:RAW_KVB + (hd + 1) * LANES]
            ss = jnp.sum(jnp.where(nope, kv * kv, 0.0), axis=-1, keepdims=True)
            kn = kv * lax.rsqrt(ss * (1.0 / B_NOPE) + EPS) * gk_nope
            kb_ref[0, :, hd * LANES:(hd + 1) * LANES] = (kn + krn).astype(BF16)
            vxb_ref[0, :, hd * LANES:(hd + 1) * LANES] = jnp.where(nope, 1.0, kv).astype(BF16)

        def norm_halves(t):
            t2 = t * t
            s_lo = jnp.sum(jnp.where(lo, t2, 0.0), axis=-1, keepdims=True)
            s_hi = jnp.sum(jnp.where(lo, 0.0, t2), axis=-1, keepdims=True)
            return t * jnp.where(lo, lax.rsqrt(s_lo * (1.0 / A_HD) + EPS),
                                 lax.rsqrt(s_hi * (1.0 / A_HD) + EPS))

        gq_a = gqa_ref[...] * (A_HD ** -0.5 * LOG2E)
        for off, gain, o_ref in ((QA_OFF, gq_a, qa_ref), (KA_OFF, gka_ref[...], ka_ref)):
            for hd in range(A_HEADS):
                t = raw_r[:, off + hd * LANES:off + (hd + 1) * LANES]
                o_ref[0, :, hd * LANES:(hd + 1) * LANES] = _rope(
                    norm_halves(t) * gain, rope_a, A_HD // 2).astype(BF16)
        ones = jnp.ones((tq, A_VD), BF16)
        for hd in range(A_HEADS):
            vxa_ref[0, :, 2 * hd * A_VD:(2 * hd + 1) * A_VD] = raw_r[
                :, VA_OFF + hd * A_VD:VA_OFF + (hd + 1) * A_VD].astype(BF16)
            vxa_ref[0, :, (2 * hd + 1) * A_VD:(2 * hd + 2) * A_VD] = ones

        y = _rms(load_x()) * g_ref[...]
        h_w[...] = (y * (1.0 + mod_ref[:, d:2 * d]) + mod_ref[:, 0:d]).astype(BF16)

    pl.when(s % 2 == 0)(lambda: stages(h0, h1, raw1, raw0))
    pl.when(s % 2 == 1)(lambda: stages(h1, h0, raw0, raw1))


def _inproj(l, x_parts, mods, p, rope_a, rope_b, tq, n_lat, n_ctx):
    b, _, d = x_parts[0].shape
    t = n_lat + n_ctx
    nt = t // tq
    ctx_tile = n_lat // tq
    n_tiles = b * nt
    split = len(x_parts) == 2
    tile_a = lambda s: jnp.minimum(s, n_tiles - 1)
    tile_b = lambda s: jnp.clip(s - 1, 0, n_tiles - 1)
    tile_c = lambda s: jnp.clip(s - 2, 0, n_tiles - 1)
    tab = pl.BlockSpec((tq, LANES), lambda s: (tile_c(s) % nt, 0))
    vec = lambda n: _layer_spec(l, (1, n))
    if split:
        x_specs = [pl.BlockSpec((1, tq, d), lambda s: (tile_a(s) // nt,
                                                       jnp.minimum(tile_a(s) % nt, ctx_tile - 1), 0)),
                   pl.BlockSpec((1, tq, d), lambda s: (tile_a(s) // nt, 0, 0))]
    else:
        x_specs = [pl.BlockSpec((1, tq, d), lambda s: (tile_a(s) // nt, tile_a(s) % nt, 0))]

    def out(width, dtype, tile):
        return (pl.BlockSpec((1, tq, width), lambda s: (tile(s) // nt, tile(s) % nt, 0)),
                jax.ShapeDtypeStruct((b, t, width), dtype))

    outs = [out(N_BRANCH * d, BF16, tile_b), out(CD_W, BF16, tile_b),
            out(A_HEADS * LANES, BF16, tile_c), out(A_HEADS * LANES, BF16, tile_c),
            out(2 * A_HEADS * A_VD, BF16, tile_c), out(B_HEADS * LANES, BF16, tile_c),
            out(B_HEADS * LANES, BF16, tile_c), out(B_HEADS * LANES, BF16, tile_c)]
    return pl.pallas_call(
        functools.partial(_inproj_kernel, nt=nt, ctx_tile=ctx_tile, n_tiles=n_tiles, split=split),
        grid=(n_tiles + 2,),
        in_specs=x_specs + [
            pl.BlockSpec((None, None, 1, N_MOD * d),
                         lambda s: (l, jnp.where(tile_a(s) % nt == ctx_tile, b, tile_a(s) // nt), 0, 0)),
            vec(d),
            _layer_spec(l, (d, HEAD_W)),
            _layer_spec(l, (d, TAIL_W)),
            _layer_spec(l, (B_QLORA, B_HEADS * LANES)),
            _layer_spec(l, (B_KVLORA, B_HEADS * LANES)),
            vec(B_QLORA), vec(B_KVLORA), vec(LANES), vec(LANES), vec(LANES), vec(LANES),
            tab, tab, tab, tab, tab, tab,
        ],
        out_specs=[o[0] for o in outs],
        out_shape=[o[1] for o in outs],
        scratch_shapes=[pltpu.VMEM((tq, d), BF16), pltpu.VMEM((tq, d), BF16),
                        pltpu.VMEM((tq, RAW_W), F32), pltpu.VMEM((tq, RAW_W), F32)],
        compiler_params=pltpu.CompilerParams(
            dimension_semantics=("arbitrary",), vmem_limit_bytes=VMEM_LIMIT),
        name="inproj",
    )(*x_parts, mods, p["g_norm1"], p["w_head"], p["w_tail"], p["w_uq"], p["w_ukv"],
      p["g_cq"], p["g_ckv"], p["gq_a"], p["gk_a"], p["gq_b"], p["gk_b"], *rope_a, *rope_b)


def _softmax_pv(q, k, vx, shift_max):
    s = _dot_nt(q, k)
    if shift_max:
        s = s - jnp.max(s, axis=-1, keepdims=True)
    return _dot(jnp.exp2(s).astype(BF16), vx)


def _sweep(make_tile, bounded, n_lat, t_all):
    def run(shift_max, sub):
        tile = make_tile(shift_max, sub)
        for s0 in range(0, t_all, sub):
            tile(s0, 0 if s0 < n_lat else n_lat)

    pl.when(bounded)(lambda: run(False, ATT_SUB_BOUNDED))
    pl.when(jnp.logical_not(bounded))(lambda: run(True, ATT_SUB))


def _attn_a_kernel(safe_ref, q_ref, k_ref, vx_ref, lam_ref, gsub_ref, o_ref, *, l, n_lat, lam_init):
    t_all = k_ref.shape[1]
    lo = _lane_iota() < A_HD
    la = lam_ref[...].astype(F32)
    lam = (jnp.exp(jnp.sum(la[0:1] * la[1:2], axis=-1, keepdims=True))
           - jnp.exp(jnp.sum(la[2:3] * la[3:4], axis=-1, keepdims=True)) + lam_init)
    post = gsub_ref[...] * (1.0 - lam_init)

    def make_tile(shift_max, sub):
        def tile(r0, key_lo):
            q = q_ref[0, r0:r0 + sub, :]
            k = k_ref[0, key_lo:t_all, :]
            vx = vx_ref[0, key_lo:t_all, :]
            zero = jnp.zeros_like(q)
            q1, q2 = jnp.where(lo, q, zero), jnp.where(lo, zero, q)
            if shift_max:
                r1 = _softmax_pv(q1, k, vx, True)
                r2 = _softmax_pv(q2, k, vx, True)
            else:
                r = _softmax_pv(jnp.concatenate([q1, q2], axis=0), k, vx, False)
                r1, r2 = r[0:sub], r[sub:2 * sub]
            o = (r1[:, 0:A_VD] / r1[:, A_VD:A_VD + 1]
                 - lam * (r2[:, 0:A_VD] / r2[:, A_VD:A_VD + 1]))
            o_ref[0, r0:r0 + sub, :] = (_rms(o) * post).astype(o_ref.dtype)
        return tile

    _sweep(make_tile, safe_ref[l] != 0, n_lat, t_all)


def _attn_a(l, qa, ka, vxa, p, n_lat, lam_init):
    b, t, _ = qa.shape
    head = lambda i, h: (i, 0, h)
    return pl.pallas_call(
        functools.partial(_attn_a_kernel, l=l, n_lat=n_lat, lam_init=lam_init),
        grid=(b, A_HEADS),
        in_specs=[
            pl.BlockSpec(memory_space=pltpu.SMEM),
            pl.BlockSpec((1, t, LANES), head),
            pl.BlockSpec((1, t, LANES), head),
            pl.BlockSpec((1, t, 2 * A_VD), head),
            _layer_spec(l, (4, A_HD)),
            _layer_spec(l, (1, LANES)),
        ],
        out_specs=pl.BlockSpec((1, t, LANES), head),
        out_shape=jax.ShapeDtypeStruct((b, t, BRANCH_W), BF16),
        compiler_params=pltpu.CompilerParams(
            dimension_semantics=("parallel", "parallel"), vmem_limit_bytes=VMEM_LIMIT),
        name="attn_a",
    )(p["safe_a"], qa, ka, vxa, p["lam_a"], p["g_sub_a"])


def _attn_b_kernel(safe_ref, q_ref, k_ref, vx_ref, o_ref, *, l, n_lat):
    t_all = k_ref.shape[1]
    nope = _lane_iota() < B_NOPE

    def make_tile(shift_max, sub):
        def tile(r0, key_lo):
            outs = []
            for hh in range(2):
                sl = slice(hh * LANES, (hh + 1) * LANES)
                r = _softmax_pv(q_ref[0, r0:r0 + sub, sl], k_ref[0, key_lo:t_all, sl],
                                vx_ref[0, key_lo:t_all, :], shift_max)[:, sl]
                outs.append(r / r[:, 0:1])
            o = jnp.where(nope, pltpu.roll(outs[0], B_VD, 1), outs[1])
            o_ref[0, r0:r0 + sub, :] = o.astype(o_ref.dtype)
        return tile

    _sweep(make_tile, safe_ref[l] != 0, n_lat, t_all)


def _attn_b(l, qb, kb, vxb, p, n_lat):
    b, t, _ = qb.shape
    pair = lambda i, h: (i, 0, h)
    spec = pl.BlockSpec((1, t, 2 * LANES), pair)
    return pl.pallas_call(
        functools.partial(_attn_b_kernel, l=l, n_lat=n_lat),
        grid=(b, B_HEADS // 2),
        in_specs=[pl.BlockSpec(memory_space=pltpu.SMEM), spec, spec, spec],
        out_specs=pl.BlockSpec((1, t, LANES), pair),
        out_shape=jax.ShapeDtypeStruct((b, t, BRANCH_W), BF16),
        compiler_params=pltpu.CompilerParams(
            dimension_semantics=("parallel", "parallel"), vmem_limit_bytes=VMEM_LIMIT),
        name="attn_b",
    )(p["safe_b"], qb, kb, vxb)


def _mix_cd_kernel(u_ref, pb_ref, pc_ref, pxx_ref, wp_ref, sp_ref, wc_ref,
                   yc_ref, yd_ref, pad_s, *, segments):
    g = pl.program_id(1)
    zeros_halo = jnp.zeros((POOL_HALO, LANES), F32)
    wc = wc_ref[...]

    edge = lax.broadcasted_iota(jnp.int32, (POOL_HALO, 1), 0)
    for start, length in segments:
        u = u_ref[0, start:start + length, :].astype(F32)
        pad_s[0:POOL_HALO, :] = zeros_halo
        pad_s[POOL_HALO:POOL_HALO + length, :] = u
        pad_s[POOL_HALO + length:2 * POOL_HALO + length, :] = zeros_halo

        for gi, w in enumerate(POOL_WINDOWS):
            @pl.when(g == gi)
            def _pool(w=w):
                acc = pad_s[POOL_HALO - w // 2:POOL_HALO - w // 2 + length, :]
                for j in range(1 - w // 2, w // 2):
                    acc = acc + pad_s[POOL_HALO + j:POOL_HALO + j + length, :]
                def clipped(r):
                    cnt = jnp.minimum(r + w // 2, length) - jnp.maximum(r - w // 2, 0)
                    return cnt.astype(F32)

                last = length - POOL_HALO
                mean = jnp.concatenate([acc[0:POOL_HALO] / clipped(edge),
                                        acc[POOL_HALO:last] * (1.0 / w),
                                        acc[last:length] / clipped(edge + last)], axis=0)
                dd = mean - u
                y = _dot(dd.astype(BF16), wp_ref[...]) * sp_ref[...]
                yc_ref[0, start:start + length, :] = y.astype(yc_ref.dtype)

        uu = (pc_ref[0, start:start + length, :].astype(F32)
              * pxx_ref[0, start:start + length, :].astype(F32))
        pad_s[POOL_HALO:POOL_HALO + length, :] = uu
        y = (pad_s[POOL_HALO - 1:POOL_HALO - 1 + length, :] * wc[0:1]
             + uu * wc[1:2]
             + pad_s[POOL_HALO + 1:POOL_HALO + 1 + length, :] * wc[2:3])
        yd_ref[0, start:start + length, :] = (
            pb_ref[0, start:start + length, :].astype(F32) * y).astype(yd_ref.dtype)


def _mix_cd(l, cd, p, n_lat):
    b, t, _ = cd.shape
    segments = ((0, n_lat), (n_lat, t - n_lat))
    n_g = len(POOL_WINDOWS)

    def col(k):
        return pl.BlockSpec((1, t, LANES), lambda i, g: (i, 0, k * n_g + g))

    out_spec = pl.BlockSpec((1, t, LANES), lambda i, g: (i, 0, g))
    return pl.pallas_call(
        functools.partial(_mix_cd_kernel, segments=segments),
        grid=(b, n_g),
        in_specs=[
            col(0), col(1), col(2), col(3),
            pl.BlockSpec((None, None, POOL_GROUP, POOL_GROUP), lambda i, g: (l, g, 0, 0)),
            pl.BlockSpec((None, 1, LANES), lambda i, g: (l, 0, g)),
            pl.BlockSpec((None, 3, LANES), lambda i, g: (l, 0, g)),
        ],
        out_specs=[out_spec, out_spec],
        out_shape=[jax.ShapeDtypeStruct((b, t, BRANCH_W), BF16)] * 2,
        scratch_shapes=[pltpu.VMEM((max(n_lat, t - n_lat) + 2 * POOL_HALO, LANES), F32)],
        compiler_params=pltpu.CompilerParams(
            dimension_semantics=("parallel", "parallel"), vmem_limit_bytes=VMEM_LIMIT),
        name="mix_cd",
    )(cd, cd, cd, cd, p["w_pool"], p["s_pool"], p["w_conv"])


def _merge_ffn_kernel(*refs, ctx_tile, split):
    ya_ref, yb_ref, yc_ref, yd_ref, gate_ref = refs[:5]
    if split:
        x_ref, xc_ref = refs[5:7]
        refs = refs[7:]
        x = jnp.where(pl.program_id(1) == ctx_tile, xc_ref[0], x_ref[0])
    else:
        x = refs[5][0]
        refs = refs[6:]
    mod_ref, wb_ref, wo_ref, g2_ref, w1_ref, w2_ref, o_ref = refs
    d = D_MODEL
    merged = None
    for n, y_ref in enumerate((ya_ref, yb_ref, yc_ref, yd_ref)):
        proj = _dot(y_ref[0], wb_ref[n])
        gate = jax.nn.sigmoid(gate_ref[0, :, n * d:(n + 1) * d].astype(F32))
        merged = gate * proj if merged is None else merged + gate * proj
    mix = _dot(merged.astype(BF16), wo_ref[...])
    x1 = x + mod_ref[:, 2 * d:3 * d] * mix

    y = _rms(x1) * g2_ref[...]
    h = (y * (1.0 + mod_ref[:, 4 * d:5 * d]) + mod_ref[:, 3 * d:4 * d]).astype(BF16)
    f = None
    for c0 in range(0, D_FF, d):
        a = jnp.maximum(_dot(h, w1_ref[:, c0:c0 + d]), 0.0)
        part = _dot((a * a).astype(BF16), w2_ref[c0:c0 + d, :])
        f = part if f is None else f + part
    o_ref[0] = x1 + mod_ref[:, 5 * d:6 * d] * f


def _merge_ffn(l, ys, gates, x_parts, mods, p, tq, n_lat, n_ctx, with_ctx):
    b, _, d = x_parts[0].shape
    t = n_lat + n_ctx
    ctx_tile = n_lat // tq
    split = len(x_parts) == 2
    row = lambda i, j: (i, j, 0)
    y_spec = pl.BlockSpec((1, tq, BRANCH_W), row)
    nq, out_rows = (t // tq, t) if with_ctx else (ctx_tile, n_lat)
    aliases = {5: 0} if (with_ctx and not split) else {}
    return pl.pallas_call(
        functools.partial(_merge_ffn_kernel, ctx_tile=ctx_tile, split=split),
        grid=(b, nq),
        in_specs=[
            y_spec, y_spec, y_spec, y_spec,
            pl.BlockSpec((1, tq, N_BRANCH * d), row),
        ] + _stream_specs(split, tq, d, ctx_tile) + [
            _mod_spec(l, b, ctx_tile),
            _layer_spec(l, (N_BRANCH, BRANCH_W, d)),
            _layer_spec(l, (d, d)),
            _layer_spec(l, (1, d)),
            _layer_spec(l, (d, D_FF)),
            _layer_spec(l, (D_FF, d)),
        ],
        out_specs=pl.BlockSpec((1, tq, d), row),
        out_shape=jax.ShapeDtypeStruct((b, out_rows, d), F32),
        input_output_aliases=aliases,
        compiler_params=pltpu.CompilerParams(
            dimension_semantics=("parallel", "parallel"), vmem_limit_bytes=VMEM_LIMIT),
        name="merge_ffn",
    )(*ys, gates, *x_parts, mods, p["w_branch"], p["w_o"], p["g_norm2"], p["w_ff1"], p["w_ff2"])


def _rope_tables(n_lat, n_ctx, rot_dim, lane_lo, period, total=LANES):
    rows = n_lat // GRID_W
    row = np.repeat(np.arange(rows, dtype=np.float64), GRID_W)
    col = np.tile(np.arange(GRID_W, dtype=np.float64), rows)
    n_freq = rot_dim // 4
    inv = ROPE_BASE ** (-np.arange(n_freq, dtype=np.float64) / n_freq)
    inv = inv.astype(np.float32).astype(np.float64)
    ang = np.concatenate([row[:, None] * inv, col[:, None] * inv], axis=-1)
    ang = ang.astype(np.float32).astype(np.float64)
    half = rot_dim // 2
    t = n_lat + n_ctx
    cos = np.ones((t, total), np.float32)
    s_left = np.zeros((t, total), np.float32)
    s_right = np.zeros((t, total), np.float32)
    starts = [lane_lo] if period == 0 else list(range(lane_lo, total, period))
    for s0 in starts:
        cos[:n_lat, s0:s0 + half] = np.cos(ang)
        cos[:n_lat, s0 + half:s0 + rot_dim] = np.cos(ang)
        s_left[:n_lat, s0:s0 + half] = -np.sin(ang)
        s_right[:n_lat, s0 + half:s0 + rot_dim] = np.sin(ang)
    return jnp.asarray(cos), jnp.asarray(s_left), jnp.asarray(s_right)


def _score_bounds(gq_a, gk_a, gq_b, gk_b):
    amax = lambda g: jnp.max(jnp.abs(g), axis=-1)
    bound_a = A_HD * amax(gq_a) * amax(gk_a) * (A_HD ** -0.5 * LOG2E)
    nq = jnp.sqrt(B_NOPE * amax(gq_b[:, :B_NOPE]) ** 2 + B_ROPE * amax(gq_b[:, B_NOPE:]) ** 2)
    nk = jnp.sqrt(B_NOPE * amax(gk_b[:, :B_NOPE]) ** 2 + B_ROPE * amax(gk_b[:, B_NOPE:]) ** 2)
    bound_b = nq * nk * (B_QK ** -0.5 * LOG2E)
    margin = 1.05
    return ((bound_a * margin < SAFE_LOG2).astype(jnp.int32),
            (bound_b * margin < SAFE_LOG2).astype(jnp.int32))


def _head_weights(w_in):
    cq0 = VA_OFF + 512
    wb = w_in[:, :, :W_IN_HEAD + B_ROPE].astype(BF16)
    kr_slot = jnp.pad(wb[:, :, W_IN_HEAD:], ((0, 0), (0, 0), (B_NOPE, LANES - B_QK)))
    return jnp.concatenate([wb[:, :, :cq0], wb[:, :, cq0 + B_QLORA:W_IN_HEAD],
                            wb[:, :, cq0:cq0 + B_QLORA], kr_slot], axis=-1)


def _pad_head_slots(v, width):
    lead = v.shape[:-1]
    v = v.reshape(lead + (-1, width))
    v = jnp.pad(v, [(0, 0)] * len(lead) + [(0, 0), (0, LANES - width)])
    return v.reshape(lead + (-1,))


def kernel(x, c, ctx, c_ctx, w_mod, b_mod, g_norm1, g_norm2, w_in, gq_a, gk_a, lam_a, g_sub_a,
           g_cq, w_uq, g_ckv, w_ukv, gq_b, gk_b, w_pool, s_pool, w_conv, w_branch, w_o,
           w_ff1, w_ff2):
    b, n_lat, d = x.shape
    n_ctx = ctx.shape[1]
    depth = w_mod.shape[0]
    tq = n_ctx
    assert d == D_MODEL and n_lat % ATT_SUB_BOUNDED == 0 and n_ctx % ATT_SUB_BOUNDED == 0
    assert n_lat % tq == 0 and tq % LANES == 0 and n_lat % GRID_W == 0

    rope_a = _rope_tables(n_lat, n_ctx, A_HD, 0, A_HD)
    rope_b = _rope_tables(n_lat, n_ctx, B_ROPE, B_NOPE, 0)

    mod_rows = -(-(b + 1) // 8) * 8
    cc = jnp.concatenate([c, c_ctx[None, :], jnp.zeros((mod_rows - b - 1, d), F32)], axis=0)
    mods = _modulation(cc, w_mod, b_mod).reshape(depth, mod_rows, 1, N_MOD * d)

    vec = lambda a: a[:, None, :]
    safe_a, safe_b = _score_bounds(gq_a, gk_a, gq_b, gk_b)
    p = {
        "safe_a": safe_a, "safe_b": safe_b,
        "w_head": _head_weights(w_in),
        "w_tail": w_in[:, :, W_IN_HEAD + B_ROPE:].astype(BF16),
        "w_uq": _pad_head_slots(w_uq, B_QK).astype(BF16),
        "w_ukv": w_ukv.astype(BF16),
        "w_pool": w_pool.astype(BF16),
        "w_branch": w_branch.astype(BF16),
        "w_o": w_o.astype(BF16),
        "w_ff1": w_ff1.astype(BF16),
        "w_ff2": w_ff2.astype(BF16),
        "g_norm1": vec(g_norm1), "g_norm2": vec(g_norm2),
        "g_cq": vec(g_cq), "g_ckv": vec(g_ckv),
        "gq_a": vec(jnp.tile(gq_a, (1, 2))), "gk_a": vec(jnp.tile(gk_a, (1, 2))),
        "gq_b": vec(jnp.pad(gq_b, ((0, 0), (0, LANES - B_QK)))),
        "gk_b": vec(jnp.pad(gk_b, ((0, 0), (0, LANES - B_QK)))),
        "lam_a": lam_a, "g_sub_a": vec(g_sub_a),
        "s_pool": vec(s_pool), "w_conv": w_conv,
    }

    x_parts = (x, ctx)
    for l in range(depth):
        last = l == depth - 1
        lam_init = 0.8 - 0.6 * math.exp(-0.3 * l)
        gates, cd, qa, ka, vxa, qb, kb, vxb = _inproj(
            l, x_parts, mods, p, rope_a, rope_b, tq, n_lat, n_ctx)
        ya = _attn_a(l, qa, ka, vxa, p, n_lat, lam_init)
        yb = _attn_b(l, qb, kb, vxb, p, n_lat)
        yc, yd = _mix_cd(l, cd, p, n_lat)
        x_parts = (_merge_ffn(l, (ya, yb, yc, yd), gates, x_parts, mods, p, tq, n_lat, n_ctx,
                              not last),)
    return x_parts[0]
```

```python
import functools
import math

import numpy as np
import jax
import jax.numpy as jnp
from jax import lax
from jax.experimental import pallas as pl
from jax.experimental.pallas import tpu as pltpu

F32 = jnp.float32
BF16 = jnp.bfloat16

D_MODEL = 1024
GRID_W = 64
ROPE_BASE = 10000.0
EPS = 1e-6
LOG2E = math.log2(math.e)

A_HEADS = 4
A_HD = 64
A_VD = 128
B_HEADS = 8
B_NOPE = 64
B_ROPE = 32
B_QK = B_NOPE + B_ROPE
B_VD = 64
B_QLORA = 384
B_KVLORA = 256
POOL_WINDOWS = (2, 4, 8, 16)
POOL_GROUP = 128
POOL_HALO = 8
BRANCH_W = 512
N_BRANCH = 4
D_FF = 4 * D_MODEL
N_MOD = 6

LANES = 128

QA_OFF = 0
KA_OFF = QA_OFF + 512
VA_OFF = KA_OFF + 512
CKV_OFF = VA_OFF + 512
CQ_OFF = CKV_OFF + B_KVLORA
KR_OFF = CQ_OFF + B_QLORA
HEAD_W = KR_OFF + LANES
W_IN_HEAD = 3 * 512 + B_QLORA + B_KVLORA
CD_OFF = 0
CD_W = 4 * 512
GATE_OFF = CD_OFF + CD_W
TAIL_W = GATE_OFF + N_BRANCH * D_MODEL
IN_FILL = 256
RAW_QB = CKV_OFF
RAW_KVB = RAW_QB + B_HEADS * LANES
RAW_KR = RAW_KVB + B_HEADS * LANES
RAW_W = RAW_KR + LANES

ATT_SUB = 128
ATT_SUB_BOUNDED = 256
SAFE_LOG2 = 40.0

VMEM_LIMIT = 56 * 1024 * 1024


def _dot(a, b):
    return jnp.dot(a, b, preferred_element_type=F32)


def _dot_nt(a, b):
    return lax.dot_general(a, b, (((1,), (1,)), ((), ())), preferred_element_type=F32)


def _lane_iota(n=LANES):
    return lax.broadcasted_iota(jnp.int32, (1, n), 1)


def _rms(x):
    return x * lax.rsqrt(jnp.mean(x * x, axis=-1, keepdims=True) + EPS)


def _rope(t, tabs, half):
    cos, s_left, s_right = tabs
    n = t.shape[-1]
    return (t * cos + pltpu.roll(t, n - half, 1) * s_left
            + pltpu.roll(t, half, 1) * s_right)


def _layer_spec(l, shape):
    nd = len(shape)
    return pl.BlockSpec((None,) + tuple(shape), lambda *_: (l,) + (0,) * nd,
                        pipeline_mode=pl.Buffered(1))


def _stream_specs(split, tq, d, ctx_tile):
    if split:
        return [pl.BlockSpec((1, tq, d), lambda i, j: (i, jnp.minimum(j, ctx_tile - 1), 0)),
                pl.BlockSpec((1, tq, d), lambda i, j: (i, 0, 0))]
    return [pl.BlockSpec((1, tq, d), lambda i, j: (i, j, 0))]


def _mod_spec(l, b, ctx_tile):
    return pl.BlockSpec((None, None, 1, N_MOD * D_MODEL),
                        lambda i, j: (l, jnp.where(j == ctx_tile, b, i), 0, 0))


def _mod_kernel(c_ref, w_ref, b_ref, o_ref):
    c = c_ref[...]
    h = (c * jax.nn.sigmoid(c)).astype(BF16)
    o_ref[...] = _dot(h, w_ref[...].astype(BF16)) + b_ref[...]


def _modulation(cc, w_mod, b_mod):
    depth, d, n = w_mod.shape
    rows = cc.shape[0]
    tn = 1536
    return pl.pallas_call(
        _mod_kernel,
        grid=(depth, n // tn),
        in_specs=[
            pl.BlockSpec((rows, d), lambda l, j: (0, 0)),
            pl.BlockSpec((None, d, tn), lambda l, j: (l, 0, j)),
            pl.BlockSpec((None, 1, tn), lambda l, j: (l, 0, j)),
        ],
        out_specs=pl.BlockSpec((None, rows, tn), lambda l, j: (l, 0, j)),
        out_shape=jax.ShapeDtypeStruct((depth, rows, n), F32),
        compiler_params=pltpu.CompilerParams(
            dimension_semantics=("parallel", "parallel"), vmem_limit_bytes=VMEM_LIMIT),
        name="modulation",
    )(cc, w_mod, b_mod.reshape(depth, 1, n))


def _inproj_kernel(*refs, nt, ctx_tile, n_tiles, split):
    s = pl.program_id(0)
    n_x = 2 if split else 1
    x_refs, refs = refs[:n_x], refs[n_x:]

    def load_x():
        if split:
            is_ctx = jnp.minimum(s, n_tiles - 1) % nt == ctx_tile
            return jnp.where(is_ctx, x_refs[1][0], x_refs[0][0])
        return x_refs[0][0]

    (mod_ref, g_ref, wh_ref, wt_ref, wuq_ref, wukv_ref, gcq_ref, gckv_ref,
     gqa_ref, gka_ref, gqb_ref, gkb_ref, ca_ref, la_ref, ra_ref, cb_ref, lb_ref, rb_ref,
     gate_ref, cd_ref, qa_ref, ka_ref, vxa_ref, qb_ref, kb_ref, vxb_ref,
     h0, h1, raw0, raw1) = refs
    d = D_MODEL
    tq = h0.shape[0]

    @pl.when(s == 0)
    def _warm_up():
        for ref in (h0, h1, raw0, raw1):
            ref[...] = jnp.zeros(ref.shape, ref.dtype)

    def stages(h_w, h_r, raw_w, raw_r):
        h = h_r[...]
        cq_kr = _dot(h, wh_ref[:, CQ_OFF:HEAD_W])
        ckv = _dot(h, wh_ref[:, CKV_OFF:CKV_OFF + B_KVLORA])
        cqn = (_rms(cq_kr[:, 0:B_QLORA]) * gcq_ref[...]).astype(BF16)
        ckvn = (_rms(ckv) * gckv_ref[...]).astype(BF16)

        def passthrough(o_ref, off, width):
            for c0 in range(0, width, IN_FILL):
                o_ref[0, :, c0:c0 + IN_FILL] = _dot(
                    h, wt_ref[:, off + c0:off + c0 + IN_FILL]).astype(o_ref.dtype)

        passthrough(cd_ref, CD_OFF, CD_W)
        raw_w[:, RAW_QB:RAW_QB + B_HEADS * LANES] = _dot(cqn, wuq_ref[...])
        raw_w[:, RAW_KVB:RAW_KVB + B_HEADS * LANES] = _dot(ckvn, wukv_ref[...])
        for c0 in range(0, CKV_OFF, IN_FILL):
            raw_w[:, c0:c0 + IN_FILL] = _dot(h, wh_ref[:, c0:c0 + IN_FILL])
        raw_w[:, RAW_KR:RAW_KR + LANES] = cq_kr[:, B_QLORA:B_QLORA + LANES]
        passthrough(gate_ref, GATE_OFF, N_BRANCH * d)

        lane = _lane_iota()
        lo = lane < A_HD
        nope = lane < B_NOPE
        rope_a = (ca_ref[...], la_ref[...], ra_ref[...])
        rope_b = (cb_ref[...], lb_ref[...], rb_ref[...])

        gq_b = gqb_ref[...] * (B_QK ** -0.5 * LOG2E)
        for hd in range(B_HEADS):
            q = raw_r[:, RAW_QB + hd * LANES:RAW_QB + (hd + 1) * LANES]
            q2 = q * q
            s_n = jnp.sum(jnp.where(nope, q2, 0.0), axis=-1, keepdims=True)
            s_r = jnp.sum(jnp.where(nope, 0.0, q2), axis=-1, keepdims=True)
            inv = jnp.where(nope, lax.rsqrt(s_n * (1.0 / B_NOPE) + EPS),
                            lax.rsqrt(s_r * (1.0 / B_ROPE) + EPS))
            qb_ref[0, :, hd * LANES:(hd + 1) * LANES] = _rope(
                q * inv * gq_b, rope_b, B_ROPE // 2).astype(BF16)

        kr = raw_r[:, RAW_KR:RAW_KR + LANES]
        krn = kr * lax.rsqrt(jnp.sum(kr * kr, axis=-1, keepdims=True) * (1.0 / B_ROPE) + EPS)
        krn = _rope(krn * jnp.where(nope, 0.0, gkb_ref[...]), rope_b, B_ROPE // 2)
        gk_nope = jnp.where(nope, gkb_ref[...], 0.0)
        for hd in range(B_HEADS):
            kv = raw_r[:, RAW_KVB + hd * LANES:RAW_KVB + (hd + 1) * LANES]
            ss = jnp.sum(jnp.where(nope, kv * kv, 0.0), axis=-1, keepdims=True)
            kn = kv * lax.rsqrt(ss * (1.0 / B_NOPE) + EPS) * gk_nope
            kb_ref[0, :, hd * LANES:(hd + 1) * LANES] = (kn + krn).astype(BF16)
            vxb_ref[0, :, hd * LANES:(hd + 1) * LANES] = jnp.where(nope, 1.0, kv).astype(BF16)

        def norm_halves(t):
            t2 = t * t
            s_lo = jnp.sum(jnp.where(lo, t2, 0.0), axis=-1, keepdims=True)
            s_hi = jnp.sum(jnp.where(lo, 0.0, t2), axis=-1, keepdims=True)
            return t * jnp.where(lo, lax.rsqrt(s_lo * (1.0 / A_HD) + EPS),
                                 lax.rsqrt(s_hi * (1.0 / A_HD) + EPS))

        gq_a = gqa_ref[...] * (A_HD ** -0.5 * LOG2E)
        for off, gain, o_ref in ((QA_OFF, gq_a, qa_ref), (KA_OFF, gka_ref[...], ka_ref)):
            for hd in range(A_HEADS):
                t = raw_r[:, off + hd * LANES:off + (hd + 1) * LANES]
                o_ref[0, :, hd * LANES:(hd + 1) * LANES] = _rope(
                    norm_halves(t) * gain, rope_a, A_HD // 2).astype(BF16)
        ones = jnp.ones((tq, A_VD), BF16)
        for hd in range(A_HEADS):
            vxa_ref[0, :, 2 * hd * A_VD:(2 * hd + 1) * A_VD] = raw_r[
                :, VA_OFF + hd * A_VD:VA_OFF + (hd + 1) * A_VD].astype(BF16)
            vxa_ref[0, :, (2 * hd + 1) * A_VD:(2 * hd + 2) * A_VD] = ones

        y = _rms(load_x()) * g_ref[...]
        h_w[...] = (y * (1.0 + mod_ref[:, d:2 * d]) + mod_ref[:, 0:d]).astype(BF16)

    pl.when(s % 2 == 0)(lambda: stages(h0, h1, raw1, raw0))
    pl.when(s % 2 == 1)(lambda: stages(h1, h0, raw0, raw1))


def _inproj(l, x_parts, mods, p, rope_a, rope_b, tq, n_lat, n_ctx):
    b, _, d = x_parts[0].shape
    t = n_lat + n_ctx
    nt = t // tq
    ctx_tile = n_lat // tq
    n_tiles = b * nt
    split = len(x_parts) == 2
    tile_a = lambda s: jnp.minimum(s, n_tiles - 1)
    tile_b = lambda s: jnp.clip(s - 1, 0, n_tiles - 1)
    tile_c = lambda s: jnp.clip(s - 2, 0, n_tiles - 1)
    tab = pl.BlockSpec((tq, LANES), lambda s: (tile_c(s) % nt, 0))
    vec = lambda n: _layer_spec(l, (1, n))
    if split:
        x_specs = [pl.BlockSpec((1, tq, d), lambda s: (tile_a(s) // nt,
                                                       jnp.minimum(tile_a(s) % nt, ctx_tile - 1), 0)),
                   pl.BlockSpec((1, tq, d), lambda s: (tile_a(s) // nt, 0, 0))]
    else:
        x_specs = [pl.BlockSpec((1, tq, d), lambda s: (tile_a(s) // nt, tile_a(s) % nt, 0))]

    def out(width, dtype, tile):
        return (pl.BlockSpec((1, tq, width), lambda s: (tile(s) // nt, tile(s) % nt, 0)),
                jax.ShapeDtypeStruct((b, t, width), dtype))

    outs = [out(N_BRANCH * d, BF16, tile_b), out(CD_W, BF16, tile_b),
            out(A_HEADS * LANES, BF16, tile_c), out(A_HEADS * LANES, BF16, tile_c),
            out(2 * A_HEADS * A_VD, BF16, tile_c), out(B_HEADS * LANES, BF16, tile_c),
            out(B_HEADS * LANES, BF16, tile_c), out(B_HEADS * LANES, BF16, tile_c)]
    return pl.pallas_call(
        functools.partial(_inproj_kernel, nt=nt, ctx_tile=ctx_tile, n_tiles=n_tiles, split=split),
        grid=(n_tiles + 2,),
        in_specs=x_specs + [
            pl.BlockSpec((None, None, 1, N_MOD * d),
                         lambda s: (l, jnp.where(tile_a(s) % nt == ctx_tile, b, tile_a(s) // nt), 0, 0)),
            vec(d),
            _layer_spec(l, (d, HEAD_W)),
            _layer_spec(l, (d, TAIL_W)),
            _layer_spec(l, (B_QLORA, B_HEADS * LANES)),
            _layer_spec(l, (B_KVLORA, B_HEADS * LANES)),
            vec(B_QLORA), vec(B_KVLORA), vec(LANES), vec(LANES), vec(LANES), vec(LANES),
            tab, tab, tab, tab, tab, tab,
        ],
        out_specs=[o[0] for o in outs],
        out_shape=[o[1] for o in outs],
        scratch_shapes=[pltpu.VMEM((tq, d), BF16), pltpu.VMEM((tq, d), BF16),
                        pltpu.VMEM((tq, RAW_W), F32), pltpu.VMEM((tq, RAW_W), F32)],
        compiler_params=pltpu.CompilerParams(
            dimension_semantics=("arbitrary",), vmem_limit_bytes=VMEM_LIMIT),
        name="inproj",
    )(*x_parts, mods, p["g_norm1"], p["w_head"], p["w_tail"], p["w_uq"], p["w_ukv"],
      p["g_cq"], p["g_ckv"], p["gq_a"], p["gk_a"], p["gq_b"], p["gk_b"], *rope_a, *rope_b)


def _softmax_pv(q, k, vx, shift_max):
    s = _dot_nt(q, k)
    if shift_max:
        s = s - jnp.max(s, axis=-1, keepdims=True)
    return _dot(jnp.exp2(s).astype(BF16), vx)


def _sweep(make_tile, bounded, n_lat, t_all):
    def run(shift_max, sub):
        tile = make_tile(shift_max, sub)
        for s0 in range(0, t_all, sub):
            tile(s0, 0 if s0 < n_lat else n_lat)

    pl.when(bounded)(lambda: run(False, ATT_SUB_BOUNDED))
    pl.when(jnp.logical_not(bounded))(lambda: run(True, ATT_SUB))


def _attn_kernel(safe_ref, qa_ref, ka_ref, vxa_ref, lam_ref, gsub_ref, qb_ref, kb_ref, vxb_ref,
                 oa_ref, ob_ref, *, l, n_lat, lam_init):
    t_all = ka_ref.shape[1]
    lane = _lane_iota()
    lo = lane < A_HD
    nope = lane < B_NOPE
    la = lam_ref[...].astype(F32)
    lam = (jnp.exp(jnp.sum(la[0:1] * la[1:2], axis=-1, keepdims=True))
           - jnp.exp(jnp.sum(la[2:3] * la[3:4], axis=-1, keepdims=True)) + lam_init)
    post = gsub_ref[...] * (1.0 - lam_init)

    def make_tile(shift_max, sub):
        def tile(r0, key_lo):
            q = qa_ref[0, r0:r0 + sub, :]
            k = ka_ref[0, key_lo:t_all, :]
            vx = vxa_ref[0, key_lo:t_all, :]
            zero = jnp.zeros_like(q)
            q1, q2 = jnp.where(lo, q, zero), jnp.where(lo, zero, q)
            if shift_max:
                r1 = _softmax_pv(q1, k, vx, True)
                r2 = _softmax_pv(q2, k, vx, True)
            else:
                r = _softmax_pv(jnp.concatenate([q1, q2], axis=0), k, vx, False)
                r1, r2 = r[0:sub], r[sub:2 * sub]
            o = (r1[:, 0:A_VD] / r1[:, A_VD:A_VD + 1]
                 - lam * (r2[:, 0:A_VD] / r2[:, A_VD:A_VD + 1]))
            oa_ref[0, r0:r0 + sub, :] = (_rms(o) * post).astype(oa_ref.dtype)

            outs = []
            for hh in range(2):
                sl = slice(hh * LANES, (hh + 1) * LANES)
                rb = _softmax_pv(qb_ref[0, r0:r0 + sub, sl], kb_ref[0, key_lo:t_all, sl],
                                 vxb_ref[0, key_lo:t_all, :], shift_max)[:, sl]
                outs.append(rb / rb[:, 0:1])
            ob = jnp.where(nope, pltpu.roll(outs[0], B_VD, 1), outs[1])
            ob_ref[0, r0:r0 + sub, :] = ob.astype(ob_ref.dtype)
        return tile

    _sweep(make_tile, safe_ref[l] != 0, n_lat, t_all)


def _attn(l, qa, ka, vxa, qb, kb, vxb, p, n_lat, lam_init):
    b, t, _ = qa.shape
    head = lambda i, h: (i, 0, h)
    one = pl.BlockSpec((1, t, LANES), head)
    two = pl.BlockSpec((1, t, 2 * LANES), head)
    return pl.pallas_call(
        functools.partial(_attn_kernel, l=l, n_lat=n_lat, lam_init=lam_init),
        grid=(b, A_HEADS),
        in_specs=[
            pl.BlockSpec(memory_space=pltpu.SMEM),
            one, one, two,
            _layer_spec(l, (4, A_HD)),
            _layer_spec(l, (1, LANES)),
            two, two, two,
        ],
        out_specs=[one, one],
        out_shape=[jax.ShapeDtypeStruct((b, t, BRANCH_W), BF16)] * 2,
        compiler_params=pltpu.CompilerParams(
            dimension_semantics=("parallel", "parallel"), vmem_limit_bytes=VMEM_LIMIT),
        name="attn",
    )(p["safe"], qa, ka, vxa, p["lam_a"], p["g_sub_a"], qb, kb, vxb)


def _mix_cd_kernel(u_ref, pb_ref, pc_ref, pxx_ref, wp_ref, sp_ref, wc_ref,
                   yc_ref, yd_ref, pad_s, *, segments):
    g = pl.program_id(1)
    zeros_halo = jnp.zeros((POOL_HALO, LANES), F32)
    wc = wc_ref[...]

    edge = lax.broadcasted_iota(jnp.int32, (POOL_HALO, 1), 0)
    for start, length in segments:
        u = u_ref[0, start:start + length, :].astype(F32)
        pad_s[0:POOL_HALO, :] = zeros_halo
        pad_s[POOL_HALO:POOL_HALO + length, :] = u
        pad_s[POOL_HALO + length:2 * POOL_HALO + length, :] = zeros_halo

        for gi, w in enumerate(POOL_WINDOWS):
            @pl.when(g == gi)
            def _pool(w=w):
                acc = pad_s[POOL_HALO - w // 2:POOL_HALO - w // 2 + length, :]
                for j in range(1 - w // 2, w // 2):
                    acc = acc + pad_s[POOL_HALO + j:POOL_HALO + j + length, :]
                def clipped(r):
                    cnt = jnp.minimum(r + w // 2, length) - jnp.maximum(r - w // 2, 0)
                    return cnt.astype(F32)

                last = length - POOL_HALO
                mean = jnp.concatenate([acc[0:POOL_HALO] / clipped(edge),
                                        acc[POOL_HALO:last] * (1.0 / w),
                                        acc[last:length] / clipped(edge + last)], axis=0)
                dd = mean - u
                y = _dot(dd.astype(BF16), wp_ref[...]) * sp_ref[...]
                yc_ref[0, start:start + length, :] = y.astype(yc_ref.dtype)

        uu = (pc_ref[0, start:start + length, :].astype(F32)
              * pxx_ref[0, start:start + length, :].astype(F32))
        pad_s[POOL_HALO:POOL_HALO + length, :] = uu
        y = (pad_s[POOL_HALO - 1:POOL_HALO - 1 + length, :] * wc[0:1]
             + uu * wc[1:2]
             + pad_s[POOL_HALO + 1:POOL_HALO + 1 + length, :] * wc[2:3])
        yd_ref[0, start:start + length, :] = (
            pb_ref[0, start:start + length, :].astype(F32) * y).astype(yd_ref.dtype)


def _mix_cd(l, cd, p, n_lat):
    b, t, _ = cd.shape
    segments = ((0, n_lat), (n_lat, t - n_lat))
    n_g = len(POOL_WINDOWS)

    def col(k):
        return pl.BlockSpec((1, t, LANES), lambda i, g: (i, 0, k * n_g + g))

    out_spec = pl.BlockSpec((1, t, LANES), lambda i, g: (i, 0, g))
    return pl.pallas_call(
        functools.partial(_mix_cd_kernel, segments=segments),
        grid=(b, n_g),
        in_specs=[
            col(0), col(1), col(2), col(3),
            pl.BlockSpec((None, None, POOL_GROUP, POOL_GROUP), lambda i, g: (l, g, 0, 0)),
            pl.BlockSpec((None, 1, LANES), lambda i, g: (l, 0, g)),
            pl.BlockSpec((None, 3, LANES), lambda i, g: (l, 0, g)),
        ],
        out_specs=[out_spec, out_spec],
        out_shape=[jax.ShapeDtypeStruct((b, t, BRANCH_W), BF16)] * 2,
        scratch_shapes=[pltpu.VMEM((max(n_lat, t - n_lat) + 2 * POOL_HALO, LANES), F32)],
        compiler_params=pltpu.CompilerParams(
            dimension_semantics=("parallel", "parallel"), vmem_limit_bytes=VMEM_LIMIT),
        name="mix_cd",
    )(cd, cd, cd, cd, p["w_pool"], p["s_pool"], p["w_conv"])


def _merge_ffn_kernel(*refs, ctx_tile, split):
    ya_ref, yb_ref, yc_ref, yd_ref, gate_ref = refs[:5]
    if split:
        x_ref, xc_ref = refs[5:7]
        refs = refs[7:]
        x = jnp.where(pl.program_id(1) == ctx_tile, xc_ref[0], x_ref[0])
    else:
        x = refs[5][0]
        refs = refs[6:]
    mod_ref, wb_ref, wo_ref, g2_ref, w1_ref, w2_ref, o_ref = refs
    d = D_MODEL
    merged = None
    for n, y_ref in enumerate((ya_ref, yb_ref, yc_ref, yd_ref)):
        proj = _dot(y_ref[0], wb_ref[n])
        gate = jax.nn.sigmoid(gate_ref[0, :, n * d:(n + 1) * d].astype(F32))
        merged = gate * proj if merged is None else merged + gate * proj
    mix = _dot(merged.astype(BF16), wo_ref[...])
    x1 = x + mod_ref[:, 2 * d:3 * d] * mix

    y = _rms(x1) * g2_ref[...]
    h = (y * (1.0 + mod_ref[:, 4 * d:5 * d]) + mod_ref[:, 3 * d:4 * d]).astype(BF16)
    f = None
    for c0 in range(0, D_FF, d):
        a = jnp.maximum(_dot(h, w1_ref[:, c0:c0 + d]), 0.0)
        part = _dot((a * a).astype(BF16), w2_ref[c0:c0 + d, :])
        f = part if f is None else f + part
    o_ref[0] = x1 + mod_ref[:, 5 * d:6 * d] * f


def _merge_ffn(l, ys, gates, x_parts, mods, p, tq, n_lat, n_ctx, with_ctx):
    b, _, d = x_parts[0].shape
    t = n_lat + n_ctx
    ctx_tile = n_lat // tq
    split = len(x_parts) == 2
    row = lambda i, j: (i, j, 0)
    y_spec = pl.BlockSpec((1, tq, BRANCH_W), row)
    nq, out_rows = (t // tq, t) if with_ctx else (ctx_tile, n_lat)
    aliases = {5: 0} if (with_ctx and not split) else {}
    return pl.pallas_call(
        functools.partial(_merge_ffn_kernel, ctx_tile=ctx_tile, split=split),
        grid=(b, nq),
        in_specs=[
            y_spec, y_spec, y_spec, y_spec,
            pl.BlockSpec((1, tq, N_BRANCH * d), row),
        ] + _stream_specs(split, tq, d, ctx_tile) + [
            _mod_spec(l, b, ctx_tile),
            _layer_spec(l, (N_BRANCH, BRANCH_W, d)),
            _layer_spec(l, (d, d)),
            _layer_spec(l, (1, d)),
            _layer_spec(l, (d, D_FF)),
            _layer_spec(l, (D_FF, d)),
        ],
        out_specs=pl.BlockSpec((1, tq, d), row),
        out_shape=jax.ShapeDtypeStruct((b, out_rows, d), F32),
        input_output_aliases=aliases,
        compiler_params=pltpu.CompilerParams(
            dimension_semantics=("parallel", "parallel"), vmem_limit_bytes=VMEM_LIMIT),
        name="merge_ffn",
    )(*ys, gates, *x_parts, mods, p["w_branch"], p["w_o"], p["g_norm2"], p["w_ff1"], p["w_ff2"])


def _rope_tables(n_lat, n_ctx, rot_dim, lane_lo, period, total=LANES):
    rows = n_lat // GRID_W
    row = np.repeat(np.arange(rows, dtype=np.float64), GRID_W)
    col = np.tile(np.arange(GRID_W, dtype=np.float64), rows)
    n_freq = rot_dim // 4
    inv = ROPE_BASE ** (-np.arange(n_freq, dtype=np.float64) / n_freq)
    inv = inv.astype(np.float32).astype(np.float64)
    ang = np.concatenate([row[:, None] * inv, col[:, None] * inv], axis=-1)
    ang = ang.astype(np.float32).astype(np.float64)
    half = rot_dim // 2
    t = n_lat + n_ctx
    cos = np.ones((t, total), np.float32)
    s_left = np.zeros((t, total), np.float32)
    s_right = np.zeros((t, total), np.float32)
    starts = [lane_lo] if period == 0 else list(range(lane_lo, total, period))
    for s0 in starts:
        cos[:n_lat, s0:s0 + half] = np.cos(ang)
        cos[:n_lat, s0 + half:s0 + rot_dim] = np.cos(ang)
        s_left[:n_lat, s0:s0 + half] = -np.sin(ang)
        s_right[:n_lat, s0 + half:s0 + rot_dim] = np.sin(ang)
    return jnp.asarray(cos), jnp.asarray(s_left), jnp.asarray(s_right)


def _score_bounds(gq_a, gk_a, gq_b, gk_b):
    amax = lambda g: jnp.max(jnp.abs(g), axis=-1)
    bound_a = A_HD * amax(gq_a) * amax(gk_a) * (A_HD ** -0.5 * LOG2E)
    nq = jnp.sqrt(B_NOPE * amax(gq_b[:, :B_NOPE]) ** 2 + B_ROPE * amax(gq_b[:, B_NOPE:]) ** 2)
    nk = jnp.sqrt(B_NOPE * amax(gk_b[:, :B_NOPE]) ** 2 + B_ROPE * amax(gk_b[:, B_NOPE:]) ** 2)
    bound_b = nq * nk * (B_QK ** -0.5 * LOG2E)
    margin = 1.05
    return ((bound_a * margin < SAFE_LOG2).astype(jnp.int32),
            (bound_b * margin < SAFE_LOG2).astype(jnp.int32))


def _head_weights(w_in):
    cq0 = VA_OFF + 512
    wb = w_in[:, :, :W_IN_HEAD + B_ROPE].astype(BF16)
    kr_slot = jnp.pad(wb[:, :, W_IN_HEAD:], ((0, 0), (0, 0), (B_NOPE, LANES - B_QK)))
    return jnp.concatenate([wb[:, :, :cq0], wb[:, :, cq0 + B_QLORA:W_IN_HEAD],
                            wb[:, :, cq0:cq0 + B_QLORA], kr_slot], axis=-1)


def _pad_head_slots(v, width):
    lead = v.shape[:-1]
    v = v.reshape(lead + (-1, width))
    v = jnp.pad(v, [(0, 0)] * len(lead) + [(0, 0), (0, LANES - width)])
    return v.reshape(lead + (-1,))


def kernel(x, c, ctx, c_ctx, w_mod, b_mod, g_norm1, g_norm2, w_in, gq_a, gk_a, lam_a, g_sub_a,
           g_cq, w_uq, g_ckv, w_ukv, gq_b, gk_b, w_pool, s_pool, w_conv, w_branch, w_o,
           w_ff1, w_ff2):
    b, n_lat, d = x.shape
    n_ctx = ctx.shape[1]
    depth = w_mod.shape[0]
    tq = n_ctx
    assert d == D_MODEL and n_lat % ATT_SUB_BOUNDED == 0 and n_ctx % ATT_SUB_BOUNDED == 0
    assert n_lat % tq == 0 and tq % LANES == 0 and n_lat % GRID_W == 0

    rope_a = _rope_tables(n_lat, n_ctx, A_HD, 0, A_HD)
    rope_b = _rope_tables(n_lat, n_ctx, B_ROPE, B_NOPE, 0)

    mod_rows = -(-(b + 1) // 8) * 8
    cc = jnp.concatenate([c, c_ctx[None, :], jnp.zeros((mod_rows - b - 1, d), F32)], axis=0)
    mods = _modulation(cc, w_mod, b_mod).reshape(depth, mod_rows, 1, N_MOD * d)

    vec = lambda a: a[:, None, :]
    safe_a, safe_b = _score_bounds(gq_a, gk_a, gq_b, gk_b)
    p = {
        "safe": safe_a * safe_b,
        "w_head": _head_weights(w_in),
        "w_tail": w_in[:, :, W_IN_HEAD + B_ROPE:].astype(BF16),
        "w_uq": _pad_head_slots(w_uq, B_QK).astype(BF16),
        "w_ukv": w_ukv.astype(BF16),
        "w_pool": w_pool.astype(BF16),
        "w_branch": w_branch.astype(BF16),
        "w_o": w_o.astype(BF16),
        "w_ff1": w_ff1.astype(BF16),
        "w_ff2": w_ff2.astype(BF16),
        "g_norm1": vec(g_norm1), "g_norm2": vec(g_norm2),
        "g_cq": vec(g_cq), "g_ckv": vec(g_ckv),
        "gq_a": vec(jnp.tile(gq_a, (1, 2))), "gk_a": vec(jnp.tile(gk_a, (1, 2))),
        "gq_b": vec(jnp.pad(gq_b, ((0, 0), (0, LANES - B_QK)))),
        "gk_b": vec(jnp.pad(gk_b, ((0, 0), (0, LANES - B_QK)))),
        "lam_a": lam_a, "g_sub_a": vec(g_sub_a),
        "s_pool": vec(s_pool), "w_conv": w_conv,
    }

    x_parts = (x, ctx)
    for l in range(depth):
        last = l == depth - 1
        lam_init = 0.8 - 0.6 * math.exp(-0.3 * l)
        gates, cd, qa, ka, vxa, qb, kb, vxb = _inproj(
            l, x_parts, mods, p, rope_a, rope_b, tq, n_lat, n_ctx)
        ya, yb = _attn(l, qa, ka, vxa, qb, kb, vxb, p, n_lat, lam_init)
        yc, yd = _mix_cd(l, cd, p, n_lat)
        x_parts = (_merge_ffn(l, (ya, yb, yc, yd), gates, x_parts, mods, p, tq, n_lat, n_ctx,
                              not last),)
    return x_parts[0]
```
